```python
import jax, jax.numpy as jnp
from jax import lax
import numpy as np

D_MODEL = 2048
BATCH = 4
SEQ = 2048
DEPTH = 4
DEC_BATCH = 32
DEC_SEQ = 8
PAST_LEN = 16384
PAGE_SIZE = 128

N_A_LAYERS = DEPTH // 2
N_B_LAYERS = DEPTH - N_A_LAYERS
CONV_A_WIDTH = 31
FFN_CONV_WIDTH = 3
D_FF = ((8 * D_MODEL // 3 + 255) // 256) * 256
HEAD_DIM = 64
N_HEADS = D_MODEL // HEAD_DIM
N_KV_HEADS = max(1, N_HEADS // 8)
GROUP = N_HEADS // N_KV_HEADS
WINDOW = 128
BLOCK = WINDOW
DN_ALPHA = (2.0 * DEPTH) ** 0.25
DN_BETA = (8.0 * DEPTH) ** -0.25
LN_EPS = 1e-5
NEG_BIG = -1e30

kernel_name = 'yoco_conformer_swa_sink_decoder_step'


def layer_norm(x, g, b):
    xf = x.astype(jnp.float32)
    mu = jnp.mean(xf, axis=-1, keepdims=True)
    var = jnp.mean(jnp.square(xf - mu), axis=-1, keepdims=True)
    y = (xf - mu) * lax.rsqrt(var + LN_EPS) * g.astype(jnp.float32) + b.astype(jnp.float32)
    return y.astype(x.dtype)


def causal_dwconv(u, prev, w, bias):
    ext = jnp.concatenate([prev.astype(u.dtype), u], axis=1)
    y = lax.conv_general_dilated(ext, w[:, None, :].astype(u.dtype), window_strides=(1,), padding='VALID',
                                 dimension_numbers=('NWC', 'WIO', 'NWC'), feature_group_count=u.shape[-1])
    return y + bias, ext[:, -(w.shape[0] - 1):]


def conformer_conv(x, prev, w_in, b_in, w_dw, b_dw, g_n, b_n, w_out, b_out):
    a, gate = jnp.split(x @ w_in + b_in, 2, axis=-1)
    u = a * jax.nn.sigmoid(gate)
    u, new_prev = causal_dwconv(u, prev, w_dw, b_dw)
    u = jax.nn.silu(layer_norm(u, g_n, b_n))
    return u @ w_out + b_out, new_prev


def conv_ffn(x, prev, w_up, w_dw, b_dw, w_down):
    g, v = jnp.split(x @ w_up, 2, axis=-1)
    g, new_prev = causal_dwconv(g, prev, w_dw, b_dw)
    return (jax.nn.gelu(g, approximate=False) * v) @ w_down, new_prev


def alibi_slopes():
    return 2.0 ** (-8.0 * jnp.arange(1, N_HEADS + 1, dtype=jnp.float32) / N_HEADS)


def _band(a, nb):
    a = a.reshape((a.shape[0], nb + 1, BLOCK) + a.shape[2:])
    return jnp.concatenate([a[:, :-1], a[:, 1:]], axis=2)


def sink_window_attention(q, k_ext, v_ext, key_valid, sinks):
    n, t = q.shape[:2]
    if t % BLOCK == 0:
        nb, qb = t // BLOCK, BLOCK
        kb, vb = _band(k_ext, nb), _band(v_ext, nb)
        valid = _band(key_valid[None], nb)[0]
    else:
        nb, qb = 1, t
        kb, vb = k_ext[:, None], v_ext[:, None]
        valid = key_valid[None]
    qg = q.reshape(n, nb, qb, N_KV_HEADS, GROUP, HEAD_DIM)
    s = jnp.einsum('nbqkgd,nbskd->nbkgqs', qg, kb, preferred_element_type=jnp.float32) * (HEAD_DIM ** -0.5)
    dist = WINDOW + jnp.arange(qb)[:, None] - jnp.arange(kb.shape[2])[None, :]
    slopes = alibi_slopes().reshape(N_KV_HEADS, GROUP)[:, :, None, None]
    s = s - slopes * dist.astype(jnp.float32)
    mask = ((dist >= 0) & (dist <= WINDOW))[None] & valid[:, None, :]
    s = jnp.where(mask[:, None, None], s, NEG_BIG)
    sink = sinks.astype(jnp.float32).reshape(N_KV_HEADS, GROUP)[:, :, None, None]
    m = jnp.maximum(jnp.max(s, axis=-1, keepdims=True), sink)
    p = jnp.exp(s - m)
    p = (p / (jnp.sum(p, axis=-1, keepdims=True) + jnp.exp(sink - m))).astype(v_ext.dtype)
    o = jnp.einsum('nbkgqs,nbskd->nbqkgd', p, vb)
    return o.reshape(n, t, N_HEADS * HEAD_DIM)


def trunk(x, conv_a_prev, ffn_prev, k_prev, v_prev, prev_valid, params):
    (a_w_in, a_b_in, a_w_dw, a_b_dw, a_norm_g, a_norm_b, a_w_out, a_b_out, w_kv, b_w_q, b_sinks, b_w_o,
     f_w_up, f_w_dw, f_b_dw, f_w_down, ln_mix_g, ln_mix_b, ln_ffn_g, ln_ffn_b) = params
    n, t = x.shape[:2]
    new_conv_a, new_ffn = [], []
    k_ext = v_ext = key_valid = None
    for l in range(DEPTH):
        if l < N_A_LAYERS:
            mix, st = conformer_conv(x, conv_a_prev[l], a_w_in[l], a_b_in[l], a_w_dw[l], a_b_dw[l],
                                     a_norm_g[l], a_norm_b[l], a_w_out[l], a_b_out[l])
            new_conv_a.append(st)
        else:
            j = l - N_A_LAYERS
            q = (x @ b_w_q[j]).reshape(n, t, N_HEADS, HEAD_DIM)
            mix = sink_window_attention(q, k_ext, v_ext, key_valid, b_sinks[j]) @ b_w_o[j]
        x = layer_norm(DN_ALPHA * x + mix, ln_mix_g[l], ln_mix_b[l])
        f, st = conv_ffn(x, ffn_prev[l], f_w_up[l], f_w_dw[l], f_b_dw[l], f_w_down[l])
        new_ffn.append(st)
        x = layer_norm(DN_ALPHA * x + f, ln_ffn_g[l], ln_ffn_b[l])
        if l == N_A_LAYERS - 1:
            kv = (x @ w_kv).reshape(n, t, 2, N_KV_HEADS, HEAD_DIM)
            k_ext = jnp.concatenate([k_prev.astype(x.dtype), kv[:, :, 0]], axis=1)
            v_ext = jnp.concatenate([v_prev.astype(x.dtype), kv[:, :, 1]], axis=1)
            key_valid = jnp.concatenate([jnp.full((WINDOW,), prev_valid), jnp.ones((t,), dtype=bool)])
    return x, jnp.stack(new_conv_a), jnp.stack(new_ffn), k_ext[:, -WINDOW:], v_ext[:, -WINDOW:]


def setup_inputs(seed: int = 0) -> dict:
    key = jax.random.key(seed)
    ks = jax.random.split(key, 32)
    nrm = lambda k, shape, scale: jax.random.normal(k, shape, dtype=jnp.float32) * scale
    D, F, KVW = D_MODEL, D_FF, N_KV_HEADS * HEAD_DIM
    return {
        'x_prompt': nrm(ks[0], (BATCH, SEQ, D), 1.0),
        'x_sample': nrm(ks[1], (DEC_BATCH, DEC_SEQ, D), 1.0),
        'state_conv_a': nrm(ks[2], (N_A_LAYERS, DEC_BATCH, CONV_A_WIDTH - 1, D), 0.5),
        'state_ffn_conv': nrm(ks[3], (DEPTH, DEC_BATCH, FFN_CONV_WIDTH - 1, F), 1.0),
        'cache_k_win': nrm(ks[4], (DEC_BATCH, WINDOW, N_KV_HEADS, HEAD_DIM), 1.0),
        'cache_v_win': nrm(ks[5], (DEC_BATCH, WINDOW, N_KV_HEADS, HEAD_DIM), 1.0),
        'a_w_in': nrm(ks[6], (N_A_LAYERS, D, 2 * D), D ** -0.5),
        'a_b_in': nrm(ks[7], (N_A_LAYERS, 2 * D), 0.02),
        'a_w_dw': nrm(ks[8], (N_A_LAYERS, CONV_A_WIDTH, D), CONV_A_WIDTH ** -0.5),
        'a_b_dw': nrm(ks[9], (N_A_LAYERS, D), 0.02),
        'a_norm_g': 1.0 + nrm(ks[10], (N_A_LAYERS, D), 0.02),
        'a_norm_b': nrm(ks[11], (N_A_LAYERS, D), 0.02),
        'a_w_out': nrm(ks[12], (N_A_LAYERS, D, D), D ** -0.5 * DN_BETA),
        'a_b_out': nrm(ks[13], (N_A_LAYERS, D), 0.02),
        'w_kv': nrm(ks[14], (D, 2 * KVW), D ** -0.5),
        'b_w_q': nrm(ks[15], (N_B_LAYERS, D, N_HEADS * HEAD_DIM), D ** -0.5),
        'b_sinks': nrm(ks[16], (N_B_LAYERS, N_HEADS), 0.5),
        'b_w_o': nrm(ks[17], (N_B_LAYERS, N_HEADS * HEAD_DIM, D), (N_HEADS * HEAD_DIM) ** -0.5 * DN_BETA),
        'f_w_up': nrm(ks[18], (DEPTH, D, 2 * F), D ** -0.5),
        'f_w_dw': nrm(ks[19], (DEPTH, FFN_CONV_WIDTH, F), FFN_CONV_WIDTH ** -0.5),
        'f_b_dw': nrm(ks[20], (DEPTH, F), 0.02),
        'f_w_down': nrm(ks[21], (DEPTH, F, D), F ** -0.5 * DN_BETA),
        'ln_mix_g': 1.0 + nrm(ks[22], (DEPTH, D), 0.02),
        'ln_mix_b': nrm(ks[23], (DEPTH, D), 0.02),
        'ln_ffn_g': 1.0 + nrm(ks[24], (DEPTH, D), 0.02),
        'ln_ffn_b': nrm(ks[25], (DEPTH, D), 0.02),
    }


def reference(x_prompt, x_sample, state_conv_a, state_ffn_conv, cache_k_win, cache_v_win,
              a_w_in, a_b_in, a_w_dw, a_b_dw, a_norm_g, a_norm_b, a_w_out, a_b_out,
              w_kv, b_w_q, b_sinks, b_w_o, f_w_up, f_w_dw, f_b_dw, f_w_down,
              ln_mix_g, ln_mix_b, ln_ffn_g, ln_ffn_b):
    params = (a_w_in, a_b_in, a_w_dw, a_b_dw, a_norm_g, a_norm_b, a_w_out, a_b_out, w_kv, b_w_q, b_sinks, b_w_o,
              f_w_up, f_w_dw, f_b_dw, f_w_down, ln_mix_g, ln_mix_b, ln_ffn_g, ln_ffn_b)
    bp, dt = x_prompt.shape[0], x_prompt.dtype
    zero_conv_a = jnp.zeros((N_A_LAYERS, bp, CONV_A_WIDTH - 1, D_MODEL), dt)
    zero_ffn = jnp.zeros((DEPTH, bp, FFN_CONV_WIDTH - 1, D_FF), dt)
    zero_kv = jnp.zeros((bp, WINDOW, N_KV_HEADS, HEAD_DIM), dt)
    y_prompt, conv_a_prompt, ffn_conv_prompt, k_win_prompt, v_win_prompt = trunk(
        x_prompt, zero_conv_a, zero_ffn, zero_kv, zero_kv, False, params)
    y_sample, conv_a_sample, ffn_conv_sample, k_win_sample, v_win_sample = trunk(
        x_sample, state_conv_a, state_ffn_conv, cache_k_win, cache_v_win, True, params)
    return (y_prompt, y_sample, conv_a_prompt, ffn_conv_prompt, k_win_prompt, v_win_prompt,
            conv_a_sample, ffn_conv_sample, k_win_sample, v_win_sample)
```

```python
import functools

import jax
import jax.numpy as jnp
from jax import lax
from jax.experimental import pallas as pl
from jax.experimental.pallas import tpu as pltpu

D_MODEL = 2048
BATCH = 4
SEQ = 2048
DEPTH = 4
DEC_BATCH = 32
DEC_SEQ = 8
N_A_LAYERS = DEPTH // 2
CONV_A_WIDTH = 31
FFN_CONV_WIDTH = 3
D_FF = ((8 * D_MODEL // 3 + 255) // 256) * 256
HEAD_DIM = 64
N_HEADS = D_MODEL // HEAD_DIM
N_KV_HEADS = max(1, N_HEADS // 8)
GROUP = N_HEADS // N_KV_HEADS
KV_WIDTH = N_KV_HEADS * HEAD_DIM
WINDOW = 128
DN_ALPHA = (2.0 * DEPTH) ** 0.25
LN_EPS = 1e-5
NEG_BIG = -1e30
ALIBI_SLOPES = tuple(2.0 ** (-8.0 * (h + 1) / N_HEADS) for h in range(N_HEADS))

BF16 = jnp.bfloat16
F32 = jnp.float32

V7X_VMEM_BYTES = 64 * 1024 * 1024
SUBLANES = 8

PROMPT_ROWS = BATCH * SEQ
SAMPLE_ROWS = DEC_BATCH * DEC_SEQ
PROMPT_TM = 1024
FFN_TF = 256
CONV_TM = 256
CONV_TC = 256
CONV_HALO = 32


def _params(semantics, vmem_bytes):
    limit = min(int(vmem_bytes * 1.25) + (4 << 20), V7X_VMEM_BYTES - (6 << 20))
    return pltpu.CompilerParams(dimension_semantics=semantics, vmem_limit_bytes=limit)


def _layer_norm(z, g, b):
    mu = jnp.mean(z, axis=-1, keepdims=True)
    zc = z - mu
    var = jnp.mean(zc * zc, axis=-1, keepdims=True)
    return zc * lax.rsqrt(var + LN_EPS) * g + b


def _gelu_exact(z):
    return 0.5 * z * (1.0 + lax.erf(z * (0.5 ** 0.5)))


def _single(shape, index_map):
    return pl.BlockSpec(shape, index_map, pipeline_mode=pl.Buffered(1))


def _mm_kernel(x_ref, w_ref, o_ref, xb_ref, *, scale):
    @pl.when(pl.program_id(1) == 0)
    def _():
        xb_ref[...] = x_ref[...].astype(BF16)

    acc = jnp.dot(xb_ref[...], w_ref[...].astype(BF16), preferred_element_type=F32)
    if scale != 1.0:
        acc = acc * scale
    o_ref[...] = acc.astype(o_ref.dtype)


def _matmul(x, w, *, tm, tn, out_dtype, scale=1.0, name):
    m, k = x.shape
    n = w.shape[1]
    vmem = tm * k * 4 + tm * k * 2 + 2 * k * tn * 4 + k * tn * 2 + 3 * tm * tn * 4
    return pl.pallas_call(
        functools.partial(_mm_kernel, scale=scale),
        grid=(m // tm, n // tn),
        in_specs=[_single((tm, k), lambda i, j: (i, 0)),
                  pl.BlockSpec((k, tn), lambda i, j: (0, j))],
        out_specs=pl.BlockSpec((tm, tn), lambda i, j: (i, j)),
        out_shape=jax.ShapeDtypeStruct((m, n), out_dtype),
        scratch_shapes=[pltpu.VMEM((tm, k), BF16)],
        compiler_params=_params(("arbitrary", "arbitrary"), vmem),
        name=name,
    )(x, w)


def _glu_kernel(x_ref, wa_ref, wg_ref, ba_ref, bg_ref, u_ref, xb_ref):
    @pl.when(pl.program_id(1) == 0)
    def _():
        xb_ref[...] = x_ref[...].astype(BF16)

    xb = xb_ref[...]
    a = jnp.dot(xb, wa_ref[...].astype(BF16), preferred_element_type=F32) + ba_ref[...]
    g = jnp.dot(xb, wg_ref[...].astype(BF16), preferred_element_type=F32) + bg_ref[...]
    u_ref[...] = a * jax.nn.sigmoid(g)


def _glu_proj(x, w_in, b_in, *, tm, tn, name):
    m, k = x.shape
    n = w_in.shape[1] // 2
    nj = n // tn
    b2 = b_in.reshape(1, 2 * n)
    vmem = tm * k * 4 + tm * k * 2 + 4 * k * tn * 4 + 2 * k * tn * 2 + 5 * tm * tn * 4
    return pl.pallas_call(
        _glu_kernel,
        grid=(m // tm, nj),
        in_specs=[_single((tm, k), lambda i, j: (i, 0)),
                  pl.BlockSpec((k, tn), lambda i, j: (0, j)),
                  pl.BlockSpec((k, tn), lambda i, j: (0, j + nj)),
                  pl.BlockSpec((1, tn), lambda i, j: (0, j)),
                  pl.BlockSpec((1, tn), lambda i, j: (0, j + nj))],
        out_specs=pl.BlockSpec((tm, tn), lambda i, j: (i, j)),
        out_shape=jax.ShapeDtypeStruct((m, n), F32),
        scratch_shapes=[pltpu.VMEM((tm, k), BF16)],
        compiler_params=_params(("arbitrary", "arbitrary"), vmem),
        name=name,
    )(x, w_in, w_in, b2, b2)


def _conv_norm_kernel(*refs, tm, tc, halo, stride, tiles_per_seq):
    if tiles_per_seq > 1:
        u_ref, uh_ref, prev_ref, w_ref, b_ref, gn_ref, bn_ref, s_ref, ext_ref, c_ref = refs
    else:
        u_ref, prev_ref, w_ref, b_ref, gn_ref, bn_ref, s_ref, ext_ref, c_ref = refs
    i = pl.program_id(0)
    j = pl.program_id(1)
    nj = pl.num_programs(1)

    if tiles_per_seq > 1:
        @pl.when(i % tiles_per_seq == 0)
        def _():
            ext_ref[0:halo, :] = prev_ref[...]

        @pl.when(i % tiles_per_seq != 0)
        def _():
            ext_ref[0:halo, :] = uh_ref[...]
    else:
        ext_ref[0:halo, :] = prev_ref[...]
    ext_ref[halo:halo + tm, :] = u_ref[...]

    bias = jnp.broadcast_to(b_ref[...], (SUBLANES, tc))
    first = halo - (CONV_A_WIDTH - 1) * stride
    for rb in range(tm // SUBLANES):
        acc = bias
        for k in range(CONV_A_WIDTH):
            off = first + k * stride + rb * SUBLANES
            acc = acc + w_ref[k * SUBLANES:(k + 1) * SUBLANES, :] * ext_ref[off:off + SUBLANES, :]
        c_ref[j, rb * SUBLANES:(rb + 1) * SUBLANES, :] = acc

    @pl.when(j == nj - 1)
    def _():
        n_c = c_ref.shape[0]
        d = n_c * tc
        tot = jnp.sum(c_ref[0], axis=-1, keepdims=True)
        for jj in range(1, n_c):
            tot = tot + jnp.sum(c_ref[jj], axis=-1, keepdims=True)
        mu = tot * (1.0 / d)
        sq = None
        for jj in range(n_c):
            zc = c_ref[jj] - mu
            part = jnp.sum(zc * zc, axis=-1, keepdims=True)
            sq = part if sq is None else sq + part
        rs = lax.rsqrt(sq * (1.0 / d) + LN_EPS)
        for jj in range(n_c):
            y = (c_ref[jj] - mu) * rs * gn_ref[:, jj * tc:(jj + 1) * tc] + bn_ref[:, jj * tc:(jj + 1) * tc]
            s_ref[:, jj * tc:(jj + 1) * tc] = (y * jax.nn.sigmoid(y)).astype(s_ref.dtype)


def _conv_norm(u, prev, w_dw, b_dw, g_n, b_n, *, tm, tc, stride, tiles_per_seq, name):
    m, d = u.shape
    halo = prev.shape[0]
    nj = d // tc
    w_rep = jnp.repeat(w_dw, SUBLANES, axis=0)
    kern = functools.partial(_conv_norm_kernel, tm=tm, tc=tc, halo=halo, stride=stride,
                             tiles_per_seq=tiles_per_seq)
    in_specs = [pl.BlockSpec((tm, tc), lambda i, j: (i, j))]
    args = [u]
    if tiles_per_seq > 1:
        hb = tm // halo
        in_specs.append(pl.BlockSpec((halo, tc), lambda i, j: (jnp.maximum(i * hb - 1, 0), j)))
        args.append(u)
    in_specs += [pl.BlockSpec((halo, tc), lambda i, j: (0, j)),
                 pl.BlockSpec((CONV_A_WIDTH * SUBLANES, tc), lambda i, j: (0, j)),
                 pl.BlockSpec((1, tc), lambda i, j: (0, j)),
                 pl.BlockSpec((1, d), lambda i, j: (0, 0)),
                 pl.BlockSpec((1, d), lambda i, j: (0, 0))]
    args += [prev, w_rep, b_dw.reshape(1, d), g_n.reshape(1, d), b_n.reshape(1, d)]
    vmem = (2 * tm * tc * 4 + 4 * halo * tc * 4 + 2 * CONV_A_WIDTH * SUBLANES * tc * 4
            + 2 * tm * d * 2 + (halo + tm) * tc * 4 + tm * d * 4 + 4 * tm * tc * 4)
    return pl.pallas_call(
        kern,
        grid=(m // tm, nj),
        in_specs=in_specs,
        out_specs=pl.BlockSpec((tm, d), lambda i, j: (i, 0)),
        out_shape=jax.ShapeDtypeStruct((m, d), BF16),
        scratch_shapes=[pltpu.VMEM((halo + tm, tc), F32), pltpu.VMEM((nj, tm, tc), F32)],
        compiler_params=_params(("arbitrary", "arbitrary"), vmem),
        name=name,
    )(*args)


def _mm_res_ln_kernel(*refs, has_bias):
    if has_bias:
        a_ref, w_ref, bias_ref, x_ref, g_ref, b_ref, o_ref = refs
    else:
        a_ref, w_ref, x_ref, g_ref, b_ref, o_ref = refs
    k = pl.program_id(1)
    part = jnp.dot(a_ref[...].astype(BF16), w_ref[...].astype(BF16), preferred_element_type=F32)

    @pl.when(k == 0)
    def _():
        o_ref[...] = part

    @pl.when(k != 0)
    def _():
        o_ref[...] = o_ref[...] + part

    @pl.when(k == pl.num_programs(1) - 1)
    def _():
        f = o_ref[...]
        if has_bias:
            f = f + bias_ref[...]
        o_ref[...] = _layer_norm(DN_ALPHA * x_ref[...] + f, g_ref[...], b_ref[...])


def _matmul_res_ln(a, w, bias, x, g, b, *, tm, tk, name):
    m, kdim = a.shape
    n = w.shape[1]
    has_bias = bias is not None
    in_specs = [pl.BlockSpec((tm, tk), lambda i, k: (i, k)),
                pl.BlockSpec((tk, n), lambda i, k: (k, 0))]
    args = [a, w]
    if has_bias:
        in_specs.append(pl.BlockSpec((1, n), lambda i, k: (0, 0)))
        args.append(bias.reshape(1, n))
    in_specs += [_single((tm, n), lambda i, k: (i, 0)),
                 pl.BlockSpec((1, n), lambda i, k: (0, 0)),
                 pl.BlockSpec((1, n), lambda i, k: (0, 0))]
    args += [x, g.reshape(1, n), b.reshape(1, n)]
    vmem = (2 * tm * tk * a.dtype.itemsize + 2 * tk * n * 4 + tk * n * 2 + tm * n * 4
            + 2 * tm * n * 4 + 2 * tm * n * 4)
    return pl.pallas_call(
        functools.partial(_mm_res_ln_kernel, has_bias=has_bias),
        grid=(m // tm, kdim // tk),
        in_specs=in_specs,
        out_specs=pl.BlockSpec((tm, n), lambda i, k: (i, 0)),
        out_shape=jax.ShapeDtypeStruct((m, n), F32),
        compiler_params=_params(("arbitrary", "arbitrary"), vmem),
        name=name,
    )(*args)


def _ffn_kernel(x_ref, gprev_ref, wg_ref, wv_ref, wdw_ref, bdw_ref, wd_ref, lg_ref, lb_ref,
                y_ref, tail_ref, xb_ref, ext_ref, *carry, tm, halo, stride, tiles_per_seq):
    i = pl.program_id(0)
    j = pl.program_id(1)

    @pl.when(j == 0)
    def _():
        xb_ref[...] = x_ref[...].astype(BF16)

    xb = xb_ref[...]
    g = jnp.dot(xb, wg_ref[...].astype(BF16), preferred_element_type=F32)
    v = jnp.dot(xb, wv_ref[...].astype(BF16), preferred_element_type=F32)

    if tiles_per_seq > 1:
        carry_ref, = carry

        @pl.when(i % tiles_per_seq == 0)
        def _():
            ext_ref[0:halo, :] = gprev_ref[...]

        @pl.when(i % tiles_per_seq != 0)
        def _():
            ext_ref[0:halo, :] = carry_ref[j]
    else:
        ext_ref[0:halo, :] = gprev_ref[...]
    ext_ref[halo:halo + tm, :] = g
    g_last = g[tm - halo:, :]
    tail_ref[0] = g_last
    if tiles_per_seq > 1:
        carry_ref[j] = g_last

    c = (wdw_ref[2:3, :] * g
         + wdw_ref[1:2, :] * ext_ref[halo - stride:halo - stride + tm, :]
         + wdw_ref[0:1, :] * ext_ref[halo - 2 * stride:halo - 2 * stride + tm, :]
         + bdw_ref[...])
    h = (_gelu_exact(c) * v).astype(BF16)
    part = jnp.dot(h, wd_ref[...].astype(BF16), preferred_element_type=F32)

    @pl.when(j == 0)
    def _():
        y_ref[...] = part

    @pl.when(j != 0)
    def _():
        y_ref[...] = y_ref[...] + part

    @pl.when(j == pl.num_programs(1) - 1)
    def _():
        y_ref[...] = _layer_norm(DN_ALPHA * x_ref[...] + y_ref[...], lg_ref[...], lb_ref[...])


def _conv_ffn(x, g_prev, w_up, w_dw, b_dw, w_down, ln_g, ln_b, *, tm, tf, stride, tiles_per_seq, name):
    m, d = x.shape
    f = w_down.shape[0]
    halo = g_prev.shape[0]
    nj = f // tf
    ni = m // tm
    kern = functools.partial(_ffn_kernel, tm=tm, halo=halo, stride=stride, tiles_per_seq=tiles_per_seq)
    scratch = [pltpu.VMEM((tm, d), BF16), pltpu.VMEM((halo + tm, tf), F32)]
    if tiles_per_seq > 1:
        scratch.append(pltpu.VMEM((nj, halo, tf), F32))
    vmem = (tm * d * 4 + 2 * tm * d * 4 + tm * d * 2 + 4 * d * tf * 4 + 2 * tf * d * 4
            + 3 * d * tf * 2 + (halo + tm) * tf * 4 + 8 * tm * tf * 4 + nj * halo * tf * 4)
    return pl.pallas_call(
        kern,
        grid=(ni, nj),
        in_specs=[_single((tm, d), lambda i, j: (i, 0)),
                  pl.BlockSpec((halo, tf), lambda i, j: (0, j)),
                  pl.BlockSpec((d, tf), lambda i, j: (0, j)),
                  pl.BlockSpec((d, tf), lambda i, j: (0, j + nj)),
                  pl.BlockSpec((FFN_CONV_WIDTH, tf), lambda i, j: (0, j)),
                  pl.BlockSpec((1, tf), lambda i, j: (0, j)),
                  pl.BlockSpec((tf, d), lambda i, j: (j, 0)),
                  pl.BlockSpec((1, d), lambda i, j: (0, 0)),
                  pl.BlockSpec((1, d), lambda i, j: (0, 0))],
        out_specs=[pl.BlockSpec((tm, d), lambda i, j: (i, 0)),
                   pl.BlockSpec((1, halo, tf), lambda i, j: (i, 0, j))],
        out_shape=[jax.ShapeDtypeStruct((m, d), F32),
                   jax.ShapeDtypeStruct((ni, halo, f), F32)],
        scratch_shapes=scratch,
        compiler_params=_params(("arbitrary", "arbitrary"), vmem),
        name=name,
    )(x, g_prev, w_up, w_up, w_dw, b_dw.reshape(1, f), w_down, ln_g.reshape(1, d), ln_b.reshape(1, d))


def _softmax_pv(s, sink, v):
    m = jnp.maximum(jnp.max(s, axis=-1, keepdims=True), sink)
    p = jnp.exp(s - m)
    denom = jnp.sum(p, axis=-1, keepdims=True) + jnp.exp(sink - m)
    o = jnp.dot(p.astype(BF16), v, preferred_element_type=F32)
    return o / denom


def _attn_prompt_kernel(sinks_ref, q_ref, kvp_ref, kvc_ref, o_ref):
    n = pl.program_id(1)
    kv_all = jnp.concatenate([kvp_ref[...], kvc_ref[...]], axis=0).astype(BF16)
    qi = lax.broadcasted_iota(jnp.int32, (WINDOW, 2 * WINDOW), 0)
    kj = lax.broadcasted_iota(jnp.int32, (WINDOW, 2 * WINDOW), 1)
    dist = WINDOW + qi - kj
    distf = dist.astype(F32)
    mask = (dist >= 0) & (dist <= WINDOW) & ((kj >= WINDOW) | (n > 0))
    for pair in range(N_HEADS // 2):
        outs = []
        for h in (2 * pair, 2 * pair + 1):
            kvh = h // GROUP
            qh = q_ref[:, h * HEAD_DIM:(h + 1) * HEAD_DIM]
            kh = kv_all[:, kvh * HEAD_DIM:(kvh + 1) * HEAD_DIM]
            vh = kv_all[:, KV_WIDTH + kvh * HEAD_DIM:KV_WIDTH + (kvh + 1) * HEAD_DIM]
            s = lax.dot_general(qh, kh, (((1,), (1,)), ((), ())), preferred_element_type=F32)
            s = jnp.where(mask, s - ALIBI_SLOPES[h] * distf, NEG_BIG)
            outs.append(_softmax_pv(s, sinks_ref[h], vh))
        o_ref[:, pair * 2 * HEAD_DIM:(pair + 1) * 2 * HEAD_DIM] = (
            jnp.concatenate(outs, axis=1).astype(o_ref.dtype))


def _attn_prompt(q, kv, sinks, *, name):
    m = q.shape[0]
    nb = SEQ // WINDOW
    return pl.pallas_call(
        _attn_prompt_kernel,
        grid=(BATCH, nb),
        in_specs=[pl.BlockSpec(memory_space=pltpu.SMEM),
                  pl.BlockSpec((WINDOW, D_MODEL), lambda b, n: (b * nb + n, 0)),
                  pl.BlockSpec((WINDOW, 2 * KV_WIDTH), lambda b, n: (b * nb + jnp.maximum(n - 1, 0), 0)),
                  pl.BlockSpec((WINDOW, 2 * KV_WIDTH), lambda b, n: (b * nb + n, 0))],
        out_specs=pl.BlockSpec((WINDOW, D_MODEL), lambda b, n: (b * nb + n, 0)),
        out_shape=jax.ShapeDtypeStruct((m, D_MODEL), BF16),
        compiler_params=_params(("arbitrary", "arbitrary"), 16 << 20),
        name=name,
    )(sinks, q, kv, kv)


def _attn_sample_kernel(sinks_ref, q_ref, kvn_ref, ck_ref, cv_ref, o_ref):
    rows = GROUP * DEC_SEQ
    n_keys = WINDOW + DEC_SEQ
    row = lax.broadcasted_iota(jnp.int32, (rows, n_keys), 0)
    kj = lax.broadcasted_iota(jnp.int32, (rows, n_keys), 1)
    head_local = lax.shift_right_logical(row, DEC_SEQ.bit_length() - 1)
    dist = WINDOW + (row & (DEC_SEQ - 1)) - kj
    distf = dist.astype(F32)
    mask = (dist >= 0) & (dist <= WINDOW)
    head_col = lax.shift_right_logical(lax.broadcasted_iota(jnp.int32, (rows, 1), 0), DEC_SEQ.bit_length() - 1)
    kvn = kvn_ref[...]
    pieces = [None] * N_HEADS
    for kvh in range(N_KV_HEADS):
        lo, hi = kvh * HEAD_DIM, (kvh + 1) * HEAD_DIM
        q_st = jnp.concatenate(
            [q_ref[:, (kvh * GROUP + g) * HEAD_DIM:(kvh * GROUP + g + 1) * HEAD_DIM] for g in range(GROUP)],
            axis=0).astype(BF16)
        k_ext = jnp.concatenate([ck_ref[:, lo:hi], kvn[:, lo:hi]], axis=0).astype(BF16)
        v_ext = jnp.concatenate([cv_ref[:, lo:hi], kvn[:, KV_WIDTH + lo:KV_WIDTH + hi]], axis=0).astype(BF16)
        slope = jnp.zeros((rows, n_keys), F32)
        sink = jnp.zeros((rows, 1), F32)
        for g in range(GROUP):
            h = kvh * GROUP + g
            slope = jnp.where(head_local == g, ALIBI_SLOPES[h], slope)
            sink = jnp.where(head_col == g, sinks_ref[h], sink)
        s = lax.dot_general(q_st, k_ext, (((1,), (1,)), ((), ())), preferred_element_type=F32)
        s = jnp.where(mask, s - slope * distf, NEG_BIG)
        o = _softmax_pv(s, sink, v_ext)
        for g in range(GROUP):
            pieces[kvh * GROUP + g] = o[g * DEC_SEQ:(g + 1) * DEC_SEQ, :]
    o_ref[...] = jnp.concatenate(pieces, axis=1)


def _attn_sample(q, kv_new, cache_k, cache_v, sinks, *, name):
    ck = cache_k.reshape(DEC_BATCH, WINDOW, KV_WIDTH)
    cv = cache_v.reshape(DEC_BATCH, WINDOW, KV_WIDTH)
    return pl.pallas_call(
        _attn_sample_kernel,
        grid=(DEC_BATCH,),
        in_specs=[pl.BlockSpec(memory_space=pltpu.SMEM),
                  pl.BlockSpec((None, DEC_SEQ, D_MODEL), lambda b: (b, 0, 0)),
                  pl.BlockSpec((None, DEC_SEQ, 2 * KV_WIDTH), lambda b: (b, 0, 0)),
                  pl.BlockSpec((None, WINDOW, KV_WIDTH), lambda b: (b, 0, 0)),
                  pl.BlockSpec((None, WINDOW, KV_WIDTH), lambda b: (b, 0, 0))],
        out_specs=pl.BlockSpec((None, DEC_SEQ, D_MODEL), lambda b: (b, 0, 0)),
        out_shape=jax.ShapeDtypeStruct((DEC_BATCH, DEC_SEQ, D_MODEL), F32),
        compiler_params=_params(("arbitrary",), 8 << 20),
        name=name,
    )(sinks, q, kv_new, ck, cv)


def _to_time_major(a):
    return jnp.swapaxes(a, 0, 1).reshape(a.shape[1] * a.shape[0], a.shape[2])


def _to_batch_major(a2d, t):
    return jnp.swapaxes(a2d.reshape(t, DEC_BATCH, a2d.shape[1]), 0, 1)


def _trunk(x, conv_prev, ffn_prev, params, *, tag, tm, stride, tiles_per_seq, attention):
    (a_w_in, a_b_in, a_w_dw, a_b_dw, a_norm_g, a_norm_b, a_w_out, a_b_out, w_kv, b_w_q, b_sinks, b_w_o,
     f_w_up, f_w_dw, f_b_dw, f_w_down, ln_mix_g, ln_mix_b, ln_ffn_g, ln_ffn_b) = params
    m = x.shape[0]
    conv_tm = min(CONV_TM, m)
    us, tails = [], []
    kv = None
    for l in range(DEPTH):
        if l < N_A_LAYERS:
            u = _glu_proj(x, a_w_in[l], a_b_in[l], tm=tm, tn=256, name=f"{tag}_glu{l}")
            us.append(u)
            s = _conv_norm(u, conv_prev[l], a_w_dw[l], a_b_dw[l], a_norm_g[l], a_norm_b[l],
                           tm=conv_tm, tc=CONV_TC, stride=stride,
                           tiles_per_seq=(SEQ // conv_tm if tiles_per_seq > 1 else 1),
                           name=f"{tag}_conv{l}")
            x = _matmul_res_ln(s, a_w_out[l], a_b_out[l], x, ln_mix_g[l], ln_mix_b[l],
                               tm=tm, tk=512, name=f"{tag}_aout{l}")
        else:
            jb = l - N_A_LAYERS
            o = attention(x, kv, b_w_q[jb], b_sinks[jb], l)
            x = _matmul_res_ln(o, b_w_o[jb], None, x, ln_mix_g[l], ln_mix_b[l],
                               tm=tm, tk=512, name=f"{tag}_bout{l}")
        x, tail = _conv_ffn(x, ffn_prev[l], f_w_up[l], f_w_dw[l], f_b_dw[l], f_w_down[l],
                            ln_ffn_g[l], ln_ffn_b[l], tm=tm, tf=FFN_TF, stride=stride,
                            tiles_per_seq=tiles_per_seq, name=f"{tag}_ffn{l}")
        tails.append(tail)
        if l == N_A_LAYERS - 1:
            kv = _matmul(x, w_kv, tm=tm, tn=2 * KV_WIDTH, out_dtype=F32, name=f"{tag}_kv")
    return x, us, tails, kv


def kernel(x_prompt, x_sample, state_conv_a, state_ffn_conv, cache_k_win, cache_v_win, a_w_in, a_b_in, a_w_dw, a_b_dw, a_norm_g, a_norm_b, a_w_out, a_b_out, w_kv, b_w_q, b_sinks, b_w_o, f_w_up, f_w_dw, f_b_dw, f_w_down, ln_mix_g, ln_mix_b, ln_ffn_g, ln_ffn_b):
    params = (a_w_in, a_b_in, a_w_dw, a_b_dw, a_norm_g, a_norm_b, a_w_out, a_b_out, w_kv, b_w_q, b_sinks, b_w_o,
              f_w_up, f_w_dw, f_b_dw, f_w_down, ln_mix_g, ln_mix_b, ln_ffn_g, ln_ffn_b)
    q_scale = HEAD_DIM ** -0.5

    def attn_p(x, kv, w_q, sinks, l):
        q = _matmul(x, w_q, tm=PROMPT_TM, tn=512, out_dtype=BF16, scale=q_scale, name=f"p_q{l}")
        return _attn_prompt(q, kv, sinks, name=f"p_attn{l}")

    xp = x_prompt.reshape(PROMPT_ROWS, D_MODEL)
    zero_conv = jnp.zeros((CONV_HALO, D_MODEL), F32)
    zero_ffn = jnp.zeros((SUBLANES, D_FF), F32)
    yp, us_p, tails_p, kv_p = _trunk(xp, [zero_conv] * N_A_LAYERS, [zero_ffn] * DEPTH, params,
                                     tag="p", tm=PROMPT_TM, stride=1, tiles_per_seq=SEQ // PROMPT_TM,
                                     attention=attn_p)
    y_prompt = yp.reshape(BATCH, SEQ, D_MODEL)
    conv_a_prompt = jnp.stack([u.reshape(BATCH, SEQ, D_MODEL)[:, SEQ - (CONV_A_WIDTH - 1):] for u in us_p])
    tps = SEQ // PROMPT_TM
    ffn_conv_prompt = jnp.stack([t[tps - 1::tps, SUBLANES - (FFN_CONV_WIDTH - 1):] for t in tails_p])
    kv_p3 = kv_p.reshape(BATCH, SEQ, 2 * KV_WIDTH)[:, SEQ - WINDOW:]
    k_win_prompt = kv_p3[..., :KV_WIDTH].reshape(BATCH, WINDOW, N_KV_HEADS, HEAD_DIM)
    v_win_prompt = kv_p3[..., KV_WIDTH:].reshape(BATCH, WINDOW, N_KV_HEADS, HEAD_DIM)

    def attn_s(x, kv, w_q, sinks, l):
        q = _matmul(x, w_q, tm=SAMPLE_ROWS, tn=512, out_dtype=F32, scale=q_scale, name=f"s_q{l}")
        o = _attn_sample(_to_batch_major(q, DEC_SEQ), _to_batch_major(kv, DEC_SEQ),
                         cache_k_win, cache_v_win, sinks, name=f"s_attn{l}")
        return _to_time_major(o)

    xs = _to_time_major(x_sample)
    conv_prev_s = [_to_time_major(state_conv_a[l]) for l in range(N_A_LAYERS)]
    ffn_prev_s = [_to_time_major(state_ffn_conv[l]) for l in range(DEPTH)]
    ys, us_s, tails_s, kv_s = _trunk(xs, conv_prev_s, ffn_prev_s, params,
                                     tag="s", tm=SAMPLE_ROWS, stride=DEC_BATCH, tiles_per_seq=1,
                                     attention=attn_s)
    y_sample = _to_batch_major(ys, DEC_SEQ)
    conv_a_sample = jnp.stack([
        _to_batch_major(jnp.concatenate([conv_prev_s[l][SAMPLE_ROWS:], us_s[l]], axis=0), CONV_A_WIDTH - 1)
        for l in range(N_A_LAYERS)])
    ffn_conv_sample = jnp.stack([_to_batch_major(t[0], FFN_CONV_WIDTH - 1) for t in tails_s])
    kv_s3 = _to_batch_major(kv_s, DEC_SEQ)
    k_new = kv_s3[..., :KV_WIDTH].reshape(DEC_BATCH, DEC_SEQ, N_KV_HEADS, HEAD_DIM)
    v_new = kv_s3[..., KV_WIDTH:].reshape(DEC_BATCH, DEC_SEQ, N_KV_HEADS, HEAD_DIM)
    k_win_sample = jnp.concatenate([cache_k_win[:, DEC_SEQ:], k_new], axis=1)
    v_win_sample = jnp.concatenate([cache_v_win[:, DEC_SEQ:], v_new], axis=1)

    return (y_prompt, y_sample, conv_a_prompt, ffn_conv_prompt, k_win_prompt, v_win_prompt,
            conv_a_sample, ffn_conv_sample, k_win_sample, v_win_sample)
```

```python
import functools

import jax
import jax.numpy as jnp
from jax import lax
from jax.experimental import pallas as pl
from jax.experimental.pallas import tpu as pltpu

D_MODEL = 2048
BATCH = 4
SEQ = 2048
DEPTH = 4
DEC_BATCH = 32
DEC_SEQ = 8
N_A_LAYERS = DEPTH // 2
CONV_A_WIDTH = 31
FFN_CONV_WIDTH = 3
D_FF = ((8 * D_MODEL // 3 + 255) // 256) * 256
HEAD_DIM = 64
N_HEADS = D_MODEL // HEAD_DIM
N_KV_HEADS = max(1, N_HEADS // 8)
GROUP = N_HEADS // N_KV_HEADS
KV_WIDTH = N_KV_HEADS * HEAD_DIM
WINDOW = 128
DN_ALPHA = (2.0 * DEPTH) ** 0.25
LN_EPS = 1e-5
NEG_BIG = -1e30
ALIBI_SLOPES = tuple(2.0 ** (-8.0 * (h + 1) / N_HEADS) for h in range(N_HEADS))

BF16 = jnp.bfloat16
F32 = jnp.float32

V7X_VMEM_BYTES = 64 * 1024 * 1024
SUBLANES = 8

PROMPT_ROWS = BATCH * SEQ
SAMPLE_ROWS = DEC_BATCH * DEC_SEQ
PROMPT_TM = 1024
FFN_TF = 256
OUT_TN = 512
CONV_TM = 256
CONV_TC = 256
CONV_HALO = 32


def _params(semantics, vmem_bytes):
    limit = min(int(vmem_bytes * 1.25) + (4 << 20), V7X_VMEM_BYTES - (6 << 20))
    return pltpu.CompilerParams(dimension_semantics=semantics, vmem_limit_bytes=limit)


def _gelu_exact(z):
    return 0.5 * z * (1.0 + lax.erf(z * (0.5 ** 0.5)))


def _single(shape, index_map):
    return pl.BlockSpec(shape, index_map, pipeline_mode=pl.Buffered(1))


def _layer_block(l, shape, index_map):
    return pl.BlockSpec((None,) + shape, lambda *g: (l,) + index_map(*g))


def _rows3(p):
    return p.reshape(p.shape[0], 1, p.shape[1])


def _mm_kernel(x_ref, w_ref, o_ref, xb_ref, *, scale):
    @pl.when(pl.program_id(1) == 0)
    def _():
        xb_ref[...] = x_ref[...].astype(BF16)

    acc = jnp.dot(xb_ref[...], w_ref[...].astype(BF16), preferred_element_type=F32)
    if scale != 1.0:
        acc = acc * scale
    o_ref[...] = acc.astype(o_ref.dtype)


def _matmul(x, w, l, *, tm, tn, out_dtype, scale=1.0, name):
    m, k = x.shape
    n = w.shape[2]
    vmem = 2 * tm * k * 4 + tm * k * 2 + 2 * k * tn * 4 + k * tn * 2 + 3 * tm * tn * 4
    return pl.pallas_call(
        functools.partial(_mm_kernel, scale=scale),
        grid=(m // tm, n // tn),
        in_specs=[pl.BlockSpec((tm, k), lambda i, j: (i, 0)),
                  _layer_block(l, (k, tn), lambda i, j: (0, j))],
        out_specs=pl.BlockSpec((tm, tn), lambda i, j: (i, j)),
        out_shape=jax.ShapeDtypeStruct((m, n), out_dtype),
        scratch_shapes=[pltpu.VMEM((tm, k), BF16)],
        compiler_params=_params(("arbitrary", "arbitrary"), vmem),
        name=name,
    )(x, w)


def _glu_kernel(x_ref, wa_ref, wg_ref, ba_ref, bg_ref, u_ref, xb_ref):
    @pl.when(pl.program_id(1) == 0)
    def _():
        xb_ref[...] = x_ref[...].astype(BF16)

    xb = xb_ref[...]
    a = jnp.dot(xb, wa_ref[...].astype(BF16), preferred_element_type=F32) + ba_ref[...]
    g = jnp.dot(xb, wg_ref[...].astype(BF16), preferred_element_type=F32) + bg_ref[...]
    u_ref[...] = a * jax.nn.sigmoid(g)


def _glu_proj(x, w_in, b_in, l, *, tm, tn, name):
    m, k = x.shape
    n = w_in.shape[2] // 2
    nj = n // tn
    b3 = _rows3(b_in)
    vmem = 2 * tm * k * 4 + tm * k * 2 + 4 * k * tn * 4 + 2 * k * tn * 2 + 5 * tm * tn * 4
    return pl.pallas_call(
        _glu_kernel,
        grid=(m // tm, nj),
        in_specs=[pl.BlockSpec((tm, k), lambda i, j: (i, 0)),
                  _layer_block(l, (k, tn), lambda i, j: (0, j)),
                  _layer_block(l, (k, tn), lambda i, j: (0, j + nj)),
                  _layer_block(l, (1, tn), lambda i, j: (0, j)),
                  _layer_block(l, (1, tn), lambda i, j: (0, j + nj))],
        out_specs=pl.BlockSpec((tm, tn), lambda i, j: (i, j)),
        out_shape=jax.ShapeDtypeStruct((m, n), F32),
        scratch_shapes=[pltpu.VMEM((tm, k), BF16)],
        compiler_params=_params(("arbitrary", "arbitrary"), vmem),
        name=name,
    )(x, w_in, w_in, b3, b3)


def _conv_norm_kernel(*refs, tm, tc, halo, stride, tiles_per_seq):
    if tiles_per_seq > 1:
        u_ref, uh_ref, prev_ref, w_ref, b_ref, gn_ref, bn_ref, s_ref, ext_ref, c_ref = refs
    else:
        u_ref, prev_ref, w_ref, b_ref, gn_ref, bn_ref, s_ref, ext_ref, c_ref = refs
    i = pl.program_id(0)
    j = pl.program_id(1)
    nj = pl.num_programs(1)

    if tiles_per_seq > 1:
        @pl.when(i % tiles_per_seq == 0)
        def _():
            ext_ref[0:halo, :] = prev_ref[...]

        @pl.when(i % tiles_per_seq != 0)
        def _():
            ext_ref[0:halo, :] = uh_ref[...]
    else:
        ext_ref[0:halo, :] = prev_ref[...]
    ext_ref[halo:halo + tm, :] = u_ref[...]

    bias = jnp.broadcast_to(b_ref[...], (SUBLANES, tc))
    first = halo - (CONV_A_WIDTH - 1) * stride
    for rb in range(tm // SUBLANES):
        acc = bias
        for k in range(CONV_A_WIDTH):
            off = first + k * stride + rb * SUBLANES
            acc = acc + w_ref[k * SUBLANES:(k + 1) * SUBLANES, :] * ext_ref[off:off + SUBLANES, :]
        c_ref[j, rb * SUBLANES:(rb + 1) * SUBLANES, :] = acc

    @pl.when(j == nj - 1)
    def _():
        n_c = c_ref.shape[0]
        d = n_c * tc
        tot = jnp.sum(c_ref[0], axis=-1, keepdims=True)
        for jj in range(1, n_c):
            tot = tot + jnp.sum(c_ref[jj], axis=-1, keepdims=True)
        mu = tot * (1.0 / d)
        sq = None
        for jj in range(n_c):
            zc = c_ref[jj] - mu
            part = jnp.sum(zc * zc, axis=-1, keepdims=True)
            sq = part if sq is None else sq + part
        rs = lax.rsqrt(sq * (1.0 / d) + LN_EPS)
        for jj in range(n_c):
            y = (c_ref[jj] - mu) * rs * gn_ref[:, jj * tc:(jj + 1) * tc] + bn_ref[:, jj * tc:(jj + 1) * tc]
            s_ref[:, jj * tc:(jj + 1) * tc] = (y * jax.nn.sigmoid(y)).astype(s_ref.dtype)


def _conv_norm(u, prev, w_dw, b_dw, g_n, b_n, l, *, tm, tc, stride, tiles_per_seq, name):
    m, d = u.shape
    halo = prev.shape[0]
    nj = d // tc
    w_rep = jnp.repeat(w_dw, SUBLANES, axis=1)
    kern = functools.partial(_conv_norm_kernel, tm=tm, tc=tc, halo=halo, stride=stride,
                             tiles_per_seq=tiles_per_seq)
    in_specs = [pl.BlockSpec((tm, tc), lambda i, j: (i, j))]
    args = [u]
    if tiles_per_seq > 1:
        hb = tm // halo
        in_specs.append(pl.BlockSpec((halo, tc), lambda i, j: (jnp.maximum(i * hb - 1, 0), j)))
        args.append(u)
    in_specs += [pl.BlockSpec((halo, tc), lambda i, j: (0, j)),
                 _layer_block(l, (CONV_A_WIDTH * SUBLANES, tc), lambda i, j: (0, j)),
                 _layer_block(l, (1, tc), lambda i, j: (0, j)),
                 _layer_block(l, (1, d), lambda i, j: (0, 0)),
                 _layer_block(l, (1, d), lambda i, j: (0, 0))]
    args += [prev, w_rep, _rows3(b_dw), _rows3(g_n), _rows3(b_n)]
    vmem = (2 * tm * tc * 4 + 4 * halo * tc * 4 + 2 * CONV_A_WIDTH * SUBLANES * tc * 4
            + 2 * tm * d * 2 + (halo + tm) * tc * 4 + tm * d * 4 + 4 * tm * tc * 4)
    return pl.pallas_call(
        kern,
        grid=(m // tm, nj),
        in_specs=in_specs,
        out_specs=pl.BlockSpec((tm, d), lambda i, j: (i, 0)),
        out_shape=jax.ShapeDtypeStruct((m, d), BF16),
        scratch_shapes=[pltpu.VMEM((halo + tm, tc), F32), pltpu.VMEM((nj, tm, tc), F32)],
        compiler_params=_params(("arbitrary", "arbitrary"), vmem),
        name=name,
    )(*args)


def _residual_ln_chunks(x_ref, acc_ref, g_ref, b_ref, o_ref, tn):
    n_c = acc_ref.shape[0]
    d = n_c * tn
    tot = None
    for jj in range(n_c):
        z = DN_ALPHA * x_ref[:, jj * tn:(jj + 1) * tn] + acc_ref[jj]
        acc_ref[jj] = z
        part = jnp.sum(z, axis=-1, keepdims=True)
        tot = part if tot is None else tot + part
    mu = tot * (1.0 / d)
    sq = None
    for jj in range(n_c):
        zc = acc_ref[jj] - mu
        part = jnp.sum(zc * zc, axis=-1, keepdims=True)
        sq = part if sq is None else sq + part
    rs = lax.rsqrt(sq * (1.0 / d) + LN_EPS)
    for jj in range(n_c):
        o_ref[:, jj * tn:(jj + 1) * tn] = ((acc_ref[jj] - mu) * rs * g_ref[:, jj * tn:(jj + 1) * tn]
                                           + b_ref[:, jj * tn:(jj + 1) * tn])


def _mm_res_ln_kernel(*refs, has_bias, tn):
    if has_bias:
        a_ref, w_ref, bias_ref, x_ref, g_ref, b_ref, o_ref, acc_ref = refs
    else:
        a_ref, w_ref, x_ref, g_ref, b_ref, o_ref, acc_ref = refs
    j = pl.program_id(1)
    part = jnp.dot(a_ref[...].astype(BF16), w_ref[...].astype(BF16), preferred_element_type=F32)
    if has_bias:
        part = part + bias_ref[...]
    acc_ref[j] = part

    @pl.when(j == pl.num_programs(1) - 1)
    def _():
        _residual_ln_chunks(x_ref, acc_ref, g_ref, b_ref, o_ref, tn)


def _matmul_res_ln(a, w, bias, x, g, b, l, jb, *, tm, tn, name):
    m, kdim = a.shape
    n = w.shape[2]
    nj = n // tn
    has_bias = bias is not None
    in_specs = [_single((tm, kdim), lambda i, j: (i, 0)),
                _layer_block(jb, (kdim, tn), lambda i, j: (0, j))]
    args = [a, w]
    if has_bias:
        in_specs.append(_layer_block(jb, (1, tn), lambda i, j: (0, j)))
        args.append(_rows3(bias))
    in_specs += [_single((tm, n), lambda i, j: (i, 0)),
                 _layer_block(l, (1, n), lambda i, j: (0, 0)),
                 _layer_block(l, (1, n), lambda i, j: (0, 0))]
    args += [x, _rows3(g), _rows3(b)]
    vmem = (tm * kdim * a.dtype.itemsize + 2 * kdim * tn * 4 + kdim * tn * 2 + tm * n * 4
            + 2 * tm * n * 4 + tm * n * 4 + 2 * tm * tn * 4)
    return pl.pallas_call(
        functools.partial(_mm_res_ln_kernel, has_bias=has_bias, tn=tn),
        grid=(m // tm, nj),
        in_specs=in_specs,
        out_specs=pl.BlockSpec((tm, n), lambda i, j: (i, 0)),
        out_shape=jax.ShapeDtypeStruct((m, n), F32),
        scratch_shapes=[pltpu.VMEM((nj, tm, tn), F32)],
        compiler_params=_params(("arbitrary", "arbitrary"), vmem),
        name=name,
    )(*args)


def _layer_norm(z, g, b):
    mu = jnp.mean(z, axis=-1, keepdims=True)
    zc = z - mu
    var = jnp.mean(zc * zc, axis=-1, keepdims=True)
    return zc * lax.rsqrt(var + LN_EPS) * g + b


def _ffn_kernel(x_ref, gprev_ref, wg_ref, wv_ref, wdw_ref, bdw_ref, wd_ref, lg_ref, lb_ref,
                y_ref, tail_ref, xb_ref, ext_ref, *carry, tm, halo, stride, tiles_per_seq):
    i = pl.program_id(0)
    j = pl.program_id(1)

    @pl.when(j == 0)
    def _():
        xb_ref[...] = x_ref[...].astype(BF16)
        y_ref[...] = jnp.zeros_like(y_ref)

    if tiles_per_seq > 1:
        carry_ref, = carry

        @pl.when((i == 0) & (j == 0))
        def _():
            carry_ref[...] = jnp.zeros_like(carry_ref)

    xb = xb_ref[...]
    g = jnp.dot(xb, wg_ref[...].astype(BF16), preferred_element_type=F32)
    v = jnp.dot(xb, wv_ref[...].astype(BF16), preferred_element_type=F32)

    if tiles_per_seq > 1:
        prev = jnp.where(i % tiles_per_seq == 0, gprev_ref[...], carry_ref[j])
    else:
        prev = gprev_ref[...]
    ext_ref[0:halo, :] = prev
    ext_ref[halo:halo + tm, :] = g
    g_last = g[tm - halo:, :]
    tail_ref[...] = g_last
    if tiles_per_seq > 1:
        carry_ref[j] = g_last

    c = (wdw_ref[2:3, :] * g
         + wdw_ref[1:2, :] * ext_ref[halo - stride:halo - stride + tm, :]
         + wdw_ref[0:1, :] * ext_ref[halo - 2 * stride:halo - 2 * stride + tm, :]
         + bdw_ref[...])
    h = (_gelu_exact(c) * v).astype(BF16)
    y_ref[...] += jnp.dot(h, wd_ref[...].astype(BF16), preferred_element_type=F32)

    @pl.when(j == pl.num_programs(1) - 1)
    def _():
        y_ref[...] = _layer_norm(DN_ALPHA * x_ref[...] + y_ref[...], lg_ref[...], lb_ref[...])


def _conv_ffn(x, g_prev, w_up, w_dw, b_dw, w_down, ln_g, ln_b, l, *, tm, tf, stride, tiles_per_seq, name):
    m, d = x.shape
    f = w_down.shape[1]
    halo = g_prev.shape[0]
    nj = f // tf
    ni = m // tm
    kern = functools.partial(_ffn_kernel, tm=tm, halo=halo, stride=stride, tiles_per_seq=tiles_per_seq)
    scratch = [pltpu.VMEM((tm, d), BF16), pltpu.VMEM((halo + tm, tf), F32)]
    if tiles_per_seq > 1:
        scratch.append(pltpu.VMEM((nj, halo, tf), F32))
    vmem = (tm * d * 4 + 2 * tm * d * 4 + tm * d * 2 + 4 * d * tf * 4 + 2 * tf * d * 4
            + 3 * d * tf * 2 + (halo + tm) * tf * 4 + 8 * tm * tf * 4 + nj * halo * tf * 4)
    return pl.pallas_call(
        kern,
        grid=(ni, nj),
        in_specs=[_single((tm, d), lambda i, j: (i, 0)),
                  pl.BlockSpec((halo, tf), lambda i, j: (0, j)),
                  _layer_block(l, (d, tf), lambda i, j: (0, j)),
                  _layer_block(l, (d, tf), lambda i, j: (0, j + nj)),
                  _layer_block(l, (FFN_CONV_WIDTH, tf), lambda i, j: (0, j)),
                  _layer_block(l, (1, tf), lambda i, j: (0, j)),
                  _layer_block(l, (tf, d), lambda i, j: (j, 0)),
                  _layer_block(l, (1, d), lambda i, j: (0, 0)),
                  _layer_block(l, (1, d), lambda i, j: (0, 0))],
        out_specs=[pl.BlockSpec((tm, d), lambda i, j: (i, 0)),
                   pl.BlockSpec((None, halo, tf), lambda i, j: (i, 0, j))],
        out_shape=[jax.ShapeDtypeStruct((m, d), F32),
                   jax.ShapeDtypeStruct((ni, halo, f), F32)],
        scratch_shapes=scratch,
        compiler_params=_params(("arbitrary", "arbitrary"), vmem),
        name=name,
    )(x, g_prev, w_up, w_up, w_dw, _rows3(b_dw), w_down, _rows3(ln_g), _rows3(ln_b))


def _softmax_pv(s, sink, v):
    m = jnp.maximum(jnp.max(s, axis=-1, keepdims=True), sink)
    p = jnp.exp(s - m)
    denom = jnp.sum(p, axis=-1, keepdims=True) + jnp.exp(sink - m)
    o = jnp.dot(p.astype(BF16), v, preferred_element_type=F32)
    return o / denom


def _attn_prompt_kernel(sinks_ref, q_ref, kvp_ref, kvc_ref, o_ref, *, jb):
    n = pl.program_id(1)
    kv_all = jnp.concatenate([kvp_ref[...], kvc_ref[...]], axis=0).astype(BF16)
    qi = lax.broadcasted_iota(jnp.int32, (WINDOW, 2 * WINDOW), 0)
    kj = lax.broadcasted_iota(jnp.int32, (WINDOW, 2 * WINDOW), 1)
    dist = WINDOW + qi - kj
    distf = dist.astype(F32)
    mask = (dist >= 0) & (dist <= WINDOW) & ((kj >= WINDOW) | (n > 0))
    for pair in range(N_HEADS // 2):
        outs = []
        for h in (2 * pair, 2 * pair + 1):
            kvh = h // GROUP
            qh = q_ref[:, h * HEAD_DIM:(h + 1) * HEAD_DIM]
            kh = kv_all[:, kvh * HEAD_DIM:(kvh + 1) * HEAD_DIM]
            vh = kv_all[:, KV_WIDTH + kvh * HEAD_DIM:KV_WIDTH + (kvh + 1) * HEAD_DIM]
            s = lax.dot_general(qh, kh, (((1,), (1,)), ((), ())), preferred_element_type=F32)
            s = jnp.where(mask, s - ALIBI_SLOPES[h] * distf, NEG_BIG)
            outs.append(_softmax_pv(s, sinks_ref[jb, h], vh))
        o_ref[:, pair * 2 * HEAD_DIM:(pair + 1) * 2 * HEAD_DIM] = (
            jnp.concatenate(outs, axis=1).astype(o_ref.dtype))


def _attn_prompt(q, kv, sinks, jb, *, name):
    m = q.shape[0]
    nb = SEQ // WINDOW
    return pl.pallas_call(
        functools.partial(_attn_prompt_kernel, jb=jb),
        grid=(BATCH, nb),
        in_specs=[pl.BlockSpec(memory_space=pltpu.SMEM),
                  pl.BlockSpec((WINDOW, D_MODEL), lambda b, n: (b * nb + n, 0)),
                  pl.BlockSpec((WINDOW, 2 * KV_WIDTH), lambda b, n: (b * nb + jnp.maximum(n - 1, 0), 0)),
                  pl.BlockSpec((WINDOW, 2 * KV_WIDTH), lambda b, n: (b * nb + n, 0))],
        out_specs=pl.BlockSpec((WINDOW, D_MODEL), lambda b, n: (b * nb + n, 0)),
        out_shape=jax.ShapeDtypeStruct((m, D_MODEL), BF16),
        compiler_params=_params(("arbitrary", "arbitrary"), 16 << 20),
        name=name,
    )(sinks, q, kv, kv)


def _attn_sample_kernel(sinks_ref, q_ref, kvn_ref, ck_ref, cv_ref, o_ref, *, jb):
    rows = GROUP * DEC_SEQ
    n_keys = WINDOW + DEC_SEQ
    row = lax.broadcasted_iota(jnp.int32, (rows, n_keys), 0)
    kj = lax.broadcasted_iota(jnp.int32, (rows, n_keys), 1)
    head_local = lax.shift_right_logical(row, DEC_SEQ.bit_length() - 1)
    dist = WINDOW + (row & (DEC_SEQ - 1)) - kj
    distf = dist.astype(F32)
    mask = (dist >= 0) & (dist <= WINDOW)
    head_col = lax.shift_right_logical(lax.broadcasted_iota(jnp.int32, (rows, 1), 0), DEC_SEQ.bit_length() - 1)
    kvn = kvn_ref[...]
    pieces = [None] * N_HEADS
    for kvh in range(N_KV_HEADS):
        lo, hi = kvh * HEAD_DIM, (kvh + 1) * HEAD_DIM
        q_st = jnp.concatenate(
            [q_ref[:, (kvh * GROUP + g) * HEAD_DIM:(kvh * GROUP + g + 1) * HEAD_DIM] for g in range(GROUP)],
            axis=0).astype(BF16)
        k_ext = jnp.concatenate([ck_ref[:, lo:hi], kvn[:, lo:hi]], axis=0).astype(BF16)
        v_ext = jnp.concatenate([cv_ref[:, lo:hi], kvn[:, KV_WIDTH + lo:KV_WIDTH + hi]], axis=0).astype(BF16)
        slope = jnp.zeros((rows, n_keys), F32)
        sink = jnp.zeros((rows, 1), F32)
        for g in range(GROUP):
            h = kvh * GROUP + g
            slope = jnp.where(head_local == g, ALIBI_SLOPES[h], slope)
            sink = jnp.where(head_col == g, sinks_ref[jb, h], sink)
        s = lax.dot_general(q_st, k_ext, (((1,), (1,)), ((), ())), preferred_element_type=F32)
        s = jnp.where(mask, s - slope * distf, NEG_BIG)
        o = _softmax_pv(s, sink, v_ext)
        for g in range(GROUP):
            pieces[kvh * GROUP + g] = o[g * DEC_SEQ:(g + 1) * DEC_SEQ, :]
    o_ref[...] = jnp.concatenate(pieces, axis=1)


def _attn_sample(q, kv_new, cache_k, cache_v, sinks, jb, *, name):
    ck = cache_k.reshape(DEC_BATCH, WINDOW, KV_WIDTH)
    cv = cache_v.reshape(DEC_BATCH, WINDOW, KV_WIDTH)
    return pl.pallas_call(
        functools.partial(_attn_sample_kernel, jb=jb),
        grid=(DEC_BATCH,),
        in_specs=[pl.BlockSpec(memory_space=pltpu.SMEM),
                  pl.BlockSpec((None, DEC_SEQ, D_MODEL), lambda b: (b, 0, 0)),
                  pl.BlockSpec((None, DEC_SEQ, 2 * KV_WIDTH), lambda b: (b, 0, 0)),
                  pl.BlockSpec((None, WINDOW, KV_WIDTH), lambda b: (b, 0, 0)),
                  pl.BlockSpec((None, WINDOW, KV_WIDTH), lambda b: (b, 0, 0))],
        out_specs=pl.BlockSpec((None, DEC_SEQ, D_MODEL), lambda b: (b, 0, 0)),
        out_shape=jax.ShapeDtypeStruct((DEC_BATCH, DEC_SEQ, D_MODEL), F32),
        compiler_params=_params(("arbitrary",), 8 << 20),
        name=name,
    )(sinks, q, kv_new, ck, cv)


def _to_time_major(a):
    return jnp.swapaxes(a, 0, 1).reshape(a.shape[1] * a.shape[0], a.shape[2])


def _to_batch_major(a2d, t):
    return jnp.swapaxes(a2d.reshape(t, DEC_BATCH, a2d.shape[1]), 0, 1)


def _trunk(x, conv_prev, ffn_prev, params, *, tag, tm, stride, tiles_per_seq, attention):
    (a_w_in, a_b_in, a_w_dw, a_b_dw, a_norm_g, a_norm_b, a_w_out, a_b_out, w_kv, b_w_q, b_sinks, b_w_o,
     f_w_up, f_w_dw, f_b_dw, f_w_down, ln_mix_g, ln_mix_b, ln_ffn_g, ln_ffn_b) = params
    m = x.shape[0]
    conv_tm = min(CONV_TM, m)
    us, tails = [], []
    kv = None
    for l in range(DEPTH):
        if l < N_A_LAYERS:
            u = _glu_proj(x, a_w_in, a_b_in, l, tm=tm, tn=256, name=f"{tag}_glu{l}")
            us.append(u)
            s = _conv_norm(u, conv_prev[l], a_w_dw, a_b_dw, a_norm_g, a_norm_b, l,
                           tm=conv_tm, tc=CONV_TC, stride=stride,
                           tiles_per_seq=(SEQ // conv_tm if tiles_per_seq > 1 else 1),
                           name=f"{tag}_conv{l}")
            x = _matmul_res_ln(s, a_w_out, a_b_out, x, ln_mix_g, ln_mix_b, l, l,
                               tm=tm, tn=OUT_TN, name=f"{tag}_aout{l}")
        else:
            jb = l - N_A_LAYERS
            o = attention(x, kv, jb)
            x = _matmul_res_ln(o, b_w_o, None, x, ln_mix_g, ln_mix_b, l, jb,
                               tm=tm, tn=OUT_TN, name=f"{tag}_bout{l}")
        x, tail = _conv_ffn(x, ffn_prev[l], f_w_up, f_w_dw, f_b_dw, f_w_down, ln_ffn_g, ln_ffn_b, l,
                            tm=tm, tf=FFN_TF, stride=stride, tiles_per_seq=tiles_per_seq,
                            name=f"{tag}_ffn{l}")
        tails.append(tail)
        if l == N_A_LAYERS - 1:
            kv = _matmul(x, w_kv[None], 0, tm=tm, tn=2 * KV_WIDTH, out_dtype=F32, name=f"{tag}_kv")
    return x, us, tails, kv


def kernel(x_prompt, x_sample, state_conv_a, state_ffn_conv, cache_k_win, cache_v_win, a_w_in, a_b_in, a_w_dw, a_b_dw, a_norm_g, a_norm_b, a_w_out, a_b_out, w_kv, b_w_q, b_sinks, b_w_o, f_w_up, f_w_dw, f_b_dw, f_w_down, ln_mix_g, ln_mix_b, ln_ffn_g, ln_ffn_b):
    params = (a_w_in, a_b_in, a_w_dw, a_b_dw, a_norm_g, a_norm_b, a_w_out, a_b_out, w_kv, b_w_q, b_sinks, b_w_o,
              f_w_up, f_w_dw, f_b_dw, f_w_down, ln_mix_g, ln_mix_b, ln_ffn_g, ln_ffn_b)
    q_scale = HEAD_DIM ** -0.5

    def attn_p(x, kv, jb):
        q = _matmul(x, b_w_q, jb, tm=PROMPT_TM, tn=512, out_dtype=BF16, scale=q_scale, name=f"p_q{jb}")
        return _attn_prompt(q, kv, b_sinks, jb, name=f"p_attn{jb}")

    xp = x_prompt.reshape(PROMPT_ROWS, D_MODEL)
    zero_conv = jnp.zeros((CONV_HALO, D_MODEL), F32)
    zero_ffn = jnp.zeros((SUBLANES, D_FF), F32)
    yp, us_p, tails_p, kv_p = _trunk(xp, [zero_conv] * N_A_LAYERS, [zero_ffn] * DEPTH, params,
                                     tag="p", tm=PROMPT_TM, stride=1, tiles_per_seq=SEQ // PROMPT_TM,
                                     attention=attn_p)
    y_prompt = yp.reshape(BATCH, SEQ, D_MODEL)
    conv_a_prompt = jnp.stack([u.reshape(BATCH, SEQ, D_MODEL)[:, SEQ - (CONV_A_WIDTH - 1):] for u in us_p])
    tps = SEQ // PROMPT_TM
    ffn_conv_prompt = jnp.stack([t[tps - 1::tps, SUBLANES - (FFN_CONV_WIDTH - 1):] for t in tails_p])
    kv_p3 = kv_p.reshape(BATCH, SEQ, 2 * KV_WIDTH)[:, SEQ - WINDOW:]
    k_win_prompt = kv_p3[..., :KV_WIDTH].reshape(BATCH, WINDOW, N_KV_HEADS, HEAD_DIM)
    v_win_prompt = kv_p3[..., KV_WIDTH:].reshape(BATCH, WINDOW, N_KV_HEADS, HEAD_DIM)

    def attn_s(x, kv, jb):
        q = _matmul(x, b_w_q, jb, tm=SAMPLE_ROWS, tn=512, out_dtype=F32, scale=q_scale, name=f"s_q{jb}")
        o = _attn_sample(_to_batch_major(q, DEC_SEQ), _to_batch_major(kv, DEC_SEQ),
                         cache_k_win, cache_v_win, b_sinks, jb, name=f"s_attn{jb}")
        return _to_time_major(o)

    xs = _to_time_major(x_sample)
    conv_prev_s = [_to_time_major(state_conv_a[l]) for l in range(N_A_LAYERS)]
    ffn_prev_s = [_to_time_major(state_ffn_conv[l]) for l in range(DEPTH)]
    ys, us_s, tails_s, kv_s = _trunk(xs, conv_prev_s, ffn_prev_s, params,
                                     tag="s", tm=SAMPLE_ROWS, stride=DEC_BATCH, tiles_per_seq=1,
                                     attention=attn_s)
    y_sample = _to_batch_major(ys, DEC_SEQ)
    conv_a_sample = jnp.stack([
        _to_batch_major(jnp.concatenate([conv_prev_s[l][SAMPLE_ROWS:], us_s[l]], axis=0), CONV_A_WIDTH - 1)
        for l in range(N_A_LAYERS)])
    ffn_conv_sample = jnp.stack([_to_batch_major(t[0], FFN_CONV_WIDTH - 1) for t in tails_s])
    kv_s3 = _to_batch_major(kv_s, DEC_SEQ)
    k_new = kv_s3[..., :KV_WIDTH].reshape(DEC_BATCH, DEC_SEQ, N_KV_HEADS, HEAD_DIM)
    v_new = kv_s3[..., KV_WIDTH:].reshape(DEC_BATCH, DEC_SEQ, N_KV_HEADS, HEAD_DIM)
    k_win_sample = jnp.concatenate([cache_k_win[:, DEC_SEQ:], k_new], axis=1)
    v_win_sample = jnp.concatenate([cache_v_win[:, DEC_SEQ:], v_new], axis=1)

    return (y_prompt, y_sample, conv_a_prompt, ffn_conv_prompt, k_win_prompt, v_win_prompt,
            conv_a_sample, ffn_conv_sample, k_win_sample, v_win_sample)
```

```python
import functools

import jax
import jax.numpy as jnp
from jax import lax
from jax.experimental import pallas as pl
from jax.experimental.pallas import tpu as pltpu

D_MODEL = 2048
BATCH = 4
SEQ = 2048
DEPTH = 4
DEC_BATCH = 32
DEC_SEQ = 8
N_A_LAYERS = DEPTH // 2
CONV_A_WIDTH = 31
FFN_CONV_WIDTH = 3
D_FF = ((8 * D_MODEL // 3 + 255) // 256) * 256
HEAD_DIM = 64
N_HEADS = D_MODEL // HEAD_DIM
N_KV_HEADS = max(1, N_HEADS // 8)
GROUP = N_HEADS // N_KV_HEADS
KV_WIDTH = N_KV_HEADS * HEAD_DIM
WINDOW = 128
DN_ALPHA = (2.0 * DEPTH) ** 0.25
LN_EPS = 1e-5
NEG_BIG = -1e30
ALIBI_SLOPES = tuple(2.0 ** (-8.0 * (h + 1) / N_HEADS) for h in range(N_HEADS))

BF16 = jnp.bfloat16
F32 = jnp.float32

V7X_VMEM_BYTES = 64 * 1024 * 1024
SUBLANES = 8

PROMPT_ROWS = BATCH * SEQ
SAMPLE_ROWS = DEC_BATCH * DEC_SEQ
PROMPT_TM = 1024
FFN_TF = 256
OUT_TN = 512
GLU_TN = 256
W_SPLIT = 4
WD_SPLIT = 2
CONV_HALO = 32


def _params(semantics, vmem_bytes):
    limit = min(int(vmem_bytes * 1.25) + (4 << 20), V7X_VMEM_BYTES - (6 << 20))
    return pltpu.CompilerParams(dimension_semantics=semantics, vmem_limit_bytes=limit)


def _gelu_exact(z):
    return 0.5 * z * (1.0 + lax.erf(z * (0.5 ** 0.5)))


def _single(shape, index_map):
    return pl.BlockSpec(shape, index_map, pipeline_mode=pl.Buffered(1))


def _layer_block(l, shape, index_map):
    return pl.BlockSpec((None,) + shape, lambda *g: (l,) + index_map(*g))


def _rows3(p):
    return p.reshape(p.shape[0], 1, p.shape[1])


def _k_split_specs(l, k, tn, col_map):
    kq = k // W_SPLIT

    def spec(q):
        return _layer_block(l, (kq, tn), lambda i, j: (q, col_map(j)))

    return [spec(q) for q in range(W_SPLIT)]


def _stage_bf16(w_refs, wb_ref):
    kq = w_refs[0].shape[0]
    for q, w_ref in enumerate(w_refs):
        wb_ref[q * kq:(q + 1) * kq, :] = w_ref[...].astype(BF16)


def _mm_kernel(x_ref, *refs, scale):
    w_refs, (o_ref, xb_ref, wb_ref) = refs[:W_SPLIT], refs[W_SPLIT:]

    @pl.when(pl.program_id(1) == 0)
    def _():
        xb_ref[...] = x_ref[...].astype(BF16)

    _stage_bf16(w_refs, wb_ref)
    acc = jnp.dot(xb_ref[...], wb_ref[...], preferred_element_type=F32)
    if scale != 1.0:
        acc = acc * scale
    o_ref[...] = acc.astype(o_ref.dtype)


def _matmul(x, w, l, *, tm, tn, out_dtype, scale=1.0, name):
    m, k = x.shape
    n = w.shape[2]
    vmem = 2 * tm * k * 4 + tm * k * 2 + 2 * k * tn * 4 + k * tn * 2 + 3 * tm * tn * 4
    return pl.pallas_call(
        functools.partial(_mm_kernel, scale=scale),
        grid=(m // tm, n // tn),
        in_specs=[pl.BlockSpec((tm, k), lambda i, j: (i, 0))] + _k_split_specs(l, k, tn, lambda j: j),
        out_specs=pl.BlockSpec((tm, tn), lambda i, j: (i, j)),
        out_shape=jax.ShapeDtypeStruct((m, n), out_dtype),
        scratch_shapes=[pltpu.VMEM((tm, k), BF16), pltpu.VMEM((k, tn), BF16)],
        compiler_params=_params(("arbitrary", "arbitrary"), vmem),
        name=name,
    )(x, *([w] * W_SPLIT))


def _dwconv_unit_stride(ext_ref, w_ref, bias, c_ref, tm):
    tc = c_ref.shape[1]
    sub = lax.broadcasted_iota(jnp.int32, (SUBLANES, tc), 0)
    first = CONV_HALO - (CONV_A_WIDTH - 1)
    n_blocks = tm // SUBLANES
    n_ext = CONV_HALO // SUBLANES + 1
    t_prev = None
    for blk in range(n_blocks + 1):
        base = blk * SUBLANES
        e = [ext_ref[base + a * SUBLANES:base + (a + 1) * SUBLANES, :] if base + (a + 1) * SUBLANES <= CONV_HALO + tm
             else None for a in range(n_ext)]
        t = [None] * SUBLANES
        for p in range(SUBLANES):
            if blk == n_blocks and p == 0:
                continue
            for a in range(n_ext):
                k = a * SUBLANES + p - first
                if 0 <= k < CONV_A_WIDTH:
                    term = w_ref[k * SUBLANES:(k + 1) * SUBLANES, :] * e[a]
                    t[p] = term if t[p] is None else t[p] + term
        if blk > 0:
            acc = bias + t_prev[0]
            for p in range(1, SUBLANES):
                mixed = jnp.where(sub >= p, t_prev[p], t[p])
                acc = acc + pltpu.roll(mixed, SUBLANES - p, axis=0)
            c_ref[base - SUBLANES:base, :] = acc
        t_prev = t


def _dwconv_aligned_stride(ext_ref, w_ref, bias, c_ref, tm, halo, stride):
    first = halo - (CONV_A_WIDTH - 1) * stride
    for blk in range(tm // SUBLANES):
        acc = bias
        for k in range(CONV_A_WIDTH):
            off = first + k * stride + blk * SUBLANES
            acc = acc + w_ref[k * SUBLANES:(k + 1) * SUBLANES, :] * ext_ref[off:off + SUBLANES, :]
        c_ref[blk * SUBLANES:(blk + 1) * SUBLANES, :] = acc


def _glu_conv_kernel(x_ref, *refs, tm, halo, stride, tiles_per_seq):
    wa_refs, wg_refs = refs[:W_SPLIT], refs[W_SPLIT:2 * W_SPLIT]
    ba_ref, bg_ref, prev_ref, wdw_ref, bdw_ref, c_ref, tail_ref, xb_ref, wab_ref, wgb_ref, ext_ref = (
        refs[2 * W_SPLIT:2 * W_SPLIT + 11])
    i = pl.program_id(0)
    j = pl.program_id(1)

    @pl.when(j == 0)
    def _():
        xb_ref[...] = x_ref[...].astype(BF16)

    if tiles_per_seq > 1:
        carry_ref = refs[2 * W_SPLIT + 11]

        @pl.when((i == 0) & (j == 0))
        def _():
            carry_ref[...] = jnp.zeros_like(carry_ref)

    _stage_bf16(wa_refs, wab_ref)
    _stage_bf16(wg_refs, wgb_ref)
    xb = xb_ref[...]
    a = jnp.dot(xb, wab_ref[...], preferred_element_type=F32) + ba_ref[...]
    g = jnp.dot(xb, wgb_ref[...], preferred_element_type=F32) + bg_ref[...]
    u = a * jax.nn.sigmoid(g)

    if tiles_per_seq > 1:
        ext_ref[0:halo, :] = jnp.where(i % tiles_per_seq == 0, prev_ref[...], carry_ref[j])
        carry_ref[j] = u[tm - halo:, :]
    else:
        ext_ref[0:halo, :] = prev_ref[...]
    ext_ref[halo:halo + tm, :] = u
    tail_ref[...] = u[tm - tail_ref.shape[0]:, :]

    bias = jnp.broadcast_to(bdw_ref[...], (SUBLANES, c_ref.shape[1]))
    if stride == 1:
        _dwconv_unit_stride(ext_ref, wdw_ref, bias, c_ref, tm)
    else:
        _dwconv_aligned_stride(ext_ref, wdw_ref, bias, c_ref, tm, halo, stride)


def _glu_conv(x, w_in, b_in, prev, w_dw, b_dw, l, *, tm, tn, stride, tiles_per_seq, tail_rows, name):
    m, k = x.shape
    n = w_in.shape[2] // 2
    nj = n // tn
    ni = m // tm
    halo = prev.shape[0]
    b3 = _rows3(b_in)
    w_rep = jnp.repeat(w_dw, SUBLANES, axis=1)
    kern = functools.partial(_glu_conv_kernel, tm=tm, halo=halo, stride=stride, tiles_per_seq=tiles_per_seq)
    scratch = [pltpu.VMEM((tm, k), BF16), pltpu.VMEM((k, tn), BF16), pltpu.VMEM((k, tn), BF16),
               pltpu.VMEM((halo + tm, tn), F32)]
    if tiles_per_seq > 1:
        scratch.append(pltpu.VMEM((nj, halo, tn), F32))
    vmem = (2 * tm * k * 4 + tm * k * 2 + 4 * k * tn * 4 + 2 * k * tn * 2 + 8 * tm * tn * 4
            + 3 * halo * tn * 4 + (halo + tm) * tn * 4 + nj * halo * tn * 4)
    return pl.pallas_call(
        kern,
        grid=(ni, nj),
        in_specs=([pl.BlockSpec((tm, k), lambda i, j: (i, 0))]
                  + _k_split_specs(l, k, tn, lambda j: j)
                  + _k_split_specs(l, k, tn, lambda j: j + nj)
                  + [_layer_block(l, (1, tn), lambda i, j: (0, j)),
                     _layer_block(l, (1, tn), lambda i, j: (0, j + nj)),
                     pl.BlockSpec((halo, tn), lambda i, j: (0, j)),
                     _layer_block(l, (CONV_A_WIDTH * SUBLANES, tn), lambda i, j: (0, j)),
                     _layer_block(l, (1, tn), lambda i, j: (0, j))]),
        out_specs=[pl.BlockSpec((tm, tn), lambda i, j: (i, j)),
                   pl.BlockSpec((None, tail_rows, tn), lambda i, j: (i, 0, j))],
        out_shape=[jax.ShapeDtypeStruct((m, n), F32),
                   jax.ShapeDtypeStruct((ni, tail_rows, n), F32)],
        scratch_shapes=scratch,
        compiler_params=_params(("arbitrary", "arbitrary"), vmem),
        name=name,
    )(x, *([w_in] * (2 * W_SPLIT)), b3, b3, prev, w_rep, _rows3(b_dw))


def _layer_norm(z, g, b):
    mu = jnp.mean(z, axis=-1, keepdims=True)
    zc = z - mu
    var = jnp.mean(zc * zc, axis=-1, keepdims=True)
    return zc * lax.rsqrt(var + LN_EPS) * g + b


def _residual_ln_chunks(x_ref, acc_ref, g_ref, b_ref, o_ref, tn):
    n_c = acc_ref.shape[0]
    d = n_c * tn
    tot = None
    for jj in range(n_c):
        z = DN_ALPHA * x_ref[:, jj * tn:(jj + 1) * tn] + acc_ref[jj]
        acc_ref[jj] = z
        part = jnp.sum(z, axis=-1, keepdims=True)
        tot = part if tot is None else tot + part
    mu = tot * (1.0 / d)
    sq = None
    for jj in range(n_c):
        zc = acc_ref[jj] - mu
        part = jnp.sum(zc * zc, axis=-1, keepdims=True)
        sq = part if sq is None else sq + part
    rs = lax.rsqrt(sq * (1.0 / d) + LN_EPS)
    for jj in range(n_c):
        o_ref[:, jj * tn:(jj + 1) * tn] = ((acc_ref[jj] - mu) * rs * g_ref[:, jj * tn:(jj + 1) * tn]
                                           + b_ref[:, jj * tn:(jj + 1) * tn])


def _mm_res_ln_kernel(a_ref, *refs, swish_norm, has_bias, tn):
    w_refs, refs = refs[:W_SPLIT], list(refs[W_SPLIT:])
    gn_ref, bn_ref = (refs.pop(0), refs.pop(0)) if swish_norm else (None, None)
    bias_ref = refs.pop(0) if has_bias else None
    x_ref, g_ref, b_ref, o_ref, acc_ref, wb_ref = refs[:6]
    j = pl.program_id(1)

    if swish_norm:
        ab_ref = refs[6]

        @pl.when(j == 0)
        def _():
            y = _layer_norm(a_ref[...], gn_ref[...], bn_ref[...])
            ab_ref[...] = (y * jax.nn.sigmoid(y)).astype(BF16)

        lhs = ab_ref[...]
    else:
        lhs = a_ref[...].astype(BF16)

    _stage_bf16(w_refs, wb_ref)
    part = jnp.dot(lhs, wb_ref[...], preferred_element_type=F32)
    if has_bias:
        part = part + bias_ref[...]
    acc_ref[j] = part

    @pl.when(j == pl.num_programs(1) - 1)
    def _():
        _residual_ln_chunks(x_ref, acc_ref, g_ref, b_ref, o_ref, tn)


def _matmul_res_ln(a, norm, w, bias, x, g, b, l, jb, *, tm, tn, name):
    m, kdim = a.shape
    n = w.shape[2]
    nj = n // tn
    has_bias = bias is not None
    swish_norm = norm is not None
    in_specs = [_single((tm, kdim), lambda i, j: (i, 0))] + _k_split_specs(jb, kdim, tn, lambda j: j)
    args = [a] + [w] * W_SPLIT
    scratch = [pltpu.VMEM((nj, tm, tn), F32), pltpu.VMEM((kdim, tn), BF16)]
    if swish_norm:
        in_specs += [_layer_block(jb, (1, kdim), lambda i, j: (0, 0)),
                     _layer_block(jb, (1, kdim), lambda i, j: (0, 0))]
        args += [_rows3(norm[0]), _rows3(norm[1])]
        scratch.append(pltpu.VMEM((tm, kdim), BF16))
    if has_bias:
        in_specs.append(_layer_block(jb, (1, tn), lambda i, j: (0, j)))
        args.append(_rows3(bias))
    in_specs += [_single((tm, n), lambda i, j: (i, 0)),
                 _layer_block(l, (1, n), lambda i, j: (0, 0)),
                 _layer_block(l, (1, n), lambda i, j: (0, 0))]
    args += [x, _rows3(g), _rows3(b)]
    vmem = (tm * kdim * a.dtype.itemsize + 2 * kdim * tn * 4 + kdim * tn * 2 + tm * n * 4
            + tm * n * 4 + tm * n * 4 + 2 * tm * tn * 4 + (tm * kdim * 2 if swish_norm else 0))
    return pl.pallas_call(
        functools.partial(_mm_res_ln_kernel, swish_norm=swish_norm, has_bias=has_bias, tn=tn),
        grid=(m // tm, nj),
        in_specs=in_specs,
        out_specs=_single((tm, n), lambda i, j: (i, 0)),
        out_shape=jax.ShapeDtypeStruct((m, n), F32),
        scratch_shapes=scratch,
        compiler_params=_params(("arbitrary", "arbitrary"), vmem),
        name=name,
    )(*args)


def _ffn_kernel(x_ref, gprev_ref, *refs, tm, halo, stride, tiles_per_seq):
    wg_refs, wv_refs = refs[:W_SPLIT], refs[W_SPLIT:2 * W_SPLIT]
    wd_refs = refs[2 * W_SPLIT:2 * W_SPLIT + WD_SPLIT]
    wdw_ref, bdw_ref, lg_ref, lb_ref, y_ref, tail_ref, xb_ref, wgb_ref, wvb_ref, wdb_ref, ext_ref = (
        refs[2 * W_SPLIT + WD_SPLIT:2 * W_SPLIT + WD_SPLIT + 11])
    i = pl.program_id(0)
    j = pl.program_id(1)

    @pl.when(j == 0)
    def _():
        xb_ref[...] = x_ref[...].astype(BF16)
        y_ref[...] = jnp.zeros_like(y_ref)

    if tiles_per_seq > 1:
        carry_ref = refs[2 * W_SPLIT + WD_SPLIT + 11]

        @pl.when((i == 0) & (j == 0))
        def _():
            carry_ref[...] = jnp.zeros_like(carry_ref)

    _stage_bf16(wg_refs, wgb_ref)
    _stage_bf16(wv_refs, wvb_ref)
    _stage_bf16(wd_refs, wdb_ref)
    xb = xb_ref[...]
    g = jnp.dot(xb, wgb_ref[...], preferred_element_type=F32)
    v = jnp.dot(xb, wvb_ref[...], preferred_element_type=F32)

    if tiles_per_seq > 1:
        prev = jnp.where(i % tiles_per_seq == 0, gprev_ref[...], carry_ref[j])
    else:
        prev = gprev_ref[...]
    ext_ref[0:halo, :] = prev
    ext_ref[halo:halo + tm, :] = g
    g_last = g[tm - halo:, :]
    tail_ref[...] = g_last
    if tiles_per_seq > 1:
        carry_ref[j] = g_last

    c = (wdw_ref[2:3, :] * g
         + wdw_ref[1:2, :] * ext_ref[halo - stride:halo - stride + tm, :]
         + wdw_ref[0:1, :] * ext_ref[halo - 2 * stride:halo - 2 * stride + tm, :]
         + bdw_ref[...])
    h = (_gelu_exact(c) * v).astype(BF16)
    y_ref[...] += jnp.dot(h, wdb_ref[...], preferred_element_type=F32)

    @pl.when(j == pl.num_programs(1) - 1)
    def _():
        y_ref[...] = _layer_norm(DN_ALPHA * x_ref[...] + y_ref[...], lg_ref[...], lb_ref[...])


def _conv_ffn(x, g_prev, w_up, w_dw, b_dw, w_down, ln_g, ln_b, l, *, tm, tf, stride, tiles_per_seq, name):
    m, d = x.shape
    f = w_down.shape[1]
    halo = g_prev.shape[0]
    nj = f // tf
    ni = m // tm
    tfq = tf // WD_SPLIT
    kern = functools.partial(_ffn_kernel, tm=tm, halo=halo, stride=stride, tiles_per_seq=tiles_per_seq)
    scratch = [pltpu.VMEM((tm, d), BF16), pltpu.VMEM((d, tf), BF16), pltpu.VMEM((d, tf), BF16),
               pltpu.VMEM((tf, d), BF16), pltpu.VMEM((halo + tm, tf), F32)]
    if tiles_per_seq > 1:
        scratch.append(pltpu.VMEM((nj, halo, tf), F32))
    vmem = (tm * d * 4 + 2 * tm * d * 4 + tm * d * 2 + 4 * d * tf * 4 + 2 * tf * d * 4
            + 3 * d * tf * 2 + (halo + tm) * tf * 4 + 8 * tm * tf * 4 + nj * halo * tf * 4)

    def wd_spec(q):
        return _layer_block(l, (tfq, d), lambda i, j: (j * WD_SPLIT + q, 0))

    return pl.pallas_call(
        kern,
        grid=(ni, nj),
        in_specs=([_single((tm, d), lambda i, j: (i, 0)),
                   pl.BlockSpec((halo, tf), lambda i, j: (0, j))]
                  + _k_split_specs(l, d, tf, lambda j: j)
                  + _k_split_specs(l, d, tf, lambda j: j + nj)
                  + [wd_spec(q) for q in range(WD_SPLIT)]
                  + [_layer_block(l, (FFN_CONV_WIDTH, tf), lambda i, j: (0, j)),
                     _layer_block(l, (1, tf), lambda i, j: (0, j)),
                     _layer_block(l, (1, d), lambda i, j: (0, 0)),
                     _layer_block(l, (1, d), lambda i, j: (0, 0))]),
        out_specs=[pl.BlockSpec((tm, d), lambda i, j: (i, 0)),
                   pl.BlockSpec((None, halo, tf), lambda i, j: (i, 0, j))],
        out_shape=[jax.ShapeDtypeStruct((m, d), F32),
                   jax.ShapeDtypeStruct((ni, halo, f), F32)],
        scratch_shapes=scratch,
        compiler_params=_params(("arbitrary", "arbitrary"), vmem),
        name=name,
    )(x, g_prev, *([w_up] * (2 * W_SPLIT)), *([w_down] * WD_SPLIT), w_dw, _rows3(b_dw),
      _rows3(ln_g), _rows3(ln_b))


def _softmax_pv(s, sink, v):
    m = jnp.maximum(jnp.max(s, axis=-1, keepdims=True), sink)
    p = jnp.exp(s - m)
    denom = jnp.sum(p, axis=-1, keepdims=True) + jnp.exp(sink - m)
    o = jnp.dot(p.astype(BF16), v, preferred_element_type=F32)
    return o / denom


def _attn_prompt_kernel(sinks_ref, q_ref, kvp_ref, kvc_ref, o_ref, *, jb):
    n = pl.program_id(1)
    kv_all = jnp.concatenate([kvp_ref[...], kvc_ref[...]], axis=0).astype(BF16)
    qi = lax.broadcasted_iota(jnp.int32, (WINDOW, 2 * WINDOW), 0)
    kj = lax.broadcasted_iota(jnp.int32, (WINDOW, 2 * WINDOW), 1)
    dist = WINDOW + qi - kj
    distf = dist.astype(F32)
    mask = (dist >= 0) & (dist <= WINDOW) & ((kj >= WINDOW) | (n > 0))
    for pair in range(N_HEADS // 2):
        outs = []
        for h in (2 * pair, 2 * pair + 1):
            kvh = h // GROUP
            qh = q_ref[:, h * HEAD_DIM:(h + 1) * HEAD_DIM]
            kh = kv_all[:, kvh * HEAD_DIM:(kvh + 1) * HEAD_DIM]
            vh = kv_all[:, KV_WIDTH + kvh * HEAD_DIM:KV_WIDTH + (kvh + 1) * HEAD_DIM]
            s = lax.dot_general(qh, kh, (((1,), (1,)), ((), ())), preferred_element_type=F32)
            s = jnp.where(mask, s - ALIBI_SLOPES[h] * distf, NEG_BIG)
            outs.append(_softmax_pv(s, sinks_ref[jb, h], vh))
        o_ref[:, pair * 2 * HEAD_DIM:(pair + 1) * 2 * HEAD_DIM] = (
            jnp.concatenate(outs, axis=1).astype(o_ref.dtype))


def _attn_prompt(q, kv, sinks, jb, *, name):
    m = q.shape[0]
    nb = SEQ // WINDOW
    return pl.pallas_call(
        functools.partial(_attn_prompt_kernel, jb=jb),
        grid=(BATCH, nb),
        in_specs=[pl.BlockSpec(memory_space=pltpu.SMEM),
                  pl.BlockSpec((WINDOW, D_MODEL), lambda b, n: (b * nb + n, 0)),
                  pl.BlockSpec((WINDOW, 2 * KV_WIDTH), lambda b, n: (b * nb + jnp.maximum(n - 1, 0), 0)),
                  pl.BlockSpec((WINDOW, 2 * KV_WIDTH), lambda b, n: (b * nb + n, 0))],
        out_specs=pl.BlockSpec((WINDOW, D_MODEL), lambda b, n: (b * nb + n, 0)),
        out_shape=jax.ShapeDtypeStruct((m, D_MODEL), BF16),
        compiler_params=_params(("arbitrary", "arbitrary"), 16 << 20),
        name=name,
    )(sinks, q, kv, kv)


def _attn_sample_kernel(sinks_ref, q_ref, kvn_ref, ck_ref, cv_ref, o_ref, *, jb):
    rows = GROUP * DEC_SEQ
    n_keys = WINDOW + DEC_SEQ
    row = lax.broadcasted_iota(jnp.int32, (rows, n_keys), 0)
    kj = lax.broadcasted_iota(jnp.int32, (rows, n_keys), 1)
    head_local = lax.shift_right_logical(row, DEC_SEQ.bit_length() - 1)
    dist = WINDOW + (row & (DEC_SEQ - 1)) - kj
    distf = dist.astype(F32)
    mask = (dist >= 0) & (dist <= WINDOW)
    head_col = lax.shift_right_logical(lax.broadcasted_iota(jnp.int32, (rows, 1), 0), DEC_SEQ.bit_length() - 1)
    kvn = kvn_ref[...]
    pieces = [None] * N_HEADS
    for kvh in range(N_KV_HEADS):
        lo, hi = kvh * HEAD_DIM, (kvh + 1) * HEAD_DIM
        q_st = jnp.concatenate(
            [q_ref[:, (kvh * GROUP + g) * HEAD_DIM:(kvh * GROUP + g + 1) * HEAD_DIM] for g in range(GROUP)],
            axis=0).astype(BF16)
        k_ext = jnp.concatenate([ck_ref[:, lo:hi], kvn[:, lo:hi]], axis=0).astype(BF16)
        v_ext = jnp.concatenate([cv_ref[:, lo:hi], kvn[:, KV_WIDTH + lo:KV_WIDTH + hi]], axis=0).astype(BF16)
        slope = jnp.zeros((rows, n_keys), F32)
        sink = jnp.zeros((rows, 1), F32)
        for g in range(GROUP):
            h = kvh * GROUP + g
            slope = jnp.where(head_local == g, ALIBI_SLOPES[h], slope)
            sink = jnp.where(head_col == g, sinks_ref[jb, h], sink)
        s = lax.dot_general(q_st, k_ext, (((1,), (1,)), ((), ())), preferred_element_type=F32)
        s = jnp.where(mask, s - slope * distf, NEG_BIG)
        o = _softmax_pv(s, sink, v_ext)
        for g in range(GROUP):
            pieces[kvh * GROUP + g] = o[g * DEC_SEQ:(g + 1) * DEC_SEQ, :]
    o_ref[...] = jnp.concatenate(pieces, axis=1)


def _attn_sample(q, kv_new, cache_k, cache_v, sinks, jb, *, name):
    ck = cache_k.reshape(DEC_BATCH, WINDOW, KV_WIDTH)
    cv = cache_v.reshape(DEC_BATCH, WINDOW, KV_WIDTH)
    return pl.pallas_call(
        functools.partial(_attn_sample_kernel, jb=jb),
        grid=(DEC_BATCH,),
        in_specs=[pl.BlockSpec(memory_space=pltpu.SMEM),
                  pl.BlockSpec((None, DEC_SEQ, D_MODEL), lambda b: (b, 0, 0)),
                  pl.BlockSpec((None, DEC_SEQ, 2 * KV_WIDTH), lambda b: (b, 0, 0)),
                  pl.BlockSpec((None, WINDOW, KV_WIDTH), lambda b: (b, 0, 0)),
                  pl.BlockSpec((None, WINDOW, KV_WIDTH), lambda b: (b, 0, 0))],
        out_specs=pl.BlockSpec((None, DEC_SEQ, D_MODEL), lambda b: (b, 0, 0)),
        out_shape=jax.ShapeDtypeStruct((DEC_BATCH, DEC_SEQ, D_MODEL), F32),
        compiler_params=_params(("arbitrary",), 8 << 20),
        name=name,
    )(sinks, q, kv_new, ck, cv)


def _to_time_major(a):
    return jnp.swapaxes(a, 0, 1).reshape(a.shape[1] * a.shape[0], a.shape[2])


def _to_batch_major(a2d, t):
    return jnp.swapaxes(a2d.reshape(t, DEC_BATCH, a2d.shape[1]), 0, 1)


def _trunk(x, conv_prev, ffn_prev, params, *, tag, tm, stride, tiles_per_seq, conv_tail_rows, attention):
    (a_w_in, a_b_in, a_w_dw, a_b_dw, a_norm_g, a_norm_b, a_w_out, a_b_out, w_kv, b_w_q, b_sinks, b_w_o,
     f_w_up, f_w_dw, f_b_dw, f_w_down, ln_mix_g, ln_mix_b, ln_ffn_g, ln_ffn_b) = params
    u_tails, tails = [], []
    kv = None
    for l in range(DEPTH):
        if l < N_A_LAYERS:
            c, u_tail = _glu_conv(x, a_w_in, a_b_in, conv_prev[l], a_w_dw, a_b_dw, l, tm=tm, tn=GLU_TN,
                                  stride=stride, tiles_per_seq=tiles_per_seq, tail_rows=conv_tail_rows,
                                  name=f"{tag}_gluconv{l}")
            u_tails.append(u_tail)
            x = _matmul_res_ln(c, (a_norm_g, a_norm_b), a_w_out, a_b_out, x, ln_mix_g, ln_mix_b, l, l,
                               tm=tm, tn=OUT_TN, name=f"{tag}_aout{l}")
        else:
            jb = l - N_A_LAYERS
            o = attention(x, kv, jb)
            x = _matmul_res_ln(o, None, b_w_o, None, x, ln_mix_g, ln_mix_b, l, jb,
                               tm=tm, tn=OUT_TN, name=f"{tag}_bout{l}")
        x, tail = _conv_ffn(x, ffn_prev[l], f_w_up, f_w_dw, f_b_dw, f_w_down, ln_ffn_g, ln_ffn_b, l,
                            tm=tm, tf=FFN_TF, stride=stride, tiles_per_seq=tiles_per_seq,
                            name=f"{tag}_ffn{l}")
        tails.append(tail)
        if l == N_A_LAYERS - 1:
            kv = _matmul(x, w_kv[None], 0, tm=tm, tn=2 * KV_WIDTH, out_dtype=F32, name=f"{tag}_kv")
    return x, u_tails, tails, kv


def kernel(x_prompt, x_sample, state_conv_a, state_ffn_conv, cache_k_win, cache_v_win, a_w_in, a_b_in, a_w_dw, a_b_dw, a_norm_g, a_norm_b, a_w_out, a_b_out, w_kv, b_w_q, b_sinks, b_w_o, f_w_up, f_w_dw, f_b_dw, f_w_down, ln_mix_g, ln_mix_b, ln_ffn_g, ln_ffn_b):
    params = (a_w_in, a_b_in, a_w_dw, a_b_dw, a_norm_g, a_norm_b, a_w_out, a_b_out, w_kv, b_w_q, b_sinks, b_w_o,
              f_w_up, f_w_dw, f_b_dw, f_w_down, ln_mix_g, ln_mix_b, ln_ffn_g, ln_ffn_b)
    q_scale = HEAD_DIM ** -0.5

    def attn_p(x, kv, jb):
        q = _matmul(x, b_w_q, jb, tm=PROMPT_TM, tn=512, out_dtype=BF16, scale=q_scale, name=f"p_q{jb}")
        return _attn_prompt(q, kv, b_sinks, jb, name=f"p_attn{jb}")

    xp = x_prompt.reshape(PROMPT_ROWS, D_MODEL)
    zero_conv = jnp.zeros((CONV_HALO, D_MODEL), F32)
    zero_ffn = jnp.zeros((SUBLANES, D_FF), F32)
    yp, us_p, tails_p, kv_p = _trunk(xp, [zero_conv] * N_A_LAYERS, [zero_ffn] * DEPTH, params,
                                     tag="p", tm=PROMPT_TM, stride=1, tiles_per_seq=SEQ // PROMPT_TM,
                                     conv_tail_rows=CONV_HALO, attention=attn_p)
    y_prompt = yp.reshape(BATCH, SEQ, D_MODEL)
    tps = SEQ // PROMPT_TM
    conv_a_prompt = jnp.stack([u[tps - 1::tps, CONV_HALO - (CONV_A_WIDTH - 1):] for u in us_p])
    ffn_conv_prompt = jnp.stack([t[tps - 1::tps, SUBLANES - (FFN_CONV_WIDTH - 1):] for t in tails_p])
    kv_p3 = kv_p.reshape(BATCH, SEQ, 2 * KV_WIDTH)[:, SEQ - WINDOW:]
    k_win_prompt = kv_p3[..., :KV_WIDTH].reshape(BATCH, WINDOW, N_KV_HEADS, HEAD_DIM)
    v_win_prompt = kv_p3[..., KV_WIDTH:].reshape(BATCH, WINDOW, N_KV_HEADS, HEAD_DIM)

    def attn_s(x, kv, jb):
        q = _matmul(x, b_w_q, jb, tm=SAMPLE_ROWS, tn=512, out_dtype=F32, scale=q_scale, name=f"s_q{jb}")
        o = _attn_sample(_to_batch_major(q, DEC_SEQ), _to_batch_major(kv, DEC_SEQ),
                         cache_k_win, cache_v_win, b_sinks, jb, name=f"s_attn{jb}")
        return _to_time_major(o)

    xs = _to_time_major(x_sample)
    conv_prev_s = [_to_time_major(state_conv_a[l]) for l in range(N_A_LAYERS)]
    ffn_prev_s = [_to_time_major(state_ffn_conv[l]) for l in range(DEPTH)]
    ys, us_s, tails_s, kv_s = _trunk(xs, conv_prev_s, ffn_prev_s, params,
                                     tag="s", tm=SAMPLE_ROWS, stride=DEC_BATCH, tiles_per_seq=1,
                                     conv_tail_rows=SAMPLE_ROWS, attention=attn_s)
    y_sample = _to_batch_major(ys, DEC_SEQ)
    conv_a_sample = jnp.stack([
        _to_batch_major(jnp.concatenate([conv_prev_s[l][SAMPLE_ROWS:], us_s[l][0]], axis=0), CONV_A_WIDTH - 1)
        for l in range(N_A_LAYERS)])
    ffn_conv_sample = jnp.stack([_to_batch_major(t[0], FFN_CONV_WIDTH - 1) for t in tails_s])
    kv_s3 = _to_batch_major(kv_s, DEC_SEQ)
    k_new = kv_s3[..., :KV_WIDTH].reshape(DEC_BATCH, DEC_SEQ, N_KV_HEADS, HEAD_DIM)
    v_new = kv_s3[..., KV_WIDTH:].reshape(DEC_BATCH, DEC_SEQ, N_KV_HEADS, HEAD_DIM)
    k_win_sample = jnp.concatenate([cache_k_win[:, DEC_SEQ:], k_new], axis=1)
    v_win_sample = jnp.concatenate([cache_v_win[:, DEC_SEQ:], v_new], axis=1)

    return (y_prompt, y_sample, conv_a_prompt, ffn_conv_prompt, k_win_prompt, v_win_prompt,
            conv_a_sample, ffn_conv_sample, k_win_sample, v_win_sample)
```

```python
import functools

import jax
import jax.numpy as jnp
from jax import lax
from jax.experimental import pallas as pl
from jax.experimental.pallas import tpu as pltpu

D_MODEL = 2048
BATCH = 4
SEQ = 2048
DEPTH = 4
DEC_BATCH = 32
DEC_SEQ = 8
N_A_LAYERS = DEPTH // 2
CONV_A_WIDTH = 31
FFN_CONV_WIDTH = 3
D_FF = ((8 * D_MODEL // 3 + 255) // 256) * 256
HEAD_DIM = 64
N_HEADS = D_MODEL // HEAD_DIM
N_KV_HEADS = max(1, N_HEADS // 8)
GROUP = N_HEADS // N_KV_HEADS
KV_WIDTH = N_KV_HEADS * HEAD_DIM
WINDOW = 128
DN_ALPHA = (2.0 * DEPTH) ** 0.25
LN_EPS = 1e-5
NEG_BIG = -1e30
ALIBI_SLOPES = tuple(2.0 ** (-8.0 * (h + 1) / N_HEADS) for h in range(N_HEADS))

BF16 = jnp.bfloat16
F32 = jnp.float32

V7X_VMEM_BYTES = 64 * 1024 * 1024
SUBLANES = 8

PROMPT_ROWS = BATCH * SEQ
SAMPLE_ROWS = DEC_BATCH * DEC_SEQ
PROMPT_TM = 1024
FFN_TF = 256
OUT_TM = 512
OUT_TN = 512
MASK_PENALTY = 1e30
GLU_TN = 256
W_SPLIT = 4
WD_SPLIT = 2
CONV_HALO = 32


def _params(semantics, vmem_bytes):
    limit = min(int(vmem_bytes * 1.25) + (4 << 20), V7X_VMEM_BYTES - (6 << 20))
    return pltpu.CompilerParams(dimension_semantics=semantics, vmem_limit_bytes=limit)


def _gelu_exact(z):
    return 0.5 * z * (1.0 + lax.erf(z * (0.5 ** 0.5)))


def _single(shape, index_map):
    return pl.BlockSpec(shape, index_map, pipeline_mode=pl.Buffered(1))


def _layer_block(l, shape, index_map):
    return pl.BlockSpec((None,) + shape, lambda *g: (l,) + index_map(*g))


def _rows3(p):
    return p.reshape(p.shape[0], 1, p.shape[1])


def _k_split_specs(l, k, tn, col_map):
    kq = k // W_SPLIT

    def spec(q):
        return _layer_block(l, (kq, tn), lambda i, j: (q, col_map(j)))

    return [spec(q) for q in range(W_SPLIT)]


def _stage_bf16(w_refs, wb_ref):
    kq = w_refs[0].shape[0]
    for q, w_ref in enumerate(w_refs):
        wb_ref[q * kq:(q + 1) * kq, :] = w_ref[...].astype(BF16)


def _mm_kernel(x_ref, *refs, scale, head_major):
    w_refs, (o_ref, xb_ref, wb_ref) = refs[:W_SPLIT], refs[W_SPLIT:]

    @pl.when(pl.program_id(1) == 0)
    def _():
        xb_ref[...] = x_ref[...].astype(BF16)

    _stage_bf16(w_refs, wb_ref)
    acc = jnp.dot(xb_ref[...], wb_ref[...], preferred_element_type=F32)
    if scale != 1.0:
        acc = acc * scale
    if head_major:
        for h in range(o_ref.shape[0]):
            o_ref[h] = acc[:, h * HEAD_DIM:(h + 1) * HEAD_DIM].astype(o_ref.dtype)
    else:
        o_ref[...] = acc.astype(o_ref.dtype)


def _matmul(x, w, l, *, tm, tn, out_dtype, scale=1.0, head_major=False, name):
    m, k = x.shape
    n = w.shape[2]
    vmem = 2 * tm * k * 4 + tm * k * 2 + 2 * k * tn * 4 + k * tn * 2 + 5 * tm * tn * 4
    if head_major:
        heads = tn // HEAD_DIM
        out_spec = pl.BlockSpec((heads, tm, HEAD_DIM), lambda i, j: (j, i, 0))
        out_shape = jax.ShapeDtypeStruct((n // HEAD_DIM, m, HEAD_DIM), out_dtype)
    else:
        out_spec = pl.BlockSpec((tm, tn), lambda i, j: (i, j))
        out_shape = jax.ShapeDtypeStruct((m, n), out_dtype)
    return pl.pallas_call(
        functools.partial(_mm_kernel, scale=scale, head_major=head_major),
        grid=(m // tm, n // tn),
        in_specs=[pl.BlockSpec((tm, k), lambda i, j: (i, 0))] + _k_split_specs(l, k, tn, lambda j: j),
        out_specs=out_spec,
        out_shape=out_shape,
        scratch_shapes=[pltpu.VMEM((tm, k), BF16), pltpu.VMEM((k, tn), BF16)],
        compiler_params=_params(("arbitrary", "arbitrary"), vmem),
        name=name,
    )(x, *([w] * W_SPLIT))


def _dwconv_unit_stride(ext_ref, w_ref, bias, c_ref, tm):
    tc = c_ref.shape[1]
    sub = lax.broadcasted_iota(jnp.int32, (SUBLANES, tc), 0)
    first = CONV_HALO - (CONV_A_WIDTH - 1)
    n_blocks = tm // SUBLANES
    n_ext = CONV_HALO // SUBLANES + 1
    t_prev = None
    for blk in range(n_blocks + 1):
        base = blk * SUBLANES
        e = [ext_ref[base + a * SUBLANES:base + (a + 1) * SUBLANES, :] if base + (a + 1) * SUBLANES <= CONV_HALO + tm
             else None for a in range(n_ext)]
        t = [None] * SUBLANES
        for p in range(SUBLANES):
            if blk == n_blocks and p == 0:
                continue
            for a in range(n_ext):
                k = a * SUBLANES + p - first
                if 0 <= k < CONV_A_WIDTH:
                    term = w_ref[k * SUBLANES:(k + 1) * SUBLANES, :] * e[a]
                    t[p] = term if t[p] is None else t[p] + term
        if blk > 0:
            acc = bias + t_prev[0]
            for p in range(1, SUBLANES):
                mixed = jnp.where(sub >= p, t_prev[p], t[p])
                acc = acc + pltpu.roll(mixed, SUBLANES - p, axis=0)
            c_ref[base - SUBLANES:base, :] = acc
        t_prev = t


def _dwconv_aligned_stride(ext_ref, w_ref, bias, c_ref, tm, halo, stride):
    first = halo - (CONV_A_WIDTH - 1) * stride
    for blk in range(tm // SUBLANES):
        acc = bias
        for k in range(CONV_A_WIDTH):
            off = first + k * stride + blk * SUBLANES
            acc = acc + w_ref[k * SUBLANES:(k + 1) * SUBLANES, :] * ext_ref[off:off + SUBLANES, :]
        c_ref[blk * SUBLANES:(blk + 1) * SUBLANES, :] = acc


def _glu_conv_kernel(x_ref, *refs, tm, halo, stride, tiles_per_seq):
    wa_refs, wg_refs = refs[:W_SPLIT], refs[W_SPLIT:2 * W_SPLIT]
    ba_ref, bg_ref, prev_ref, wdw_ref, bdw_ref, c_ref, tail_ref, xb_ref, wab_ref, wgb_ref, ext_ref = (
        refs[2 * W_SPLIT:2 * W_SPLIT + 11])
    i = pl.program_id(0)
    j = pl.program_id(1)

    @pl.when(j == 0)
    def _():
        xb_ref[...] = x_ref[...].astype(BF16)

    if tiles_per_seq > 1:
        carry_ref = refs[2 * W_SPLIT + 11]

        @pl.when((i == 0) & (j == 0))
        def _():
            carry_ref[...] = jnp.zeros_like(carry_ref)

    _stage_bf16(wa_refs, wab_ref)
    _stage_bf16(wg_refs, wgb_ref)
    xb = xb_ref[...]
    a = jnp.dot(xb, wab_ref[...], preferred_element_type=F32) + ba_ref[...]
    g = jnp.dot(xb, wgb_ref[...], preferred_element_type=F32) + bg_ref[...]
    u = a * jax.nn.sigmoid(g)

    if tiles_per_seq > 1:
        ext_ref[0:halo, :] = jnp.where(i % tiles_per_seq == 0, prev_ref[...], carry_ref[j])
        carry_ref[j] = u[tm - halo:, :]
    else:
        ext_ref[0:halo, :] = prev_ref[...]
    ext_ref[halo:halo + tm, :] = u
    tail_ref[...] = u[tm - tail_ref.shape[0]:, :]

    bias = jnp.broadcast_to(bdw_ref[...], (SUBLANES, c_ref.shape[1]))
    if stride == 1:
        _dwconv_unit_stride(ext_ref, wdw_ref, bias, c_ref, tm)
    else:
        _dwconv_aligned_stride(ext_ref, wdw_ref, bias, c_ref, tm, halo, stride)


def _glu_conv(x, w_in, b_in, prev, w_dw, b_dw, l, *, tm, tn, stride, tiles_per_seq, tail_rows, name):
    m, k = x.shape
    n = w_in.shape[2] // 2
    nj = n // tn
    ni = m // tm
    halo = prev.shape[0]
    b3 = _rows3(b_in)
    w_rep = jnp.repeat(w_dw, SUBLANES, axis=1)
    kern = functools.partial(_glu_conv_kernel, tm=tm, halo=halo, stride=stride, tiles_per_seq=tiles_per_seq)
    scratch = [pltpu.VMEM((tm, k), BF16), pltpu.VMEM((k, tn), BF16), pltpu.VMEM((k, tn), BF16),
               pltpu.VMEM((halo + tm, tn), F32)]
    if tiles_per_seq > 1:
        scratch.append(pltpu.VMEM((nj, halo, tn), F32))
    vmem = (2 * tm * k * 4 + tm * k * 2 + 4 * k * tn * 4 + 2 * k * tn * 2 + 8 * tm * tn * 4
            + 3 * halo * tn * 4 + (halo + tm) * tn * 4 + nj * halo * tn * 4)
    return pl.pallas_call(
        kern,
        grid=(ni, nj),
        in_specs=([pl.BlockSpec((tm, k), lambda i, j: (i, 0))]
                  + _k_split_specs(l, k, tn, lambda j: j)
                  + _k_split_specs(l, k, tn, lambda j: j + nj)
                  + [_layer_block(l, (1, tn), lambda i, j: (0, j)),
                     _layer_block(l, (1, tn), lambda i, j: (0, j + nj)),
                     pl.BlockSpec((halo, tn), lambda i, j: (0, j)),
                     _layer_block(l, (CONV_A_WIDTH * SUBLANES, tn), lambda i, j: (0, j)),
                     _layer_block(l, (1, tn), lambda i, j: (0, j))]),
        out_specs=[pl.BlockSpec((tm, tn), lambda i, j: (i, j)),
                   pl.BlockSpec((None, tail_rows, tn), lambda i, j: (i, 0, j))],
        out_shape=[jax.ShapeDtypeStruct((m, n), F32),
                   jax.ShapeDtypeStruct((ni, tail_rows, n), F32)],
        scratch_shapes=scratch,
        compiler_params=_params(("arbitrary", "arbitrary"), vmem),
        name=name,
    )(x, *([w_in] * (2 * W_SPLIT)), b3, b3, prev, w_rep, _rows3(b_dw))


def _layer_norm(z, g, b):
    mu = jnp.mean(z, axis=-1, keepdims=True)
    zc = z - mu
    var = jnp.mean(zc * zc, axis=-1, keepdims=True)
    return zc * lax.rsqrt(var + LN_EPS) * g + b


def _residual_ln_chunks(x_ref, acc_ref, g_ref, b_ref, o_ref, tn):
    n_c = acc_ref.shape[0]
    d = n_c * tn
    tot = None
    for jj in range(n_c):
        z = DN_ALPHA * x_ref[:, jj * tn:(jj + 1) * tn] + acc_ref[jj]
        acc_ref[jj] = z
        part = jnp.sum(z, axis=-1, keepdims=True)
        tot = part if tot is None else tot + part
    mu = tot * (1.0 / d)
    sq = None
    for jj in range(n_c):
        zc = acc_ref[jj] - mu
        part = jnp.sum(zc * zc, axis=-1, keepdims=True)
        sq = part if sq is None else sq + part
    rs = lax.rsqrt(sq * (1.0 / d) + LN_EPS)
    for jj in range(n_c):
        o_ref[:, jj * tn:(jj + 1) * tn] = ((acc_ref[jj] - mu) * rs * g_ref[:, jj * tn:(jj + 1) * tn]
                                           + b_ref[:, jj * tn:(jj + 1) * tn])


def _mm_res_ln_kernel(a_ref, *refs, swish_norm, has_bias, tn):
    w_refs, refs = refs[:W_SPLIT], list(refs[W_SPLIT:])
    gn_ref, bn_ref = (refs.pop(0), refs.pop(0)) if swish_norm else (None, None)
    bias_ref = refs.pop(0) if has_bias else None
    x_ref, g_ref, b_ref, o_ref, acc_ref, wb_ref = refs[:6]
    j = pl.program_id(1)

    if swish_norm:
        ab_ref = refs[6]

        @pl.when(j == 0)
        def _():
            y = _layer_norm(a_ref[...], gn_ref[...], bn_ref[...])
            ab_ref[...] = (y * jax.nn.sigmoid(y)).astype(BF16)

        lhs = ab_ref[...]
    else:
        lhs = a_ref[...].astype(BF16)

    _stage_bf16(w_refs, wb_ref)
    part = jnp.dot(lhs, wb_ref[...], preferred_element_type=F32)
    if has_bias:
        part = part + bias_ref[...]
    acc_ref[j] = part

    @pl.when(j == pl.num_programs(1) - 1)
    def _():
        _residual_ln_chunks(x_ref, acc_ref, g_ref, b_ref, o_ref, tn)


def _matmul_res_ln(a, norm, w, bias, x, g, b, l, jb, *, tm, tn, name):
    m, kdim = a.shape
    n = w.shape[2]
    nj = n // tn
    has_bias = bias is not None
    swish_norm = norm is not None
    in_specs = [pl.BlockSpec((tm, kdim), lambda i, j: (i, 0))] + _k_split_specs(jb, kdim, tn, lambda j: j)
    args = [a] + [w] * W_SPLIT
    scratch = [pltpu.VMEM((nj, tm, tn), F32), pltpu.VMEM((kdim, tn), BF16)]
    if swish_norm:
        in_specs += [_layer_block(jb, (1, kdim), lambda i, j: (0, 0)),
                     _layer_block(jb, (1, kdim), lambda i, j: (0, 0))]
        args += [_rows3(norm[0]), _rows3(norm[1])]
        scratch.append(pltpu.VMEM((tm, kdim), BF16))
    if has_bias:
        in_specs.append(_layer_block(jb, (1, tn), lambda i, j: (0, j)))
        args.append(_rows3(bias))
    in_specs += [pl.BlockSpec((tm, n), lambda i, j: (i, 0)),
                 _layer_block(l, (1, n), lambda i, j: (0, 0)),
                 _layer_block(l, (1, n), lambda i, j: (0, 0))]
    args += [x, _rows3(g), _rows3(b)]
    vmem = (2 * tm * kdim * a.dtype.itemsize + 2 * kdim * tn * 4 + kdim * tn * 2 + 2 * tm * n * 4
            + 2 * tm * n * 4 + tm * n * 4 + 2 * tm * tn * 4 + (tm * kdim * 2 if swish_norm else 0))
    return pl.pallas_call(
        functools.partial(_mm_res_ln_kernel, swish_norm=swish_norm, has_bias=has_bias, tn=tn),
        grid=(m // tm, nj),
        in_specs=in_specs,
        out_specs=pl.BlockSpec((tm, n), lambda i, j: (i, 0)),
        out_shape=jax.ShapeDtypeStruct((m, n), F32),
        scratch_shapes=scratch,
        compiler_params=_params(("arbitrary", "arbitrary"), vmem),
        name=name,
    )(*args)


def _ffn_kernel(x_ref, gprev_ref, *refs, tm, halo, stride, tiles_per_seq):
    wg_refs, wv_refs = refs[:W_SPLIT], refs[W_SPLIT:2 * W_SPLIT]
    wd_refs = refs[2 * W_SPLIT:2 * W_SPLIT + WD_SPLIT]
    wdw_ref, bdw_ref, lg_ref, lb_ref, y_ref, tail_ref, xb_ref, wgb_ref, wvb_ref, wdb_ref, ext_ref = (
        refs[2 * W_SPLIT + WD_SPLIT:2 * W_SPLIT + WD_SPLIT + 11])
    i = pl.program_id(0)
    j = pl.program_id(1)

    @pl.when(j == 0)
    def _():
        xb_ref[...] = x_ref[...].astype(BF16)
        y_ref[...] = jnp.zeros_like(y_ref)

    if tiles_per_seq > 1:
        carry_ref = refs[2 * W_SPLIT + WD_SPLIT + 11]

        @pl.when((i == 0) & (j == 0))
        def _():
            carry_ref[...] = jnp.zeros_like(carry_ref)

    _stage_bf16(wg_refs, wgb_ref)
    _stage_bf16(wv_refs, wvb_ref)
    _stage_bf16(wd_refs, wdb_ref)
    xb = xb_ref[...]
    g = jnp.dot(xb, wgb_ref[...], preferred_element_type=F32)
    v = jnp.dot(xb, wvb_ref[...], preferred_element_type=F32)

    if tiles_per_seq > 1:
        prev = jnp.where(i % tiles_per_seq == 0, gprev_ref[...], carry_ref[j])
    else:
        prev = gprev_ref[...]
    ext_ref[0:halo, :] = prev
    ext_ref[halo:halo + tm, :] = g
    g_last = g[tm - halo:, :]
    tail_ref[...] = g_last
    if tiles_per_seq > 1:
        carry_ref[j] = g_last

    c = (wdw_ref[2:3, :] * g
         + wdw_ref[1:2, :] * ext_ref[halo - stride:halo - stride + tm, :]
         + wdw_ref[0:1, :] * ext_ref[halo - 2 * stride:halo - 2 * stride + tm, :]
         + bdw_ref[...])
    h = (_gelu_exact(c) * v).astype(BF16)
    y_ref[...] += jnp.dot(h, wdb_ref[...], preferred_element_type=F32)

    @pl.when(j == pl.num_programs(1) - 1)
    def _():
        y_ref[...] = _layer_norm(DN_ALPHA * x_ref[...] + y_ref[...], lg_ref[...], lb_ref[...])


def _conv_ffn(x, g_prev, w_up, w_dw, b_dw, w_down, ln_g, ln_b, l, *, tm, tf, stride, tiles_per_seq, name):
    m, d = x.shape
    f = w_down.shape[1]
    halo = g_prev.shape[0]
    nj = f // tf
    ni = m // tm
    tfq = tf // WD_SPLIT
    kern = functools.partial(_ffn_kernel, tm=tm, halo=halo, stride=stride, tiles_per_seq=tiles_per_seq)
    scratch = [pltpu.VMEM((tm, d), BF16), pltpu.VMEM((d, tf), BF16), pltpu.VMEM((d, tf), BF16),
               pltpu.VMEM((tf, d), BF16), pltpu.VMEM((halo + tm, tf), F32)]
    if tiles_per_seq > 1:
        scratch.append(pltpu.VMEM((nj, halo, tf), F32))
    vmem = (tm * d * 4 + 2 * tm * d * 4 + tm * d * 2 + 4 * d * tf * 4 + 2 * tf * d * 4
            + 3 * d * tf * 2 + (halo + tm) * tf * 4 + 8 * tm * tf * 4 + nj * halo * tf * 4)

    def wd_spec(q):
        return _layer_block(l, (tfq, d), lambda i, j: (j * WD_SPLIT + q, 0))

    return pl.pallas_call(
        kern,
        grid=(ni, nj),
        in_specs=([_single((tm, d), lambda i, j: (i, 0)),
                   pl.BlockSpec((halo, tf), lambda i, j: (0, j))]
                  + _k_split_specs(l, d, tf, lambda j: j)
                  + _k_split_specs(l, d, tf, lambda j: j + nj)
                  + [wd_spec(q) for q in range(WD_SPLIT)]
                  + [_layer_block(l, (FFN_CONV_WIDTH, tf), lambda i, j: (0, j)),
                     _layer_block(l, (1, tf), lambda i, j: (0, j)),
                     _layer_block(l, (1, d), lambda i, j: (0, 0)),
                     _layer_block(l, (1, d), lambda i, j: (0, 0))]),
        out_specs=[pl.BlockSpec((tm, d), lambda i, j: (i, 0)),
                   pl.BlockSpec((None, halo, tf), lambda i, j: (i, 0, j))],
        out_shape=[jax.ShapeDtypeStruct((m, d), F32),
                   jax.ShapeDtypeStruct((ni, halo, f), F32)],
        scratch_shapes=scratch,
        compiler_params=_params(("arbitrary", "arbitrary"), vmem),
        name=name,
    )(x, g_prev, *([w_up] * (2 * W_SPLIT)), *([w_down] * WD_SPLIT), w_dw, _rows3(b_dw),
      _rows3(ln_g), _rows3(ln_b))


def _softmax_pv(s, sink, v):
    m = jnp.maximum(jnp.max(s, axis=-1, keepdims=True), sink)
    p = jnp.exp(s - m)
    denom = jnp.sum(p, axis=-1, keepdims=True) + jnp.exp(sink - m)
    o = jnp.dot(p.astype(BF16), v, preferred_element_type=F32)
    return o / denom


def _alibi_bias_table():
    qi = jnp.arange(WINDOW)[:, None]
    kj = jnp.arange(2 * WINDOW)[None, :]
    dist = WINDOW + qi - kj
    band = (dist >= 0) & (dist <= WINDOW)
    slopes = 2.0 ** (-8.0 * jnp.arange(1, N_HEADS + 1, dtype=F32) / N_HEADS)
    pen = slopes[:, None, None] * dist.astype(F32)[None]
    first = band & (kj >= WINDOW)
    table = jnp.stack([jnp.where(first[None], pen, MASK_PENALTY), jnp.where(band[None], pen, MASK_PENALTY)])
    return table.reshape(2, N_HEADS * WINDOW, 2 * WINDOW)


def _attn_prompt_kernel(sinks_ref, q_ref, kvp_ref, kvc_ref, bias_ref, o_ref, *, jb):
    kv_all = jnp.concatenate([kvp_ref[...], kvc_ref[...]], axis=0).astype(BF16)
    for pair in range(N_HEADS // 2):
        outs = []
        for h in (2 * pair, 2 * pair + 1):
            kvh = h // GROUP
            kh = kv_all[:, kvh * HEAD_DIM:(kvh + 1) * HEAD_DIM]
            vh = kv_all[:, KV_WIDTH + kvh * HEAD_DIM:KV_WIDTH + (kvh + 1) * HEAD_DIM]
            s = lax.dot_general(q_ref[h], kh, (((1,), (1,)), ((), ())), preferred_element_type=F32)
            s = s - bias_ref[h * WINDOW:(h + 1) * WINDOW, :]
            outs.append(_softmax_pv(s, sinks_ref[jb, h], vh))
        o_ref[:, pair * 2 * HEAD_DIM:(pair + 1) * 2 * HEAD_DIM] = (
            jnp.concatenate(outs, axis=1).astype(o_ref.dtype))


def _attn_prompt(q, kv, sinks, bias, jb, *, name):
    m = q.shape[1]
    nb = SEQ // WINDOW
    return pl.pallas_call(
        functools.partial(_attn_prompt_kernel, jb=jb),
        grid=(BATCH, nb),
        in_specs=[pl.BlockSpec(memory_space=pltpu.SMEM),
                  pl.BlockSpec((N_HEADS, WINDOW, HEAD_DIM), lambda b, n: (0, b * nb + n, 0)),
                  pl.BlockSpec((WINDOW, 2 * KV_WIDTH), lambda b, n: (b * nb + jnp.maximum(n - 1, 0), 0)),
                  pl.BlockSpec((WINDOW, 2 * KV_WIDTH), lambda b, n: (b * nb + n, 0)),
                  pl.BlockSpec((None, N_HEADS * WINDOW, 2 * WINDOW), lambda b, n: (jnp.minimum(n, 1), 0, 0))],
        out_specs=pl.BlockSpec((WINDOW, D_MODEL), lambda b, n: (b * nb + n, 0)),
        out_shape=jax.ShapeDtypeStruct((m, D_MODEL), BF16),
        compiler_params=_params(("arbitrary", "arbitrary"), 32 << 20),
        name=name,
    )(sinks, q, kv, kv, bias)


def _attn_sample_kernel(sinks_ref, q_ref, kvn_ref, ck_ref, cv_ref, o_ref, *, jb):
    rows = GROUP * DEC_SEQ
    n_keys = WINDOW + DEC_SEQ
    row = lax.broadcasted_iota(jnp.int32, (rows, n_keys), 0)
    kj = lax.broadcasted_iota(jnp.int32, (rows, n_keys), 1)
    head_local = lax.shift_right_logical(row, DEC_SEQ.bit_length() - 1)
    dist = WINDOW + (row & (DEC_SEQ - 1)) - kj
    distf = dist.astype(F32)
    mask = (dist >= 0) & (dist <= WINDOW)
    head_col = lax.shift_right_logical(lax.broadcasted_iota(jnp.int32, (rows, 1), 0), DEC_SEQ.bit_length() - 1)
    kvn = kvn_ref[...]
    pieces = [None] * N_HEADS
    for kvh in range(N_KV_HEADS):
        lo, hi = kvh * HEAD_DIM, (kvh + 1) * HEAD_DIM
        q_st = jnp.concatenate(
            [q_ref[:, (kvh * GROUP + g) * HEAD_DIM:(kvh * GROUP + g + 1) * HEAD_DIM] for g in range(GROUP)],
            axis=0).astype(BF16)
        k_ext = jnp.concatenate([ck_ref[:, lo:hi], kvn[:, lo:hi]], axis=0).astype(BF16)
        v_ext = jnp.concatenate([cv_ref[:, lo:hi], kvn[:, KV_WIDTH + lo:KV_WIDTH + hi]], axis=0).astype(BF16)
        slope = jnp.zeros((rows, n_keys), F32)
        sink = jnp.zeros((rows, 1), F32)
        for g in range(GROUP):
            h = kvh * GROUP + g
            slope = jnp.where(head_local == g, ALIBI_SLOPES[h], slope)
            sink = jnp.where(head_col == g, sinks_ref[jb, h], sink)
        s = lax.dot_general(q_st, k_ext, (((1,), (1,)), ((), ())), preferred_element_type=F32)
        s = jnp.where(mask, s - slope * distf, NEG_BIG)
        o = _softmax_pv(s, sink, v_ext)
        for g in range(GROUP):
            pieces[kvh * GROUP + g] = o[g * DEC_SEQ:(g + 1) * DEC_SEQ, :]
    o_ref[...] = jnp.concatenate(pieces, axis=1)


def _attn_sample(q, kv_new, cache_k, cache_v, sinks, jb, *, name):
    ck = cache_k.reshape(DEC_BATCH, WINDOW, KV_WIDTH)
    cv = cache_v.reshape(DEC_BATCH, WINDOW, KV_WIDTH)
    return pl.pallas_call(
        functools.partial(_attn_sample_kernel, jb=jb),
        grid=(DEC_BATCH,),
        in_specs=[pl.BlockSpec(memory_space=pltpu.SMEM),
                  pl.BlockSpec((None, DEC_SEQ, D_MODEL), lambda b: (b, 0, 0)),
                  pl.BlockSpec((None, DEC_SEQ, 2 * KV_WIDTH), lambda b: (b, 0, 0)),
                  pl.BlockSpec((None, WINDOW, KV_WIDTH), lambda b: (b, 0, 0)),
                  pl.BlockSpec((None, WINDOW, KV_WIDTH), lambda b: (b, 0, 0))],
        out_specs=pl.BlockSpec((None, DEC_SEQ, D_MODEL), lambda b: (b, 0, 0)),
        out_shape=jax.ShapeDtypeStruct((DEC_BATCH, DEC_SEQ, D_MODEL), F32),
        compiler_params=_params(("arbitrary",), 8 << 20),
        name=name,
    )(sinks, q, kv_new, ck, cv)


def _to_time_major(a):
    return jnp.swapaxes(a, 0, 1).reshape(a.shape[1] * a.shape[0], a.shape[2])


def _to_batch_major(a2d, t):
    return jnp.swapaxes(a2d.reshape(t, DEC_BATCH, a2d.shape[1]), 0, 1)


def _trunk(x, conv_prev, ffn_prev, params, *, tag, tm, stride, tiles_per_seq, conv_tail_rows, attention):
    (a_w_in, a_b_in, a_w_dw, a_b_dw, a_norm_g, a_norm_b, a_w_out, a_b_out, w_kv, b_w_q, b_sinks, b_w_o,
     f_w_up, f_w_dw, f_b_dw, f_w_down, ln_mix_g, ln_mix_b, ln_ffn_g, ln_ffn_b) = params
    u_tails, tails = [], []
    kv = None
    for l in range(DEPTH):
        if l < N_A_LAYERS:
            c, u_tail = _glu_conv(x, a_w_in, a_b_in, conv_prev[l], a_w_dw, a_b_dw, l, tm=tm, tn=GLU_TN,
                                  stride=stride, tiles_per_seq=tiles_per_seq, tail_rows=conv_tail_rows,
                                  name=f"{tag}_gluconv{l}")
            u_tails.append(u_tail)
            x = _matmul_res_ln(c, (a_norm_g, a_norm_b), a_w_out, a_b_out, x, ln_mix_g, ln_mix_b, l, l,
                               tm=min(OUT_TM, tm), tn=OUT_TN, name=f"{tag}_aout{l}")
        else:
            jb = l - N_A_LAYERS
            o = attention(x, kv, jb)
            x = _matmul_res_ln(o, None, b_w_o, None, x, ln_mix_g, ln_mix_b, l, jb,
                               tm=min(OUT_TM, tm), tn=OUT_TN, name=f"{tag}_bout{l}")
        x, tail = _conv_ffn(x, ffn_prev[l], f_w_up, f_w_dw, f_b_dw, f_w_down, ln_ffn_g, ln_ffn_b, l,
                            tm=tm, tf=FFN_TF, stride=stride, tiles_per_seq=tiles_per_seq,
                            name=f"{tag}_ffn{l}")
        tails.append(tail)
        if l == N_A_LAYERS - 1:
            kv = _matmul(x, w_kv[None], 0, tm=tm, tn=2 * KV_WIDTH, out_dtype=F32, name=f"{tag}_kv")
    return x, u_tails, tails, kv


def kernel(x_prompt, x_sample, state_conv_a, state_ffn_conv, cache_k_win, cache_v_win, a_w_in, a_b_in, a_w_dw, a_b_dw, a_norm_g, a_norm_b, a_w_out, a_b_out, w_kv, b_w_q, b_sinks, b_w_o, f_w_up, f_w_dw, f_b_dw, f_w_down, ln_mix_g, ln_mix_b, ln_ffn_g, ln_ffn_b):
    params = (a_w_in, a_b_in, a_w_dw, a_b_dw, a_norm_g, a_norm_b, a_w_out, a_b_out, w_kv, b_w_q, b_sinks, b_w_o,
              f_w_up, f_w_dw, f_b_dw, f_w_down, ln_mix_g, ln_mix_b, ln_ffn_g, ln_ffn_b)
    q_scale = HEAD_DIM ** -0.5
    alibi_bias = _alibi_bias_table()

    def attn_p(x, kv, jb):
        q = _matmul(x, b_w_q, jb, tm=PROMPT_TM, tn=GROUP * HEAD_DIM, out_dtype=BF16, scale=q_scale,
                    head_major=True, name=f"p_q{jb}")
        return _attn_prompt(q, kv, b_sinks, alibi_bias, jb, name=f"p_attn{jb}")

    xp = x_prompt.reshape(PROMPT_ROWS, D_MODEL)
    zero_conv = jnp.zeros((CONV_HALO, D_MODEL), F32)
    zero_ffn = jnp.zeros((SUBLANES, D_FF), F32)
    yp, us_p, tails_p, kv_p = _trunk(xp, [zero_conv] * N_A_LAYERS, [zero_ffn] * DEPTH, params,
                                     tag="p", tm=PROMPT_TM, stride=1, tiles_per_seq=SEQ // PROMPT_TM,
                                     conv_tail_rows=CONV_HALO, attention=attn_p)
    y_prompt = yp.reshape(BATCH, SEQ, D_MODEL)
    tps = SEQ // PROMPT_TM
    conv_a_prompt = jnp.stack([u[tps - 1::tps, CONV_HALO - (CONV_A_WIDTH - 1):] for u in us_p])
    ffn_conv_prompt = jnp.stack([t[tps - 1::tps, SUBLANES - (FFN_CONV_WIDTH - 1):] for t in tails_p])
    kv_p3 = kv_p.reshape(BATCH, SEQ, 2 * KV_WIDTH)[:, SEQ - WINDOW:]
    k_win_prompt = kv_p3[..., :KV_WIDTH].reshape(BATCH, WINDOW, N_KV_HEADS, HEAD_DIM)
    v_win_prompt = kv_p3[..., KV_WIDTH:].reshape(BATCH, WINDOW, N_KV_HEADS, HEAD_DIM)

    def attn_s(x, kv, jb):
        q = _matmul(x, b_w_q, jb, tm=SAMPLE_ROWS, tn=512, out_dtype=F32, scale=q_scale, name=f"s_q{jb}")
        o = _attn_sample(_to_batch_major(q, DEC_SEQ), _to_batch_major(kv, DEC_SEQ),
                         cache_k_win, cache_v_win, b_sinks, jb, name=f"s_attn{jb}")
        return _to_time_major(o)

    xs = _to_time_major(x_sample)
    conv_prev_s = [_to_time_major(state_conv_a[l]) for l in range(N_A_LAYERS)]
    ffn_prev_s = [_to_time_major(state_ffn_conv[l]) for l in range(DEPTH)]
    ys, us_s, tails_s, kv_s = _trunk(xs, conv_prev_s, ffn_prev_s, params,
                                     tag="s", tm=SAMPLE_ROWS, stride=DEC_BATCH, tiles_per_seq=1,
                                     conv_tail_rows=SAMPLE_ROWS, attention=attn_s)
    y_sample = _to_batch_major(ys, DEC_SEQ)
    conv_a_sample = jnp.stack([
        _to_batch_major(jnp.concatenate([conv_prev_s[l][SAMPLE_ROWS:], us_s[l][0]], axis=0), CONV_A_WIDTH - 1)
        for l in range(N_A_LAYERS)])
    ffn_conv_sample = jnp.stack([_to_batch_major(t[0], FFN_CONV_WIDTH - 1) for t in tails_s])
    kv_s3 = _to_batch_major(kv_s, DEC_SEQ)
    k_new = kv_s3[..., :KV_WIDTH].reshape(DEC_BATCH, DEC_SEQ, N_KV_HEADS, HEAD_DIM)
    v_new = kv_s3[..., KV_WIDTH:].reshape(DEC_BATCH, DEC_SEQ, N_KV_HEADS, HEAD_DIM)
    k_win_sample = jnp.concatenate([cache_k_win[:, DEC_SEQ:], k_new], axis=1)
    v_win_sample = jnp.concatenate([cache_v_win[:, DEC_SEQ:], v_new], axis=1)

    return (y_prompt, y_sample, conv_a_prompt, ffn_conv_prompt, k_win_prompt, v_win_prompt,
            conv_a_sample, ffn_conv_sample, k_win_sample, v_win_sample)
```

```python
import functools

import jax
import jax.numpy as jnp
from jax import lax
from jax.experimental import pallas as pl
from jax.experimental.pallas import tpu as pltpu

D_MODEL = 2048
BATCH = 4
SEQ = 2048
DEPTH = 4
DEC_BATCH = 32
DEC_SEQ = 8
N_A_LAYERS = DEPTH // 2
CONV_A_WIDTH = 31
FFN_CONV_WIDTH = 3
D_FF = ((8 * D_MODEL // 3 + 255) // 256) * 256
HEAD_DIM = 64
N_HEADS = D_MODEL // HEAD_DIM
N_KV_HEADS = max(1, N_HEADS // 8)
GROUP = N_HEADS // N_KV_HEADS
KV_WIDTH = N_KV_HEADS * HEAD_DIM
WINDOW = 128
DN_ALPHA = (2.0 * DEPTH) ** 0.25
LN_EPS = 1e-5
NEG_BIG = -1e30
ALIBI_SLOPES = tuple(2.0 ** (-8.0 * (h + 1) / N_HEADS) for h in range(N_HEADS))

BF16 = jnp.bfloat16
F32 = jnp.float32

V7X_VMEM_BYTES = 64 * 1024 * 1024
SUBLANES = 8

PROMPT_ROWS = BATCH * SEQ
SAMPLE_ROWS = DEC_BATCH * DEC_SEQ
PROMPT_TM = 1024
FFN_TF = 256
OUT_TM = 512
OUT_TN = 512
MASK_PENALTY = 1e30
GLU_TN = 256
GLU_STEP_TN = 2 * GLU_TN
W_SPLIT = 4
WD_SPLIT = 2
CONV_HALO = 32


def _params(semantics, vmem_bytes):
    limit = min(int(vmem_bytes * 1.25) + (4 << 20), V7X_VMEM_BYTES - (6 << 20))
    return pltpu.CompilerParams(dimension_semantics=semantics, vmem_limit_bytes=limit)


def _gelu_exact(z):
    return 0.5 * z * (1.0 + lax.erf(z * (0.5 ** 0.5)))


def _single(shape, index_map):
    return pl.BlockSpec(shape, index_map, pipeline_mode=pl.Buffered(1))


def _layer_block(l, shape, index_map):
    return pl.BlockSpec((None,) + shape, lambda *g: (l,) + index_map(*g))


def _rows3(p):
    return p.reshape(p.shape[0], 1, p.shape[1])


def _k_split_specs(l, k, tn, col_map):
    kq = k // W_SPLIT

    def spec(q):
        return _layer_block(l, (kq, tn), lambda i, j: (q, col_map(j)))

    return [spec(q) for q in range(W_SPLIT)]


def _weight_operand(w, l, k, tn, col_map):
    if w.dtype == BF16:
        return [pl.BlockSpec((k, tn), lambda i, j: (0, col_map(j)))], [w]
    return _k_split_specs(l, k, tn, col_map), [w] * W_SPLIT


def _stage_bf16(w_refs, wb_ref):
    kq = w_refs[0].shape[0]
    for q, w_ref in enumerate(w_refs):
        wb_ref[q * kq:(q + 1) * kq, :] = w_ref[...].astype(BF16)


def _bf16_weight(w_refs, wb_ref, copy_ref=None):
    if len(w_refs) == 1:
        return w_refs[0][...]
    _stage_bf16(w_refs, wb_ref)
    if copy_ref is not None:
        copy_ref[...] = wb_ref[...]
    return wb_ref[...]


def _mm_kernel(x_ref, *refs, scale, head_major, n_w, save_w):
    w_refs, rest = refs[:n_w], list(refs[n_w:])
    o_ref = rest.pop(0)
    copy_ref = rest.pop(0) if save_w else None
    xb_ref = rest.pop(0)
    wb_ref = rest.pop(0) if n_w > 1 else None

    @pl.when(pl.program_id(1) == 0)
    def _():
        xb_ref[...] = x_ref[...].astype(BF16)

    acc = jnp.dot(xb_ref[...], _bf16_weight(w_refs, wb_ref, copy_ref), preferred_element_type=F32)
    if scale != 1.0:
        acc = acc * scale
    if head_major:
        for h in range(o_ref.shape[0]):
            o_ref[h] = acc[:, h * HEAD_DIM:(h + 1) * HEAD_DIM].astype(o_ref.dtype)
    else:
        o_ref[...] = acc.astype(o_ref.dtype)


def _matmul(x, w, l, *, tm, tn, out_dtype, scale=1.0, head_major=False, save_w=False, name):
    m, k = x.shape
    n = w.shape[-1]
    assert not save_w or m == tm
    w_specs, w_args = _weight_operand(w, l, k, tn, lambda j: j)
    vmem = 2 * tm * k * 4 + tm * k * 2 + 2 * k * tn * 4 + 3 * k * tn * 2 + 5 * tm * tn * 4
    if head_major:
        heads = tn // HEAD_DIM
        out_specs = [pl.BlockSpec((heads, tm, HEAD_DIM), lambda i, j: (j, i, 0))]
        out_shape = [jax.ShapeDtypeStruct((n // HEAD_DIM, m, HEAD_DIM), out_dtype)]
    else:
        out_specs = [pl.BlockSpec((tm, tn), lambda i, j: (i, j))]
        out_shape = [jax.ShapeDtypeStruct((m, n), out_dtype)]
    if save_w:
        out_specs.append(pl.BlockSpec((k, tn), lambda i, j: (0, j)))
        out_shape.append(jax.ShapeDtypeStruct((k, n), BF16))
    scratch = [pltpu.VMEM((tm, k), BF16)]
    if len(w_specs) > 1:
        scratch.append(pltpu.VMEM((k, tn), BF16))
    outs = pl.pallas_call(
        functools.partial(_mm_kernel, scale=scale, head_major=head_major, n_w=len(w_specs), save_w=save_w),
        grid=(m // tm, n // tn),
        in_specs=[pl.BlockSpec((tm, k), lambda i, j: (i, 0))] + w_specs,
        out_specs=out_specs,
        out_shape=out_shape,
        scratch_shapes=scratch,
        compiler_params=_params(("arbitrary", "arbitrary"), vmem),
        name=name,
    )(x, *w_args)
    return outs if save_w else outs[0]


def _dwconv_unit_stride(ext_ref, w_ref, bias, c_ref, tm, col, tc):
    sub = lax.broadcasted_iota(jnp.int32, (SUBLANES, tc), 0)
    first = CONV_HALO - (CONV_A_WIDTH - 1)
    n_blocks = tm // SUBLANES
    n_ext = CONV_HALO // SUBLANES + 1
    t_prev = None
    for blk in range(n_blocks + 1):
        base = blk * SUBLANES
        e = [ext_ref[base + a * SUBLANES:base + (a + 1) * SUBLANES, :] if base + (a + 1) * SUBLANES <= CONV_HALO + tm
             else None for a in range(n_ext)]
        t = [None] * SUBLANES
        for p in range(SUBLANES):
            if blk == n_blocks and p == 0:
                continue
            for a in range(n_ext):
                k = a * SUBLANES + p - first
                if 0 <= k < CONV_A_WIDTH:
                    term = w_ref[k * SUBLANES:(k + 1) * SUBLANES, col:col + tc] * e[a]
                    t[p] = term if t[p] is None else t[p] + term
        if blk > 0:
            acc = bias + t_prev[0]
            for p in range(1, SUBLANES):
                mixed = jnp.where(sub >= p, t_prev[p], t[p])
                acc = acc + pltpu.roll(mixed, SUBLANES - p, axis=0)
            c_ref[base - SUBLANES:base, col:col + tc] = acc
        t_prev = t


def _dwconv_aligned_stride(ext_ref, w_ref, bias, c_ref, tm, halo, stride, col, tc):
    first = halo - (CONV_A_WIDTH - 1) * stride
    for blk in range(tm // SUBLANES):
        acc = bias
        for k in range(CONV_A_WIDTH):
            off = first + k * stride + blk * SUBLANES
            acc = acc + w_ref[k * SUBLANES:(k + 1) * SUBLANES, col:col + tc] * ext_ref[off:off + SUBLANES, :]
        c_ref[blk * SUBLANES:(blk + 1) * SUBLANES, col:col + tc] = acc


def _glu_conv_kernel(x_ref, *refs, tm, halo, stride, tiles_per_seq):
    wa_refs, wg_refs = refs[:W_SPLIT], refs[W_SPLIT:2 * W_SPLIT]
    ba_ref, bg_ref, prev_ref, wdw_ref, bdw_ref, c_ref, tail_ref, xb_ref, wab_ref, wgb_ref, ext_ref = (
        refs[2 * W_SPLIT:2 * W_SPLIT + 11])
    i = pl.program_id(0)
    j = pl.program_id(1)
    tc = GLU_TN
    n_sub = c_ref.shape[1] // tc

    @pl.when(j == 0)
    def _():
        xb_ref[...] = x_ref[...].astype(BF16)

    if tiles_per_seq > 1:
        carry_ref = refs[2 * W_SPLIT + 11]

        @pl.when((i == 0) & (j == 0))
        def _():
            carry_ref[...] = jnp.zeros_like(carry_ref)

    _stage_bf16(wa_refs, wab_ref)
    _stage_bf16(wg_refs, wgb_ref)
    xb = xb_ref[...]
    for sub in range(n_sub):
        col = sub * tc
        a = jnp.dot(xb, wab_ref[:, col:col + tc], preferred_element_type=F32) + ba_ref[:, col:col + tc]
        g = jnp.dot(xb, wgb_ref[:, col:col + tc], preferred_element_type=F32) + bg_ref[:, col:col + tc]
        u = a * jax.nn.sigmoid(g)
        ext = ext_ref.at[sub]
        if tiles_per_seq > 1:
            ext[0:halo, :] = jnp.where(i % tiles_per_seq == 0, prev_ref[:, col:col + tc],
                                       carry_ref[j, :, col:col + tc])
            carry_ref[j, :, col:col + tc] = u[tm - halo:, :]
        else:
            ext[0:halo, :] = prev_ref[:, col:col + tc]
        ext[halo:halo + tm, :] = u
        tail_ref[:, col:col + tc] = u[tm - tail_ref.shape[0]:, :]

    for sub in range(n_sub):
        col = sub * tc
        ext = ext_ref.at[sub]
        bias = jnp.broadcast_to(bdw_ref[:, col:col + tc], (SUBLANES, tc))
        if stride == 1:
            _dwconv_unit_stride(ext, wdw_ref, bias, c_ref, tm, col, tc)
        else:
            _dwconv_aligned_stride(ext, wdw_ref, bias, c_ref, tm, halo, stride, col, tc)


def _glu_conv(x, w_in, b_in, prev, w_dw, b_dw, l, *, tm, tn, stride, tiles_per_seq, tail_rows, name):
    m, k = x.shape
    n = w_in.shape[2] // 2
    nj = n // tn
    ni = m // tm
    halo = prev.shape[0]
    b3 = _rows3(b_in)
    w_rep = jnp.repeat(w_dw, SUBLANES, axis=1)
    kern = functools.partial(_glu_conv_kernel, tm=tm, halo=halo, stride=stride, tiles_per_seq=tiles_per_seq)
    scratch = [pltpu.VMEM((tm, k), BF16), pltpu.VMEM((k, tn), BF16), pltpu.VMEM((k, tn), BF16),
               pltpu.VMEM((tn // GLU_TN, halo + tm, GLU_TN), F32)]
    if tiles_per_seq > 1:
        scratch.append(pltpu.VMEM((nj, halo, tn), F32))
    vmem = (2 * tm * k * 4 + tm * k * 2 + 4 * k * tn * 4 + 2 * k * tn * 2 + 6 * tm * tn * 4
            + 3 * halo * tn * 4 + (halo + tm) * tn * 4 + nj * halo * tn * 4)
    return pl.pallas_call(
        kern,
        grid=(ni, nj),
        in_specs=([pl.BlockSpec((tm, k), lambda i, j: (i, 0))]
                  + _k_split_specs(l, k, tn, lambda j: j)
                  + _k_split_specs(l, k, tn, lambda j: j + nj)
                  + [_layer_block(l, (1, tn), lambda i, j: (0, j)),
                     _layer_block(l, (1, tn), lambda i, j: (0, j + nj)),
                     pl.BlockSpec((halo, tn), lambda i, j: (0, j)),
                     _layer_block(l, (CONV_A_WIDTH * SUBLANES, tn), lambda i, j: (0, j)),
                     _layer_block(l, (1, tn), lambda i, j: (0, j))]),
        out_specs=[pl.BlockSpec((tm, tn), lambda i, j: (i, j)),
                   pl.BlockSpec((None, tail_rows, tn), lambda i, j: (i, 0, j))],
        out_shape=[jax.ShapeDtypeStruct((m, n), F32),
                   jax.ShapeDtypeStruct((ni, tail_rows, n), F32)],
        scratch_shapes=scratch,
        compiler_params=_params(("arbitrary", "arbitrary"), vmem),
        name=name,
    )(x, *([w_in] * (2 * W_SPLIT)), b3, b3, prev, w_rep, _rows3(b_dw))


def _layer_norm(z, g, b):
    mu = jnp.mean(z, axis=-1, keepdims=True)
    zc = z - mu
    var = jnp.mean(zc * zc, axis=-1, keepdims=True)
    return zc * lax.rsqrt(var + LN_EPS) * g + b


def _residual_ln_chunks(x_ref, acc_ref, g_ref, b_ref, o_ref, tn):
    n_c = acc_ref.shape[0]
    d = n_c * tn
    tot = None
    for jj in range(n_c):
        z = DN_ALPHA * x_ref[:, jj * tn:(jj + 1) * tn] + acc_ref[jj]
        acc_ref[jj] = z
        part = jnp.sum(z, axis=-1, keepdims=True)
        tot = part if tot is None else tot + part
    mu = tot * (1.0 / d)
    sq = None
    for jj in range(n_c):
        zc = acc_ref[jj] - mu
        part = jnp.sum(zc * zc, axis=-1, keepdims=True)
        sq = part if sq is None else sq + part
    rs = lax.rsqrt(sq * (1.0 / d) + LN_EPS)
    for jj in range(n_c):
        o_ref[:, jj * tn:(jj + 1) * tn] = ((acc_ref[jj] - mu) * rs * g_ref[:, jj * tn:(jj + 1) * tn]
                                           + b_ref[:, jj * tn:(jj + 1) * tn])


def _mm_res_ln_kernel(a_ref, *refs, swish_norm, has_bias, tn, n_w, save_w):
    w_refs, refs = refs[:n_w], list(refs[n_w:])
    gn_ref, bn_ref = (refs.pop(0), refs.pop(0)) if swish_norm else (None, None)
    bias_ref = refs.pop(0) if has_bias else None
    x_ref, g_ref, b_ref, o_ref = refs[:4]
    refs = refs[4:]
    copy_ref = refs.pop(0) if save_w else None
    acc_ref = refs.pop(0)
    wb_ref = refs.pop(0) if n_w > 1 else None
    j = pl.program_id(1)

    if swish_norm:
        ab_ref = refs.pop(0)

        @pl.when(j == 0)
        def _():
            y = _layer_norm(a_ref[...], gn_ref[...], bn_ref[...])
            ab_ref[...] = (y * jax.nn.sigmoid(y)).astype(BF16)

        lhs = ab_ref[...]
    else:
        lhs = a_ref[...].astype(BF16)

    part = jnp.dot(lhs, _bf16_weight(w_refs, wb_ref, copy_ref), preferred_element_type=F32)
    if has_bias:
        part = part + bias_ref[...]
    acc_ref[j] = part

    @pl.when(j == pl.num_programs(1) - 1)
    def _():
        _residual_ln_chunks(x_ref, acc_ref, g_ref, b_ref, o_ref, tn)


def _matmul_res_ln(a, norm, w, bias, x, g, b, l, jb, *, tm, tn, save_w=False, name):
    m, kdim = a.shape
    n = w.shape[-1]
    nj = n // tn
    assert not save_w or m == tm
    has_bias = bias is not None
    swish_norm = norm is not None
    w_specs, w_args = _weight_operand(w, jb, kdim, tn, lambda j: j)
    in_specs = [pl.BlockSpec((tm, kdim), lambda i, j: (i, 0))] + w_specs
    args = [a] + w_args
    scratch = [pltpu.VMEM((nj, tm, tn), F32)]
    if len(w_specs) > 1:
        scratch.append(pltpu.VMEM((kdim, tn), BF16))
    if swish_norm:
        in_specs += [_layer_block(jb, (1, kdim), lambda i, j: (0, 0)),
                     _layer_block(jb, (1, kdim), lambda i, j: (0, 0))]
        args += [_rows3(norm[0]), _rows3(norm[1])]
        scratch.append(pltpu.VMEM((tm, kdim), BF16))
    if has_bias:
        in_specs.append(_layer_block(jb, (1, tn), lambda i, j: (0, j)))
        args.append(_rows3(bias))
    in_specs += [pl.BlockSpec((tm, n), lambda i, j: (i, 0)),
                 _layer_block(l, (1, n), lambda i, j: (0, 0)),
                 _layer_block(l, (1, n), lambda i, j: (0, 0))]
    args += [x, _rows3(g), _rows3(b)]
    out_specs = [pl.BlockSpec((tm, n), lambda i, j: (i, 0))]
    out_shape = [jax.ShapeDtypeStruct((m, n), F32)]
    if save_w:
        out_specs.append(pl.BlockSpec((kdim, tn), lambda i, j: (0, j)))
        out_shape.append(jax.ShapeDtypeStruct((kdim, n), BF16))
    vmem = (2 * tm * kdim * a.dtype.itemsize + 2 * kdim * tn * 4 + 3 * kdim * tn * 2 + 2 * tm * n * 4
            + 2 * tm * n * 4 + tm * n * 4 + 2 * tm * tn * 4 + (tm * kdim * 2 if swish_norm else 0))
    outs = pl.pallas_call(
        functools.partial(_mm_res_ln_kernel, swish_norm=swish_norm, has_bias=has_bias, tn=tn,
                          n_w=len(w_specs), save_w=save_w),
        grid=(m // tm, nj),
        in_specs=in_specs,
        out_specs=out_specs,
        out_shape=out_shape,
        scratch_shapes=scratch,
        compiler_params=_params(("arbitrary", "arbitrary"), vmem),
        name=name,
    )(*args)
    return outs if save_w else outs[0]


def _ffn_kernel(x_ref, gprev_ref, *refs, tm, halo, stride, tiles_per_seq):
    wg_refs, wv_refs = refs[:W_SPLIT], refs[W_SPLIT:2 * W_SPLIT]
    wd_refs = refs[2 * W_SPLIT:2 * W_SPLIT + WD_SPLIT]
    wdw_ref, bdw_ref, lg_ref, lb_ref, y_ref, tail_ref, xb_ref, wgb_ref, wvb_ref, wdb_ref, ext_ref = (
        refs[2 * W_SPLIT + WD_SPLIT:2 * W_SPLIT + WD_SPLIT + 11])
    i = pl.program_id(0)
    j = pl.program_id(1)

    @pl.when(j == 0)
    def _():
        xb_ref[...] = x_ref[...].astype(BF16)
        y_ref[...] = jnp.zeros_like(y_ref)

    if tiles_per_seq > 1:
        carry_ref = refs[2 * W_SPLIT + WD_SPLIT + 11]

        @pl.when((i == 0) & (j == 0))
        def _():
            carry_ref[...] = jnp.zeros_like(carry_ref)

    _stage_bf16(wg_refs, wgb_ref)
    _stage_bf16(wv_refs, wvb_ref)
    _stage_bf16(wd_refs, wdb_ref)
    xb = xb_ref[...]
    g = jnp.dot(xb, wgb_ref[...], preferred_element_type=F32)
    v = jnp.dot(xb, wvb_ref[...], preferred_element_type=F32)

    if tiles_per_seq > 1:
        prev = jnp.where(i % tiles_per_seq == 0, gprev_ref[...], carry_ref[j])
    else:
        prev = gprev_ref[...]
    ext_ref[0:halo, :] = prev
    ext_ref[halo:halo + tm, :] = g
    g_last = g[tm - halo:, :]
    tail_ref[...] = g_last
    if tiles_per_seq > 1:
        carry_ref[j] = g_last

    c = (wdw_ref[2:3, :] * g
         + wdw_ref[1:2, :] * ext_ref[halo - stride:halo - stride + tm, :]
         + wdw_ref[0:1, :] * ext_ref[halo - 2 * stride:halo - 2 * stride + tm, :]
         + bdw_ref[...])
    h = (_gelu_exact(c) * v).astype(BF16)
    y_ref[...] += jnp.dot(h, wdb_ref[...], preferred_element_type=F32)

    @pl.when(j == pl.num_programs(1) - 1)
    def _():
        y_ref[...] = _layer_norm(DN_ALPHA * x_ref[...] + y_ref[...], lg_ref[...], lb_ref[...])


def _conv_ffn(x, g_prev, w_up, w_dw, b_dw, w_down, ln_g, ln_b, l, *, tm, tf, stride, tiles_per_seq, name):
    m, d = x.shape
    f = w_down.shape[1]
    halo = g_prev.shape[0]
    nj = f // tf
    ni = m // tm
    tfq = tf // WD_SPLIT
    kern = functools.partial(_ffn_kernel, tm=tm, halo=halo, stride=stride, tiles_per_seq=tiles_per_seq)
    scratch = [pltpu.VMEM((tm, d), BF16), pltpu.VMEM((d, tf), BF16), pltpu.VMEM((d, tf), BF16),
               pltpu.VMEM((tf, d), BF16), pltpu.VMEM((halo + tm, tf), F32)]
    if tiles_per_seq > 1:
        scratch.append(pltpu.VMEM((nj, halo, tf), F32))
    vmem = (tm * d * 4 + 2 * tm * d * 4 + tm * d * 2 + 4 * d * tf * 4 + 2 * tf * d * 4
            + 3 * d * tf * 2 + (halo + tm) * tf * 4 + 8 * tm * tf * 4 + nj * halo * tf * 4)

    def wd_spec(q):
        return _layer_block(l, (tfq, d), lambda i, j: (j * WD_SPLIT + q, 0))

    return pl.pallas_call(
        kern,
        grid=(ni, nj),
        in_specs=([_single((tm, d), lambda i, j: (i, 0)),
                   pl.BlockSpec((halo, tf), lambda i, j: (0, j))]
                  + _k_split_specs(l, d, tf, lambda j: j)
                  + _k_split_specs(l, d, tf, lambda j: j + nj)
                  + [wd_spec(q) for q in range(WD_SPLIT)]
                  + [_layer_block(l, (FFN_CONV_WIDTH, tf), lambda i, j: (0, j)),
                     _layer_block(l, (1, tf), lambda i, j: (0, j)),
                     _layer_block(l, (1, d), lambda i, j: (0, 0)),
                     _layer_block(l, (1, d), lambda i, j: (0, 0))]),
        out_specs=[pl.BlockSpec((tm, d), lambda i, j: (i, 0)),
                   pl.BlockSpec((None, halo, tf), lambda i, j: (i, 0, j))],
        out_shape=[jax.ShapeDtypeStruct((m, d), F32),
                   jax.ShapeDtypeStruct((ni, halo, f), F32)],
        scratch_shapes=scratch,
        compiler_params=_params(("arbitrary", "arbitrary"), vmem),
        name=name,
    )(x, g_prev, *([w_up] * (2 * W_SPLIT)), *([w_down] * WD_SPLIT), w_dw, _rows3(b_dw),
      _rows3(ln_g), _rows3(ln_b))


def _softmax_pv(s, sink, v):
    m = jnp.maximum(jnp.max(s, axis=-1, keepdims=True), sink)
    p = jnp.exp(s - m)
    denom = jnp.sum(p, axis=-1, keepdims=True) + jnp.exp(sink - m)
    o = jnp.dot(p.astype(BF16), v, preferred_element_type=F32)
    return o / denom


def _alibi_bias_table():
    qi = jnp.arange(WINDOW)[:, None]
    kj = jnp.arange(2 * WINDOW)[None, :]
    dist = WINDOW + qi - kj
    band = (dist >= 0) & (dist <= WINDOW)
    slopes = 2.0 ** (-8.0 * jnp.arange(1, N_HEADS + 1, dtype=F32) / N_HEADS)
    pen = slopes[:, None, None] * dist.astype(F32)[None]
    first = band & (kj >= WINDOW)
    table = jnp.stack([jnp.where(first[None], pen, MASK_PENALTY), jnp.where(band[None], pen, MASK_PENALTY)])
    return table.reshape(2, N_HEADS * WINDOW, 2 * WINDOW)


def _attn_prompt_kernel(sinks_ref, q_ref, kvp_ref, kvc_ref, bias_ref, o_ref, *, jb):
    kv_all = jnp.concatenate([kvp_ref[...], kvc_ref[...]], axis=0).astype(BF16)
    for pair in range(N_HEADS // 2):
        outs = []
        for h in (2 * pair, 2 * pair + 1):
            kvh = h // GROUP
            kh = kv_all[:, kvh * HEAD_DIM:(kvh + 1) * HEAD_DIM]
            vh = kv_all[:, KV_WIDTH + kvh * HEAD_DIM:KV_WIDTH + (kvh + 1) * HEAD_DIM]
            s = lax.dot_general(q_ref[h], kh, (((1,), (1,)), ((), ())), preferred_element_type=F32)
            s = s - bias_ref[h * WINDOW:(h + 1) * WINDOW, :]
            outs.append(_softmax_pv(s, sinks_ref[jb, h], vh))
        o_ref[:, pair * 2 * HEAD_DIM:(pair + 1) * 2 * HEAD_DIM] = (
            jnp.concatenate(outs, axis=1).astype(o_ref.dtype))


def _attn_prompt(q, kv, sinks, bias, jb, *, name):
    m = q.shape[1]
    nb = SEQ // WINDOW
    return pl.pallas_call(
        functools.partial(_attn_prompt_kernel, jb=jb),
        grid=(BATCH, nb),
        in_specs=[pl.BlockSpec(memory_space=pltpu.SMEM),
                  pl.BlockSpec((N_HEADS, WINDOW, HEAD_DIM), lambda b, n: (0, b * nb + n, 0)),
                  pl.BlockSpec((WINDOW, 2 * KV_WIDTH), lambda b, n: (b * nb + jnp.maximum(n - 1, 0), 0)),
                  pl.BlockSpec((WINDOW, 2 * KV_WIDTH), lambda b, n: (b * nb + n, 0)),
                  pl.BlockSpec((None, N_HEADS * WINDOW, 2 * WINDOW), lambda b, n: (jnp.minimum(n, 1), 0, 0))],
        out_specs=pl.BlockSpec((WINDOW, D_MODEL), lambda b, n: (b * nb + n, 0)),
        out_shape=jax.ShapeDtypeStruct((m, D_MODEL), BF16),
        compiler_params=_params(("arbitrary", "arbitrary"), 32 << 20),
        name=name,
    )(sinks, q, kv, kv, bias)


def _attn_sample_kernel(sinks_ref, q_ref, kvn_ref, ck_ref, cv_ref, o_ref, *, jb):
    rows = GROUP * DEC_SEQ
    n_keys = WINDOW + DEC_SEQ
    row = lax.broadcasted_iota(jnp.int32, (rows, n_keys), 0)
    kj = lax.broadcasted_iota(jnp.int32, (rows, n_keys), 1)
    head_local = lax.shift_right_logical(row, DEC_SEQ.bit_length() - 1)
    dist = WINDOW + (row & (DEC_SEQ - 1)) - kj
    distf = dist.astype(F32)
    mask = (dist >= 0) & (dist <= WINDOW)
    head_col = lax.shift_right_logical(lax.broadcasted_iota(jnp.int32, (rows, 1), 0), DEC_SEQ.bit_length() - 1)
    kvn = kvn_ref[...]
    pieces = [None] * N_HEADS
    for kvh in range(N_KV_HEADS):
        lo, hi = kvh * HEAD_DIM, (kvh + 1) * HEAD_DIM
        q_st = jnp.concatenate(
            [q_ref[:, (kvh * GROUP + g) * HEAD_DIM:(kvh * GROUP + g + 1) * HEAD_DIM] for g in range(GROUP)],
            axis=0).astype(BF16)
        k_ext = jnp.concatenate([ck_ref[:, lo:hi], kvn[:, lo:hi]], axis=0).astype(BF16)
        v_ext = jnp.concatenate([cv_ref[:, lo:hi], kvn[:, KV_WIDTH + lo:KV_WIDTH + hi]], axis=0).astype(BF16)
        slope = jnp.zeros((rows, n_keys), F32)
        sink = jnp.zeros((rows, 1), F32)
        for g in range(GROUP):
            h = kvh * GROUP + g
            slope = jnp.where(head_local == g, ALIBI_SLOPES[h], slope)
            sink = jnp.where(head_col == g, sinks_ref[jb, h], sink)
        s = lax.dot_general(q_st, k_ext, (((1,), (1,)), ((), ())), preferred_element_type=F32)
        s = jnp.where(mask, s - slope * distf, NEG_BIG)
        o = _softmax_pv(s, sink, v_ext)
        for g in range(GROUP):
            pieces[kvh * GROUP + g] = o[g * DEC_SEQ:(g + 1) * DEC_SEQ, :]
    o_ref[...] = jnp.concatenate(pieces, axis=1)


def _attn_sample(q, kv_new, cache_k, cache_v, sinks, jb, *, name):
    ck = cache_k.reshape(DEC_BATCH, WINDOW, KV_WIDTH)
    cv = cache_v.reshape(DEC_BATCH, WINDOW, KV_WIDTH)
    return pl.pallas_call(
        functools.partial(_attn_sample_kernel, jb=jb),
        grid=(DEC_BATCH,),
        in_specs=[pl.BlockSpec(memory_space=pltpu.SMEM),
                  pl.BlockSpec((None, DEC_SEQ, D_MODEL), lambda b: (b, 0, 0)),
                  pl.BlockSpec((None, DEC_SEQ, 2 * KV_WIDTH), lambda b: (b, 0, 0)),
                  pl.BlockSpec((None, WINDOW, KV_WIDTH), lambda b: (b, 0, 0)),
                  pl.BlockSpec((None, WINDOW, KV_WIDTH), lambda b: (b, 0, 0))],
        out_specs=pl.BlockSpec((None, DEC_SEQ, D_MODEL), lambda b: (b, 0, 0)),
        out_shape=jax.ShapeDtypeStruct((DEC_BATCH, DEC_SEQ, D_MODEL), F32),
        compiler_params=_params(("arbitrary",), 8 << 20),
        name=name,
    )(sinks, q, kv_new, ck, cv)


def _to_time_major(a):
    return jnp.swapaxes(a, 0, 1).reshape(a.shape[1] * a.shape[0], a.shape[2])


def _to_batch_major(a2d, t):
    return jnp.swapaxes(a2d.reshape(t, DEC_BATCH, a2d.shape[1]), 0, 1)


def _trunk(x, conv_prev, ffn_prev, params, *, tag, tm, stride, tiles_per_seq, conv_tail_rows, attention,
           bf16_weights, save_w):
    (a_w_in, a_b_in, a_w_dw, a_b_dw, a_norm_g, a_norm_b, a_w_out, a_b_out, w_kv, b_w_q, b_sinks, b_w_o,
     f_w_up, f_w_dw, f_b_dw, f_w_down, ln_mix_g, ln_mix_b, ln_ffn_g, ln_ffn_b) = params
    u_tails, tails = [], []
    kv = None
    for l in range(DEPTH):
        if l < N_A_LAYERS:
            c, u_tail = _glu_conv(x, a_w_in, a_b_in, conv_prev[l], a_w_dw, a_b_dw, l, tm=tm, tn=GLU_STEP_TN,
                                  stride=stride, tiles_per_seq=tiles_per_seq, tail_rows=conv_tail_rows,
                                  name=f"{tag}_gluconv{l}")
            u_tails.append(u_tail)
            key, w_out, bias, norm, jb = ("aout", l), a_w_out, a_b_out, (a_norm_g, a_norm_b), l
            lhs = c
        else:
            jb = l - N_A_LAYERS
            lhs = attention(x, kv, jb)
            key, w_out, bias, norm = ("bout", jb), b_w_o, None, None
        res = _matmul_res_ln(lhs, norm, w_out if save_w else bf16_weights[key], bias, x, ln_mix_g, ln_mix_b,
                             l, jb, tm=min(OUT_TM, tm), tn=OUT_TN, save_w=save_w, name=f"{tag}_{key[0]}{l}")
        if save_w:
            x, bf16_weights[key] = res
        else:
            x = res
        x, tail = _conv_ffn(x, ffn_prev[l], f_w_up, f_w_dw, f_b_dw, f_w_down, ln_ffn_g, ln_ffn_b, l,
                            tm=tm, tf=FFN_TF, stride=stride, tiles_per_seq=tiles_per_seq,
                            name=f"{tag}_ffn{l}")
        tails.append(tail)
        if l == N_A_LAYERS - 1:
            kv = _matmul(x, w_kv[None], 0, tm=tm, tn=2 * KV_WIDTH, out_dtype=F32, name=f"{tag}_kv")
    return x, u_tails, tails, kv


def kernel(x_prompt, x_sample, state_conv_a, state_ffn_conv, cache_k_win, cache_v_win, a_w_in, a_b_in, a_w_dw, a_b_dw, a_norm_g, a_norm_b, a_w_out, a_b_out, w_kv, b_w_q, b_sinks, b_w_o, f_w_up, f_w_dw, f_b_dw, f_w_down, ln_mix_g, ln_mix_b, ln_ffn_g, ln_ffn_b):
    params = (a_w_in, a_b_in, a_w_dw, a_b_dw, a_norm_g, a_norm_b, a_w_out, a_b_out, w_kv, b_w_q, b_sinks, b_w_o,
              f_w_up, f_w_dw, f_b_dw, f_w_down, ln_mix_g, ln_mix_b, ln_ffn_g, ln_ffn_b)
    q_scale = HEAD_DIM ** -0.5
    alibi_bias = _alibi_bias_table()
    bf16_weights = {}

    def attn_s(x, kv, jb):
        q, bf16_weights["q", jb] = _matmul(x, b_w_q, jb, tm=SAMPLE_ROWS, tn=GROUP * HEAD_DIM, out_dtype=F32,
                                           scale=q_scale, save_w=True, name=f"s_q{jb}")
        o = _attn_sample(_to_batch_major(q, DEC_SEQ), _to_batch_major(kv, DEC_SEQ),
                         cache_k_win, cache_v_win, b_sinks, jb, name=f"s_attn{jb}")
        return _to_time_major(o)

    xs = _to_time_major(x_sample)
    conv_prev_s = [_to_time_major(state_conv_a[l]) for l in range(N_A_LAYERS)]
    ffn_prev_s = [_to_time_major(state_ffn_conv[l]) for l in range(DEPTH)]
    ys, us_s, tails_s, kv_s = _trunk(xs, conv_prev_s, ffn_prev_s, params,
                                     tag="s", tm=SAMPLE_ROWS, stride=DEC_BATCH, tiles_per_seq=1,
                                     conv_tail_rows=SAMPLE_ROWS, attention=attn_s,
                                     bf16_weights=bf16_weights, save_w=True)
    y_sample = _to_batch_major(ys, DEC_SEQ)
    conv_a_sample = jnp.stack([
        _to_batch_major(jnp.concatenate([conv_prev_s[l][SAMPLE_ROWS:], us_s[l][0]], axis=0), CONV_A_WIDTH - 1)
        for l in range(N_A_LAYERS)])
    ffn_conv_sample = jnp.stack([_to_batch_major(t[0], FFN_CONV_WIDTH - 1) for t in tails_s])
    kv_s3 = _to_batch_major(kv_s, DEC_SEQ)
    k_new = kv_s3[..., :KV_WIDTH].reshape(DEC_BATCH, DEC_SEQ, N_KV_HEADS, HEAD_DIM)
    v_new = kv_s3[..., KV_WIDTH:].reshape(DEC_BATCH, DEC_SEQ, N_KV_HEADS, HEAD_DIM)
    k_win_sample = jnp.concatenate([cache_k_win[:, DEC_SEQ:], k_new], axis=1)
    v_win_sample = jnp.concatenate([cache_v_win[:, DEC_SEQ:], v_new], axis=1)

    def attn_p(x, kv, jb):
        q = _matmul(x, bf16_weights["q", jb], jb, tm=PROMPT_TM, tn=GROUP * HEAD_DIM, out_dtype=BF16,
                    scale=q_scale, head_major=True, name=f"p_q{jb}")
        return _attn_prompt(q, kv, b_sinks, alibi_bias, jb, name=f"p_attn{jb}")

    xp = x_prompt.reshape(PROMPT_ROWS, D_MODEL)
    zero_conv = jnp.zeros((CONV_HALO, D_MODEL), F32)
    zero_ffn = jnp.zeros((SUBLANES, D_FF), F32)
    yp, us_p, tails_p, kv_p = _trunk(xp, [zero_conv] * N_A_LAYERS, [zero_ffn] * DEPTH, params,
                                     tag="p", tm=PROMPT_TM, stride=1, tiles_per_seq=SEQ // PROMPT_TM,
                                     conv_tail_rows=CONV_HALO, attention=attn_p,
                                     bf16_weights=bf16_weights, save_w=False)
    y_prompt = yp.reshape(BATCH, SEQ, D_MODEL)
    tps = SEQ // PROMPT_TM
    conv_a_prompt = jnp.stack([u[tps - 1::tps, CONV_HALO - (CONV_A_WIDTH - 1):] for u in us_p])
    ffn_conv_prompt = jnp.stack([t[tps - 1::tps, SUBLANES - (FFN_CONV_WIDTH - 1):] for t in tails_p])
    kv_p3 = kv_p.reshape(BATCH, SEQ, 2 * KV_WIDTH)[:, SEQ - WINDOW:]
    k_win_prompt = kv_p3[..., :KV_WIDTH].reshape(BATCH, WINDOW, N_KV_HEADS, HEAD_DIM)
    v_win_prompt = kv_p3[..., KV_WIDTH:].reshape(BATCH, WINDOW, N_KV_HEADS, HEAD_DIM)

    return (y_prompt, y_sample, conv_a_prompt, ffn_conv_prompt, k_win_prompt, v_win_prompt,
            conv_a_sample, ffn_conv_sample, k_win_sample, v_win_sample)
```

```python
import functools

import jax
import jax.numpy as jnp
from jax import lax
from jax.experimental import pallas as pl
from jax.experimental.pallas import tpu as pltpu

D_MODEL = 2048
BATCH = 4
SEQ = 2048
DEPTH = 4
DEC_BATCH = 32
DEC_SEQ = 8
N_A_LAYERS = DEPTH // 2
CONV_A_WIDTH = 31
FFN_CONV_WIDTH = 3
D_FF = ((8 * D_MODEL // 3 + 255) // 256) * 256
HEAD_DIM = 64
N_HEADS = D_MODEL // HEAD_DIM
N_KV_HEADS = max(1, N_HEADS // 8)
GROUP = N_HEADS // N_KV_HEADS
KV_WIDTH = N_KV_HEADS * HEAD_DIM
WINDOW = 128
DN_ALPHA = (2.0 * DEPTH) ** 0.25
LN_EPS = 1e-5
NEG_BIG = -1e30
ALIBI_SLOPES = tuple(2.0 ** (-8.0 * (h + 1) / N_HEADS) for h in range(N_HEADS))

BF16 = jnp.bfloat16
F32 = jnp.float32

V7X_VMEM_BYTES = 64 * 1024 * 1024
SUBLANES = 8

PROMPT_ROWS = BATCH * SEQ
SAMPLE_ROWS = DEC_BATCH * DEC_SEQ
PROMPT_TM = 1024
FFN_TF_F32 = 256
FFN_TF_BF16 = 512
OUT_TM = 512
OUT_TN = 512
MASK_PENALTY = 1e30
GLU_TN = 256
GLU_STEP_TN = 2 * GLU_TN
W_SPLIT = 4
WD_SPLIT = 2
CONV_HALO = 32


def _params(semantics, vmem_bytes):
    limit = min(int(vmem_bytes * 1.25) + (4 << 20), V7X_VMEM_BYTES - (6 << 20))
    return pltpu.CompilerParams(dimension_semantics=semantics, vmem_limit_bytes=limit)


def _gelu_exact(z):
    return 0.5 * z * (1.0 + lax.erf(z * (0.5 ** 0.5)))


def _single(shape, index_map):
    return pl.BlockSpec(shape, index_map, pipeline_mode=pl.Buffered(1))


def _layer_block(l, shape, index_map):
    return pl.BlockSpec((None,) + shape, lambda *g: (l,) + index_map(*g))


def _rows3(p):
    return p.reshape(p.shape[0], 1, p.shape[1])


def _k_split_specs(l, k, tn, col_map):
    kq = k // W_SPLIT

    def spec(q):
        return _layer_block(l, (kq, tn), lambda i, j: (q, col_map(j)))

    return [spec(q) for q in range(W_SPLIT)]


def _weight_operand(w, l, k, tn, col_map):
    if w.dtype == BF16:
        return [pl.BlockSpec((k, tn), lambda i, j: (0, col_map(j)))], [w]
    return _k_split_specs(l, k, tn, col_map), [w] * W_SPLIT


def _stage_bf16(w_refs, wb_ref):
    kq = w_refs[0].shape[0]
    for q, w_ref in enumerate(w_refs):
        wb_ref[q * kq:(q + 1) * kq, :] = w_ref[...].astype(BF16)


def _bf16_weight(w_refs, wb_ref, copy_ref=None):
    if len(w_refs) == 1:
        return w_refs[0][...]
    _stage_bf16(w_refs, wb_ref)
    if copy_ref is not None:
        copy_ref[...] = wb_ref[...]
    return wb_ref[...]


def _mm_kernel(x_ref, *refs, scale, head_major, n_w, save_w):
    w_refs, rest = refs[:n_w], list(refs[n_w:])
    o_ref = rest.pop(0)
    copy_ref = rest.pop(0) if save_w else None
    xb_ref = rest.pop(0)
    wb_ref = rest.pop(0) if n_w > 1 else None

    @pl.when(pl.program_id(1) == 0)
    def _():
        xb_ref[...] = x_ref[...].astype(BF16)

    acc = jnp.dot(xb_ref[...], _bf16_weight(w_refs, wb_ref, copy_ref), preferred_element_type=F32)
    if scale != 1.0:
        acc = acc * scale
    if head_major:
        for h in range(o_ref.shape[0]):
            o_ref[h] = acc[:, h * HEAD_DIM:(h + 1) * HEAD_DIM].astype(o_ref.dtype)
    else:
        o_ref[...] = acc.astype(o_ref.dtype)


def _matmul(x, w, l, *, tm, tn, out_dtype, scale=1.0, head_major=False, save_w=False, name):
    m, k = x.shape
    n = w.shape[-1]
    assert not save_w or m == tm
    w_specs, w_args = _weight_operand(w, l, k, tn, lambda j: j)
    vmem = 2 * tm * k * 4 + tm * k * 2 + 2 * k * tn * 4 + 3 * k * tn * 2 + 5 * tm * tn * 4
    if head_major:
        heads = tn // HEAD_DIM
        out_specs = [pl.BlockSpec((heads, tm, HEAD_DIM), lambda i, j: (j, i, 0))]
        out_shape = [jax.ShapeDtypeStruct((n // HEAD_DIM, m, HEAD_DIM), out_dtype)]
    else:
        out_specs = [pl.BlockSpec((tm, tn), lambda i, j: (i, j))]
        out_shape = [jax.ShapeDtypeStruct((m, n), out_dtype)]
    if save_w:
        out_specs.append(pl.BlockSpec((k, tn), lambda i, j: (0, j)))
        out_shape.append(jax.ShapeDtypeStruct((k, n), BF16))
    scratch = [pltpu.VMEM((tm, k), BF16)]
    if len(w_specs) > 1:
        scratch.append(pltpu.VMEM((k, tn), BF16))
    outs = pl.pallas_call(
        functools.partial(_mm_kernel, scale=scale, head_major=head_major, n_w=len(w_specs), save_w=save_w),
        grid=(m // tm, n // tn),
        in_specs=[pl.BlockSpec((tm, k), lambda i, j: (i, 0))] + w_specs,
        out_specs=out_specs,
        out_shape=out_shape,
        scratch_shapes=scratch,
        compiler_params=_params(("arbitrary", "arbitrary"), vmem),
        name=name,
    )(x, *w_args)
    return outs if save_w else outs[0]


def _dwconv_unit_stride(ext_ref, w_ref, bias, c_ref, tm, col, tc):
    sub = lax.broadcasted_iota(jnp.int32, (SUBLANES, tc), 0)
    first = CONV_HALO - (CONV_A_WIDTH - 1)
    n_blocks = tm // SUBLANES
    n_ext = CONV_HALO // SUBLANES + 1
    t_prev = None
    for blk in range(n_blocks + 1):
        base = blk * SUBLANES
        e = [ext_ref[base + a * SUBLANES:base + (a + 1) * SUBLANES, :] if base + (a + 1) * SUBLANES <= CONV_HALO + tm
             else None for a in range(n_ext)]
        t = [None] * SUBLANES
        for p in range(SUBLANES):
            if blk == n_blocks and p == 0:
                continue
            for a in range(n_ext):
                k = a * SUBLANES + p - first
                if 0 <= k < CONV_A_WIDTH:
                    term = w_ref[k * SUBLANES:(k + 1) * SUBLANES, col:col + tc] * e[a]
                    t[p] = term if t[p] is None else t[p] + term
        if blk > 0:
            acc = bias + t_prev[0]
            for p in range(1, SUBLANES):
                mixed = jnp.where(sub >= p, t_prev[p], t[p])
                acc = acc + pltpu.roll(mixed, SUBLANES - p, axis=0)
            c_ref[base - SUBLANES:base, col:col + tc] = acc
        t_prev = t


def _dwconv_aligned_stride(ext_ref, w_ref, bias, c_ref, tm, halo, stride, col, tc):
    first = halo - (CONV_A_WIDTH - 1) * stride
    for blk in range(tm // SUBLANES):
        acc = bias
        for k in range(CONV_A_WIDTH):
            off = first + k * stride + blk * SUBLANES
            acc = acc + w_ref[k * SUBLANES:(k + 1) * SUBLANES, col:col + tc] * ext_ref[off:off + SUBLANES, :]
        c_ref[blk * SUBLANES:(blk + 1) * SUBLANES, col:col + tc] = acc


def _glu_conv_kernel(x_ref, *refs, tm, halo, stride, tiles_per_seq):
    wa_refs, wg_refs = refs[:W_SPLIT], refs[W_SPLIT:2 * W_SPLIT]
    ba_ref, bg_ref, prev_ref, wdw_ref, bdw_ref, c_ref, tail_ref, xb_ref, wab_ref, wgb_ref, ext_ref = (
        refs[2 * W_SPLIT:2 * W_SPLIT + 11])
    i = pl.program_id(0)
    j = pl.program_id(1)
    tc = GLU_TN
    n_sub = c_ref.shape[1] // tc

    @pl.when(j == 0)
    def _():
        xb_ref[...] = x_ref[...].astype(BF16)

    if tiles_per_seq > 1:
        carry_ref = refs[2 * W_SPLIT + 11]

        @pl.when((i == 0) & (j == 0))
        def _():
            carry_ref[...] = jnp.zeros_like(carry_ref)

    _stage_bf16(wa_refs, wab_ref)
    _stage_bf16(wg_refs, wgb_ref)
    xb = xb_ref[...]
    for sub in range(n_sub):
        col = sub * tc
        a = jnp.dot(xb, wab_ref[:, col:col + tc], preferred_element_type=F32) + ba_ref[:, col:col + tc]
        g = jnp.dot(xb, wgb_ref[:, col:col + tc], preferred_element_type=F32) + bg_ref[:, col:col + tc]
        u = a * jax.nn.sigmoid(g)
        ext = ext_ref.at[sub]
        if tiles_per_seq > 1:
            ext[0:halo, :] = jnp.where(i % tiles_per_seq == 0, prev_ref[:, col:col + tc],
                                       carry_ref[j, :, col:col + tc])
            carry_ref[j, :, col:col + tc] = u[tm - halo:, :]
        else:
            ext[0:halo, :] = prev_ref[:, col:col + tc]
        ext[halo:halo + tm, :] = u
        tail_ref[:, col:col + tc] = u[tm - tail_ref.shape[0]:, :]

    for sub in range(n_sub):
        col = sub * tc
        ext = ext_ref.at[sub]
        bias = jnp.broadcast_to(bdw_ref[:, col:col + tc], (SUBLANES, tc))
        if stride == 1:
            _dwconv_unit_stride(ext, wdw_ref, bias, c_ref, tm, col, tc)
        else:
            _dwconv_aligned_stride(ext, wdw_ref, bias, c_ref, tm, halo, stride, col, tc)


def _glu_conv(x, w_in, b_in, prev, w_dw, b_dw, l, *, tm, tn, stride, tiles_per_seq, tail_rows, name):
    m, k = x.shape
    n = w_in.shape[2] // 2
    nj = n // tn
    ni = m // tm
    halo = prev.shape[0]
    b3 = _rows3(b_in)
    w_rep = jnp.repeat(w_dw, SUBLANES, axis=1)
    kern = functools.partial(_glu_conv_kernel, tm=tm, halo=halo, stride=stride, tiles_per_seq=tiles_per_seq)
    scratch = [pltpu.VMEM((tm, k), BF16), pltpu.VMEM((k, tn), BF16), pltpu.VMEM((k, tn), BF16),
               pltpu.VMEM((tn // GLU_TN, halo + tm, GLU_TN), F32)]
    if tiles_per_seq > 1:
        scratch.append(pltpu.VMEM((nj, halo, tn), F32))
    vmem = (2 * tm * k * 4 + tm * k * 2 + 4 * k * tn * 4 + 2 * k * tn * 2 + 6 * tm * tn * 4
            + 3 * halo * tn * 4 + (halo + tm) * tn * 4 + nj * halo * tn * 4)
    return pl.pallas_call(
        kern,
        grid=(ni, nj),
        in_specs=([pl.BlockSpec((tm, k), lambda i, j: (i, 0))]
                  + _k_split_specs(l, k, tn, lambda j: j)
                  + _k_split_specs(l, k, tn, lambda j: j + nj)
                  + [_layer_block(l, (1, tn), lambda i, j: (0, j)),
                     _layer_block(l, (1, tn), lambda i, j: (0, j + nj)),
                     pl.BlockSpec((halo, tn), lambda i, j: (0, j)),
                     _layer_block(l, (CONV_A_WIDTH * SUBLANES, tn), lambda i, j: (0, j)),
                     _layer_block(l, (1, tn), lambda i, j: (0, j))]),
        out_specs=[pl.BlockSpec((tm, tn), lambda i, j: (i, j)),
                   pl.BlockSpec((None, tail_rows, tn), lambda i, j: (i, 0, j))],
        out_shape=[jax.ShapeDtypeStruct((m, n), F32),
                   jax.ShapeDtypeStruct((ni, tail_rows, n), F32)],
        scratch_shapes=scratch,
        compiler_params=_params(("arbitrary", "arbitrary"), vmem),
        name=name,
    )(x, *([w_in] * (2 * W_SPLIT)), b3, b3, prev, w_rep, _rows3(b_dw))


def _layer_norm(z, g, b):
    mu = jnp.mean(z, axis=-1, keepdims=True)
    zc = z - mu
    var = jnp.mean(zc * zc, axis=-1, keepdims=True)
    return zc * lax.rsqrt(var + LN_EPS) * g + b


def _residual_ln_chunks(x_ref, acc_ref, g_ref, b_ref, o_ref, tn):
    n_c = acc_ref.shape[0]
    d = n_c * tn
    tot = None
    for jj in range(n_c):
        z = DN_ALPHA * x_ref[:, jj * tn:(jj + 1) * tn] + acc_ref[jj]
        acc_ref[jj] = z
        part = jnp.sum(z, axis=-1, keepdims=True)
        tot = part if tot is None else tot + part
    mu = tot * (1.0 / d)
    sq = None
    for jj in range(n_c):
        zc = acc_ref[jj] - mu
        part = jnp.sum(zc * zc, axis=-1, keepdims=True)
        sq = part if sq is None else sq + part
    rs = lax.rsqrt(sq * (1.0 / d) + LN_EPS)
    for jj in range(n_c):
        o_ref[:, jj * tn:(jj + 1) * tn] = ((acc_ref[jj] - mu) * rs * g_ref[:, jj * tn:(jj + 1) * tn]
                                           + b_ref[:, jj * tn:(jj + 1) * tn])


def _mm_res_ln_kernel(a_ref, *refs, swish_norm, has_bias, tn, n_w, save_w):
    w_refs, refs = refs[:n_w], list(refs[n_w:])
    gn_ref, bn_ref = (refs.pop(0), refs.pop(0)) if swish_norm else (None, None)
    bias_ref = refs.pop(0) if has_bias else None
    x_ref, g_ref, b_ref, o_ref = refs[:4]
    refs = refs[4:]
    copy_ref = refs.pop(0) if save_w else None
    acc_ref = refs.pop(0)
    wb_ref = refs.pop(0) if n_w > 1 else None
    j = pl.program_id(1)

    if swish_norm:
        ab_ref = refs.pop(0)

        @pl.when(j == 0)
        def _():
            y = _layer_norm(a_ref[...], gn_ref[...], bn_ref[...])
            ab_ref[...] = (y * jax.nn.sigmoid(y)).astype(BF16)

        lhs = ab_ref[...]
    else:
        lhs = a_ref[...].astype(BF16)

    part = jnp.dot(lhs, _bf16_weight(w_refs, wb_ref, copy_ref), preferred_element_type=F32)
    if has_bias:
        part = part + bias_ref[...]
    acc_ref[j] = part

    @pl.when(j == pl.num_programs(1) - 1)
    def _():
        _residual_ln_chunks(x_ref, acc_ref, g_ref, b_ref, o_ref, tn)


def _matmul_res_ln(a, norm, w, bias, x, g, b, l, jb, *, tm, tn, save_w=False, name):
    m, kdim = a.shape
    n = w.shape[-1]
    nj = n // tn
    assert not save_w or m == tm
    has_bias = bias is not None
    swish_norm = norm is not None
    w_specs, w_args = _weight_operand(w, jb, kdim, tn, lambda j: j)
    in_specs = [pl.BlockSpec((tm, kdim), lambda i, j: (i, 0))] + w_specs
    args = [a] + w_args
    scratch = [pltpu.VMEM((nj, tm, tn), F32)]
    if len(w_specs) > 1:
        scratch.append(pltpu.VMEM((kdim, tn), BF16))
    if swish_norm:
        in_specs += [_layer_block(jb, (1, kdim), lambda i, j: (0, 0)),
                     _layer_block(jb, (1, kdim), lambda i, j: (0, 0))]
        args += [_rows3(norm[0]), _rows3(norm[1])]
        scratch.append(pltpu.VMEM((tm, kdim), BF16))
    if has_bias:
        in_specs.append(_layer_block(jb, (1, tn), lambda i, j: (0, j)))
        args.append(_rows3(bias))
    in_specs += [pl.BlockSpec((tm, n), lambda i, j: (i, 0)),
                 _layer_block(l, (1, n), lambda i, j: (0, 0)),
                 _layer_block(l, (1, n), lambda i, j: (0, 0))]
    args += [x, _rows3(g), _rows3(b)]
    out_specs = [pl.BlockSpec((tm, n), lambda i, j: (i, 0))]
    out_shape = [jax.ShapeDtypeStruct((m, n), F32)]
    if save_w:
        out_specs.append(pl.BlockSpec((kdim, tn), lambda i, j: (0, j)))
        out_shape.append(jax.ShapeDtypeStruct((kdim, n), BF16))
    vmem = (2 * tm * kdim * a.dtype.itemsize + 2 * kdim * tn * 4 + 3 * kdim * tn * 2 + 2 * tm * n * 4
            + 2 * tm * n * 4 + tm * n * 4 + 2 * tm * tn * 4 + (tm * kdim * 2 if swish_norm else 0))
    outs = pl.pallas_call(
        functools.partial(_mm_res_ln_kernel, swish_norm=swish_norm, has_bias=has_bias, tn=tn,
                          n_w=len(w_specs), save_w=save_w),
        grid=(m // tm, nj),
        in_specs=in_specs,
        out_specs=out_specs,
        out_shape=out_shape,
        scratch_shapes=scratch,
        compiler_params=_params(("arbitrary", "arbitrary"), vmem),
        name=name,
    )(*args)
    return outs if save_w else outs[0]


def _ffn_kernel(x_ref, gprev_ref, *refs, tm, halo, stride, tiles_per_seq, n_wu, n_wd, save_w):
    refs = list(refs)
    wg_refs = [refs.pop(0) for _ in range(n_wu)]
    wv_refs = [refs.pop(0) for _ in range(n_wu)]
    wd_refs = [refs.pop(0) for _ in range(n_wd)]
    wdw_ref, bdw_ref, lg_ref, lb_ref, y_ref, tail_ref = [refs.pop(0) for _ in range(6)]
    wg_copy, wv_copy, wd_copy = [refs.pop(0) for _ in range(3)] if save_w else (None, None, None)
    xb_ref = refs.pop(0)
    wgb_ref, wvb_ref = (refs.pop(0), refs.pop(0)) if n_wu > 1 else (None, None)
    wdb_ref = refs.pop(0) if n_wd > 1 else None
    ext_ref = refs.pop(0)
    i = pl.program_id(0)
    j = pl.program_id(1)

    @pl.when(j == 0)
    def _():
        xb_ref[...] = x_ref[...].astype(BF16)
        y_ref[...] = jnp.zeros_like(y_ref)

    if tiles_per_seq > 1:
        carry_ref = refs.pop(0)

        @pl.when((i == 0) & (j == 0))
        def _():
            carry_ref[...] = jnp.zeros_like(carry_ref)

    xb = xb_ref[...]
    g = jnp.dot(xb, _bf16_weight(wg_refs, wgb_ref, wg_copy), preferred_element_type=F32)
    v = jnp.dot(xb, _bf16_weight(wv_refs, wvb_ref, wv_copy), preferred_element_type=F32)

    if tiles_per_seq > 1:
        prev = jnp.where(i % tiles_per_seq == 0, gprev_ref[...], carry_ref[j])
    else:
        prev = gprev_ref[...]
    ext_ref[0:halo, :] = prev
    ext_ref[halo:halo + tm, :] = g
    g_last = g[tm - halo:, :]
    tail_ref[...] = g_last
    if tiles_per_seq > 1:
        carry_ref[j] = g_last

    c = (wdw_ref[2:3, :] * g
         + wdw_ref[1:2, :] * ext_ref[halo - stride:halo - stride + tm, :]
         + wdw_ref[0:1, :] * ext_ref[halo - 2 * stride:halo - 2 * stride + tm, :]
         + bdw_ref[...])
    h = (_gelu_exact(c) * v).astype(BF16)
    y_ref[...] += jnp.dot(h, _bf16_weight(wd_refs, wdb_ref, wd_copy), preferred_element_type=F32)

    @pl.when(j == pl.num_programs(1) - 1)
    def _():
        y_ref[...] = _layer_norm(DN_ALPHA * x_ref[...] + y_ref[...], lg_ref[...], lb_ref[...])


def _conv_ffn(x, g_prev, w_up, w_dw, b_dw, w_down, ln_g, ln_b, l, *, tm, tf, stride, tiles_per_seq,
              save_w=False, name):
    m, d = x.shape
    halo = g_prev.shape[0]
    ni = m // tm
    assert not save_w or ni == 1
    if isinstance(w_up, tuple):
        f = w_down.shape[0]
        nj = f // tf
        wg_specs, wg_args = _weight_operand(w_up[0], l, d, tf, lambda j: j)
        wv_specs, wv_args = _weight_operand(w_up[1], l, d, tf, lambda j: j)
        wd_specs, wd_args = [pl.BlockSpec((tf, d), lambda i, j: (j, 0))], [w_down]
    else:
        f = w_down.shape[1]
        nj = f // tf
        tfq = tf // WD_SPLIT
        wg_specs, wg_args = _weight_operand(w_up, l, d, tf, lambda j: j)
        wv_specs, wv_args = _weight_operand(w_up, l, d, tf, lambda j: j + nj)

        def wd_spec(q):
            return _layer_block(l, (tfq, d), lambda i, j: (j * WD_SPLIT + q, 0))

        wd_specs, wd_args = [wd_spec(q) for q in range(WD_SPLIT)], [w_down] * WD_SPLIT
    n_wu, n_wd = len(wg_specs), len(wd_specs)
    kern = functools.partial(_ffn_kernel, tm=tm, halo=halo, stride=stride, tiles_per_seq=tiles_per_seq,
                             n_wu=n_wu, n_wd=n_wd, save_w=save_w)
    out_specs = [pl.BlockSpec((tm, d), lambda i, j: (i, 0)),
                 pl.BlockSpec((None, halo, tf), lambda i, j: (i, 0, j))]
    out_shape = [jax.ShapeDtypeStruct((m, d), F32), jax.ShapeDtypeStruct((ni, halo, f), F32)]
    if save_w:
        out_specs += [pl.BlockSpec((d, tf), lambda i, j: (0, j)), pl.BlockSpec((d, tf), lambda i, j: (0, j)),
                      pl.BlockSpec((tf, d), lambda i, j: (j, 0))]
        out_shape += [jax.ShapeDtypeStruct((d, f), BF16), jax.ShapeDtypeStruct((d, f), BF16),
                      jax.ShapeDtypeStruct((f, d), BF16)]
    scratch = [pltpu.VMEM((tm, d), BF16)]
    if n_wu > 1:
        scratch += [pltpu.VMEM((d, tf), BF16), pltpu.VMEM((d, tf), BF16)]
    if n_wd > 1:
        scratch.append(pltpu.VMEM((tf, d), BF16))
    scratch.append(pltpu.VMEM((halo + tm, tf), F32))
    if tiles_per_seq > 1:
        scratch.append(pltpu.VMEM((nj, halo, tf), F32))
    w_bytes = 4 if n_wu > 1 else 2
    vmem = (tm * d * 4 + 2 * tm * d * 4 + tm * d * 2 + 6 * d * tf * w_bytes + (9 * d * tf * 2 if save_w else 0)
            + (3 * d * tf * 2 if n_wu > 1 else 0) + (halo + tm) * tf * 4 + 6 * tm * tf * 4 + nj * halo * tf * 4)
    outs = pl.pallas_call(
        kern,
        grid=(ni, nj),
        in_specs=([_single((tm, d), lambda i, j: (i, 0)),
                   pl.BlockSpec((halo, tf), lambda i, j: (0, j))]
                  + wg_specs + wv_specs + wd_specs
                  + [_layer_block(l, (FFN_CONV_WIDTH, tf), lambda i, j: (0, j)),
                     _layer_block(l, (1, tf), lambda i, j: (0, j)),
                     _layer_block(l, (1, d), lambda i, j: (0, 0)),
                     _layer_block(l, (1, d), lambda i, j: (0, 0))]),
        out_specs=out_specs,
        out_shape=out_shape,
        scratch_shapes=scratch,
        compiler_params=_params(("arbitrary", "arbitrary"), vmem),
        name=name,
    )(x, g_prev, *wg_args, *wv_args, *wd_args, w_dw, _rows3(b_dw), _rows3(ln_g), _rows3(ln_b))
    if save_w:
        return outs[0], outs[1], ((outs[2], outs[3]), outs[4])
    return outs[0], outs[1]


def _softmax_pv(s, sink, v):
    m = jnp.maximum(jnp.max(s, axis=-1, keepdims=True), sink)
    p = jnp.exp(s - m)
    denom = jnp.sum(p, axis=-1, keepdims=True) + jnp.exp(sink - m)
    o = jnp.dot(p.astype(BF16), v, preferred_element_type=F32)
    return o / denom


def _alibi_bias_table():
    qi = jnp.arange(WINDOW)[:, None]
    kj = jnp.arange(2 * WINDOW)[None, :]
    dist = WINDOW + qi - kj
    band = (dist >= 0) & (dist <= WINDOW)
    slopes = 2.0 ** (-8.0 * jnp.arange(1, N_HEADS + 1, dtype=F32) / N_HEADS)
    pen = slopes[:, None, None] * dist.astype(F32)[None]
    first = band & (kj >= WINDOW)
    table = jnp.stack([jnp.where(first[None], pen, MASK_PENALTY), jnp.where(band[None], pen, MASK_PENALTY)])
    return table.reshape(2, N_HEADS * WINDOW, 2 * WINDOW)


def _attn_prompt_kernel(sinks_ref, q_ref, kvp_ref, kvc_ref, bias_ref, o_ref, *, jb):
    kv_all = jnp.concatenate([kvp_ref[...], kvc_ref[...]], axis=0).astype(BF16)
    for pair in range(N_HEADS // 2):
        outs = []
        for h in (2 * pair, 2 * pair + 1):
            kvh = h // GROUP
            kh = kv_all[:, kvh * HEAD_DIM:(kvh + 1) * HEAD_DIM]
            vh = kv_all[:, KV_WIDTH + kvh * HEAD_DIM:KV_WIDTH + (kvh + 1) * HEAD_DIM]
            s = lax.dot_general(q_ref[h], kh, (((1,), (1,)), ((), ())), preferred_element_type=F32)
            s = s - bias_ref[h * WINDOW:(h + 1) * WINDOW, :]
            outs.append(_softmax_pv(s, sinks_ref[jb, h], vh))
        o_ref[:, pair * 2 * HEAD_DIM:(pair + 1) * 2 * HEAD_DIM] = (
            jnp.concatenate(outs, axis=1).astype(o_ref.dtype))


def _attn_prompt(q, kv, sinks, bias, jb, *, name):
    m = q.shape[1]
    nb = SEQ // WINDOW
    return pl.pallas_call(
        functools.partial(_attn_prompt_kernel, jb=jb),
        grid=(BATCH, nb),
        in_specs=[pl.BlockSpec(memory_space=pltpu.SMEM),
                  pl.BlockSpec((N_HEADS, WINDOW, HEAD_DIM), lambda b, n: (0, b * nb + n, 0)),
                  pl.BlockSpec((WINDOW, 2 * KV_WIDTH), lambda b, n: (b * nb + jnp.maximum(n - 1, 0), 0)),
                  pl.BlockSpec((WINDOW, 2 * KV_WIDTH), lambda b, n: (b * nb + n, 0)),
                  pl.BlockSpec((None, N_HEADS * WINDOW, 2 * WINDOW), lambda b, n: (jnp.minimum(n, 1), 0, 0))],
        out_specs=pl.BlockSpec((WINDOW, D_MODEL), lambda b, n: (b * nb + n, 0)),
        out_shape=jax.ShapeDtypeStruct((m, D_MODEL), BF16),
        compiler_params=_params(("arbitrary", "arbitrary"), 32 << 20),
        name=name,
    )(sinks, q, kv, kv, bias)


def _attn_sample_kernel(sinks_ref, q_ref, kvn_ref, ck_ref, cv_ref, o_ref, *, jb):
    rows = GROUP * DEC_SEQ
    n_keys = WINDOW + DEC_SEQ
    row = lax.broadcasted_iota(jnp.int32, (rows, n_keys), 0)
    kj = lax.broadcasted_iota(jnp.int32, (rows, n_keys), 1)
    head_local = lax.shift_right_logical(row, DEC_SEQ.bit_length() - 1)
    dist = WINDOW + (row & (DEC_SEQ - 1)) - kj
    distf = dist.astype(F32)
    mask = (dist >= 0) & (dist <= WINDOW)
    head_col = lax.shift_right_logical(lax.broadcasted_iota(jnp.int32, (rows, 1), 0), DEC_SEQ.bit_length() - 1)
    kvn = kvn_ref[...]
    pieces = [None] * N_HEADS
    for kvh in range(N_KV_HEADS):
        lo, hi = kvh * HEAD_DIM, (kvh + 1) * HEAD_DIM
        q_st = jnp.concatenate(
            [q_ref[:, (kvh * GROUP + g) * HEAD_DIM:(kvh * GROUP + g + 1) * HEAD_DIM] for g in range(GROUP)],
            axis=0).astype(BF16)
        k_ext = jnp.concatenate([ck_ref[:, lo:hi], kvn[:, lo:hi]], axis=0).astype(BF16)
        v_ext = jnp.concatenate([cv_ref[:, lo:hi], kvn[:, KV_WIDTH + lo:KV_WIDTH + hi]], axis=0).astype(BF16)
        slope = jnp.zeros((rows, n_keys), F32)
        sink = jnp.zeros((rows, 1), F32)
        for g in range(GROUP):
            h = kvh * GROUP + g
            slope = jnp.where(head_local == g, ALIBI_SLOPES[h], slope)
            sink = jnp.where(head_col == g, sinks_ref[jb, h], sink)
        s = lax.dot_general(q_st, k_ext, (((1,), (1,)), ((), ())), preferred_element_type=F32)
        s = jnp.where(mask, s - slope * distf, NEG_BIG)
        o = _softmax_pv(s, sink, v_ext)
        for g in range(GROUP):
            pieces[kvh * GROUP + g] = o[g * DEC_SEQ:(g + 1) * DEC_SEQ, :]
    o_ref[...] = jnp.concatenate(pieces, axis=1)


def _attn_sample(q, kv_new, cache_k, cache_v, sinks, jb, *, name):
    ck = cache_k.reshape(DEC_BATCH, WINDOW, KV_WIDTH)
    cv = cache_v.reshape(DEC_BATCH, WINDOW, KV_WIDTH)
    return pl.pallas_call(
        functools.partial(_attn_sample_kernel, jb=jb),
        grid=(DEC_BATCH,),
        in_specs=[pl.BlockSpec(memory_space=pltpu.SMEM),
                  pl.BlockSpec((None, DEC_SEQ, D_MODEL), lambda b: (b, 0, 0)),
                  pl.BlockSpec((None, DEC_SEQ, 2 * KV_WIDTH), lambda b: (b, 0, 0)),
                  pl.BlockSpec((None, WINDOW, KV_WIDTH), lambda b: (b, 0, 0)),
                  pl.BlockSpec((None, WINDOW, KV_WIDTH), lambda b: (b, 0, 0))],
        out_specs=pl.BlockSpec((None, DEC_SEQ, D_MODEL), lambda b: (b, 0, 0)),
        out_shape=jax.ShapeDtypeStruct((DEC_BATCH, DEC_SEQ, D_MODEL), F32),
        compiler_params=_params(("arbitrary",), 8 << 20),
        name=name,
    )(sinks, q, kv_new, ck, cv)


def _to_time_major(a):
    return jnp.swapaxes(a, 0, 1).reshape(a.shape[1] * a.shape[0], a.shape[2])


def _to_batch_major(a2d, t):
    return jnp.swapaxes(a2d.reshape(t, DEC_BATCH, a2d.shape[1]), 0, 1)


def _trunk(x, conv_prev, ffn_prev, params, *, tag, tm, stride, tiles_per_seq, conv_tail_rows, attention,
           bf16_weights, save_w):
    (a_w_in, a_b_in, a_w_dw, a_b_dw, a_norm_g, a_norm_b, a_w_out, a_b_out, w_kv, b_w_q, b_sinks, b_w_o,
     f_w_up, f_w_dw, f_b_dw, f_w_down, ln_mix_g, ln_mix_b, ln_ffn_g, ln_ffn_b) = params
    u_tails, tails = [], []
    kv = None
    for l in range(DEPTH):
        if l < N_A_LAYERS:
            c, u_tail = _glu_conv(x, a_w_in, a_b_in, conv_prev[l], a_w_dw, a_b_dw, l, tm=tm, tn=GLU_STEP_TN,
                                  stride=stride, tiles_per_seq=tiles_per_seq, tail_rows=conv_tail_rows,
                                  name=f"{tag}_gluconv{l}")
            u_tails.append(u_tail)
            key, w_out, bias, norm, jb = ("aout", l), a_w_out, a_b_out, (a_norm_g, a_norm_b), l
            lhs = c
        else:
            jb = l - N_A_LAYERS
            lhs = attention(x, kv, jb)
            key, w_out, bias, norm = ("bout", jb), b_w_o, None, None
        res = _matmul_res_ln(lhs, norm, w_out if save_w else bf16_weights[key], bias, x, ln_mix_g, ln_mix_b,
                             l, jb, tm=min(OUT_TM, tm), tn=OUT_TN, save_w=save_w, name=f"{tag}_{key[0]}{l}")
        if save_w:
            x, bf16_weights[key] = res
        else:
            x = res
        if save_w:
            x, tail, bf16_weights["ffn", l] = _conv_ffn(
                x, ffn_prev[l], f_w_up, f_w_dw, f_b_dw, f_w_down, ln_ffn_g, ln_ffn_b, l, tm=tm, tf=FFN_TF_F32,
                stride=stride, tiles_per_seq=tiles_per_seq, save_w=True, name=f"{tag}_ffn{l}")
        else:
            w_up_b, w_down_b = bf16_weights["ffn", l]
            x, tail = _conv_ffn(x, ffn_prev[l], w_up_b, f_w_dw, f_b_dw, w_down_b, ln_ffn_g, ln_ffn_b, l, tm=tm,
                                tf=FFN_TF_BF16, stride=stride, tiles_per_seq=tiles_per_seq, name=f"{tag}_ffn{l}")
        tails.append(tail)
        if l == N_A_LAYERS - 1:
            kv = _matmul(x, w_kv[None], 0, tm=tm, tn=2 * KV_WIDTH, out_dtype=F32, name=f"{tag}_kv")
    return x, u_tails, tails, kv


def kernel(x_prompt, x_sample, state_conv_a, state_ffn_conv, cache_k_win, cache_v_win, a_w_in, a_b_in, a_w_dw, a_b_dw, a_norm_g, a_norm_b, a_w_out, a_b_out, w_kv, b_w_q, b_sinks, b_w_o, f_w_up, f_w_dw, f_b_dw, f_w_down, ln_mix_g, ln_mix_b, ln_ffn_g, ln_ffn_b):
    params = (a_w_in, a_b_in, a_w_dw, a_b_dw, a_norm_g, a_norm_b, a_w_out, a_b_out, w_kv, b_w_q, b_sinks, b_w_o,
              f_w_up, f_w_dw, f_b_dw, f_w_down, ln_mix_g, ln_mix_b, ln_ffn_g, ln_ffn_b)
    q_scale = HEAD_DIM ** -0.5
    alibi_bias = _alibi_bias_table()
    bf16_weights = {}

    def attn_s(x, kv, jb):
        q, bf16_weights["q", jb] = _matmul(x, b_w_q, jb, tm=SAMPLE_ROWS, tn=GROUP * HEAD_DIM, out_dtype=F32,
                                           scale=q_scale, save_w=True, name=f"s_q{jb}")
        o = _attn_sample(_to_batch_major(q, DEC_SEQ), _to_batch_major(kv, DEC_SEQ),
                         cache_k_win, cache_v_win, b_sinks, jb, name=f"s_attn{jb}")
        return _to_time_major(o)

    xs = _to_time_major(x_sample)
    conv_prev_s = [_to_time_major(state_conv_a[l]) for l in range(N_A_LAYERS)]
    ffn_prev_s = [_to_time_major(state_ffn_conv[l]) for l in range(DEPTH)]
    ys, us_s, tails_s, kv_s = _trunk(xs, conv_prev_s, ffn_prev_s, params,
                                     tag="s", tm=SAMPLE_ROWS, stride=DEC_BATCH, tiles_per_seq=1,
                                     conv_tail_rows=SAMPLE_ROWS, attention=attn_s,
                                     bf16_weights=bf16_weights, save_w=True)
    y_sample = _to_batch_major(ys, DEC_SEQ)
    conv_a_sample = jnp.stack([
        _to_batch_major(jnp.concatenate([conv_prev_s[l][SAMPLE_ROWS:], us_s[l][0]], axis=0), CONV_A_WIDTH - 1)
        for l in range(N_A_LAYERS)])
    ffn_conv_sample = jnp.stack([_to_batch_major(t[0], FFN_CONV_WIDTH - 1) for t in tails_s])
    kv_s3 = _to_batch_major(kv_s, DEC_SEQ)
    k_new = kv_s3[..., :KV_WIDTH].reshape(DEC_BATCH, DEC_SEQ, N_KV_HEADS, HEAD_DIM)
    v_new = kv_s3[..., KV_WIDTH:].reshape(DEC_BATCH, DEC_SEQ, N_KV_HEADS, HEAD_DIM)
    k_win_sample = jnp.concatenate([cache_k_win[:, DEC_SEQ:], k_new], axis=1)
    v_win_sample = jnp.concatenate([cache_v_win[:, DEC_SEQ:], v_new], axis=1)

    def attn_p(x, kv, jb):
        q = _matmul(x, bf16_weights["q", jb], jb, tm=PROMPT_TM, tn=GROUP * HEAD_DIM, out_dtype=BF16,
                    scale=q_scale, head_major=True, name=f"p_q{jb}")
        return _attn_prompt(q, kv, b_sinks, alibi_bias, jb, name=f"p_attn{jb}")

    xp = x_prompt.reshape(PROMPT_ROWS, D_MODEL)
    zero_conv = jnp.zeros((CONV_HALO, D_MODEL), F32)
    zero_ffn = jnp.zeros((SUBLANES, D_FF), F32)
    yp, us_p, tails_p, kv_p = _trunk(xp, [zero_conv] * N_A_LAYERS, [zero_ffn] * DEPTH, params,
                                     tag="p", tm=PROMPT_TM, stride=1, tiles_per_seq=SEQ // PROMPT_TM,
                                     conv_tail_rows=CONV_HALO, attention=attn_p,
                                     bf16_weights=bf16_weights, save_w=False)
    y_prompt = yp.reshape(BATCH, SEQ, D_MODEL)
    tps = SEQ // PROMPT_TM
    conv_a_prompt = jnp.stack([u[tps - 1::tps, CONV_HALO - (CONV_A_WIDTH - 1):] for u in us_p])
    ffn_conv_prompt = jnp.stack([t[tps - 1::tps, SUBLANES - (FFN_CONV_WIDTH - 1):] for t in tails_p])
    kv_p3 = kv_p.reshape(BATCH, SEQ, 2 * KV_WIDTH)[:, SEQ - WINDOW:]
    k_win_prompt = kv_p3[..., :KV_WIDTH].reshape(BATCH, WINDOW, N_KV_HEADS, HEAD_DIM)
    v_win_prompt = kv_p3[..., KV_WIDTH:].reshape(BATCH, WINDOW, N_KV_HEADS, HEAD_DIM)

    return (y_prompt, y_sample, conv_a_prompt, ffn_conv_prompt, k_win_prompt, v_win_prompt,
            conv_a_sample, ffn_conv_sample, k_win_sample, v_win_sample)
```

```python
import functools

import jax
import jax.numpy as jnp
from jax import lax
from jax.experimental import pallas as pl
from jax.experimental.pallas import tpu as pltpu

D_MODEL = 2048
BATCH = 4
SEQ = 2048
DEPTH = 4
DEC_BATCH = 32
DEC_SEQ = 8
N_A_LAYERS = DEPTH // 2
CONV_A_WIDTH = 31
FFN_CONV_WIDTH = 3
D_FF = ((8 * D_MODEL // 3 + 255) // 256) * 256
HEAD_DIM = 64
N_HEADS = D_MODEL // HEAD_DIM
N_KV_HEADS = max(1, N_HEADS // 8)
GROUP = N_HEADS // N_KV_HEADS
KV_WIDTH = N_KV_HEADS * HEAD_DIM
WINDOW = 128
DN_ALPHA = (2.0 * DEPTH) ** 0.25
LN_EPS = 1e-5
NEG_BIG = -1e30
ALIBI_SLOPES = tuple(2.0 ** (-8.0 * (h + 1) / N_HEADS) for h in range(N_HEADS))

BF16 = jnp.bfloat16
F32 = jnp.float32

V7X_VMEM_BYTES = 64 * 1024 * 1024
SUBLANES = 8

PROMPT_ROWS = BATCH * SEQ
SAMPLE_ROWS = DEC_BATCH * DEC_SEQ
PROMPT_TM = 1024
FFN_TF_F32 = 256
FFN_TF_BF16 = 512
OUT_TM = 512
OUT_TN = 1024
MASK_PENALTY = 1e30
GLU_TN = 256
GLU_STEP_TN = 2 * GLU_TN
W_SPLIT = 4
WD_SPLIT = 2
CONV_HALO = 32


def _params(semantics, vmem_bytes):
    limit = min(int(vmem_bytes * 1.25) + (4 << 20), V7X_VMEM_BYTES - (6 << 20))
    return pltpu.CompilerParams(dimension_semantics=semantics, vmem_limit_bytes=limit)


def _gelu_exact(z):
    return 0.5 * z * (1.0 + lax.erf(z * (0.5 ** 0.5)))


def _single(shape, index_map):
    return pl.BlockSpec(shape, index_map, pipeline_mode=pl.Buffered(1))


def _layer_block(l, shape, index_map):
    return pl.BlockSpec((None,) + shape, lambda *g: (l,) + index_map(*g))


def _rows3(p):
    return p.reshape(p.shape[0], 1, p.shape[1])


def _k_split_specs(l, k, tn, col_map):
    kq = k // W_SPLIT

    def spec(q):
        return _layer_block(l, (kq, tn), lambda i, j: (q, col_map(j)))

    return [spec(q) for q in range(W_SPLIT)]


def _weight_operand(w, l, k, tn, col_map):
    if w.dtype == BF16:
        return [pl.BlockSpec((k, tn), lambda i, j: (0, col_map(j)))], [w]
    return _k_split_specs(l, k, tn, col_map), [w] * W_SPLIT


def _stage_bf16(w_refs, wb_ref):
    kq = w_refs[0].shape[0]
    for q, w_ref in enumerate(w_refs):
        wb_ref[q * kq:(q + 1) * kq, :] = w_ref[...].astype(BF16)


def _bf16_weight(w_refs, wb_ref, copy_ref=None):
    if len(w_refs) == 1:
        return w_refs[0][...]
    _stage_bf16(w_refs, wb_ref)
    if copy_ref is not None:
        copy_ref[...] = wb_ref[...]
    return wb_ref[...]


def _mm_kernel(x_ref, *refs, scale, head_major, n_w, save_w):
    w_refs, rest = refs[:n_w], list(refs[n_w:])
    o_ref = rest.pop(0)
    copy_ref = rest.pop(0) if save_w else None
    xb_ref = rest.pop(0)
    wb_ref = rest.pop(0) if n_w > 1 else None

    @pl.when(pl.program_id(1) == 0)
    def _():
        xb_ref[...] = x_ref[...].astype(BF16)

    acc = jnp.dot(xb_ref[...], _bf16_weight(w_refs, wb_ref, copy_ref), preferred_element_type=F32)
    if scale != 1.0:
        acc = acc * scale
    if head_major:
        for h in range(o_ref.shape[0]):
            o_ref[h] = acc[:, h * HEAD_DIM:(h + 1) * HEAD_DIM].astype(o_ref.dtype)
    else:
        o_ref[...] = acc.astype(o_ref.dtype)


def _matmul(x, w, l, *, tm, tn, out_dtype, scale=1.0, head_major=False, save_w=False, name):
    m, k = x.shape
    n = w.shape[-1]
    assert not save_w or m == tm
    w_specs, w_args = _weight_operand(w, l, k, tn, lambda j: j)
    vmem = 2 * tm * k * 4 + tm * k * 2 + 2 * k * tn * 4 + 3 * k * tn * 2 + 5 * tm * tn * 4
    if head_major:
        heads = tn // HEAD_DIM
        out_specs = [pl.BlockSpec((heads, tm, HEAD_DIM), lambda i, j: (j, i, 0))]
        out_shape = [jax.ShapeDtypeStruct((n // HEAD_DIM, m, HEAD_DIM), out_dtype)]
    else:
        out_specs = [pl.BlockSpec((tm, tn), lambda i, j: (i, j))]
        out_shape = [jax.ShapeDtypeStruct((m, n), out_dtype)]
    if save_w:
        out_specs.append(pl.BlockSpec((k, tn), lambda i, j: (0, j)))
        out_shape.append(jax.ShapeDtypeStruct((k, n), BF16))
    scratch = [pltpu.VMEM((tm, k), BF16)]
    if len(w_specs) > 1:
        scratch.append(pltpu.VMEM((k, tn), BF16))
    outs = pl.pallas_call(
        functools.partial(_mm_kernel, scale=scale, head_major=head_major, n_w=len(w_specs), save_w=save_w),
        grid=(m // tm, n // tn),
        in_specs=[pl.BlockSpec((tm, k), lambda i, j: (i, 0))] + w_specs,
        out_specs=out_specs,
        out_shape=out_shape,
        scratch_shapes=scratch,
        compiler_params=_params(("arbitrary", "arbitrary"), vmem),
        name=name,
    )(x, *w_args)
    return outs if save_w else outs[0]


def _dwconv_unit_stride(ext_ref, w_ref, bias, c_ref, tm, col, tc):
    sub = lax.broadcasted_iota(jnp.int32, (SUBLANES, tc), 0)
    first = CONV_HALO - (CONV_A_WIDTH - 1)
    n_blocks = tm // SUBLANES
    n_ext = CONV_HALO // SUBLANES + 1
    t_prev = None
    for blk in range(n_blocks + 1):
        base = blk * SUBLANES
        e = [ext_ref[base + a * SUBLANES:base + (a + 1) * SUBLANES, :] if base + (a + 1) * SUBLANES <= CONV_HALO + tm
             else None for a in range(n_ext)]
        t = [None] * SUBLANES
        for p in range(SUBLANES):
            if blk == n_blocks and p == 0:
                continue
            for a in range(n_ext):
                k = a * SUBLANES + p - first
                if 0 <= k < CONV_A_WIDTH:
                    term = w_ref[k * SUBLANES:(k + 1) * SUBLANES, col:col + tc] * e[a]
                    t[p] = term if t[p] is None else t[p] + term
        if blk > 0:
            acc = bias + t_prev[0]
            for p in range(1, SUBLANES):
                mixed = jnp.where(sub >= p, t_prev[p], t[p])
                acc = acc + pltpu.roll(mixed, SUBLANES - p, axis=0)
            c_ref[base - SUBLANES:base, col:col + tc] = acc
        t_prev = t


def _dwconv_aligned_stride(ext_ref, w_ref, bias, c_ref, tm, halo, stride, col, tc):
    first = halo - (CONV_A_WIDTH - 1) * stride
    for blk in range(tm // SUBLANES):
        acc = bias
        for k in range(CONV_A_WIDTH):
            off = first + k * stride + blk * SUBLANES
            acc = acc + w_ref[k * SUBLANES:(k + 1) * SUBLANES, col:col + tc] * ext_ref[off:off + SUBLANES, :]
        c_ref[blk * SUBLANES:(blk + 1) * SUBLANES, col:col + tc] = acc


def _glu_conv_kernel(x_ref, *refs, tm, halo, stride, tiles_per_seq):
    wa_refs, wg_refs = refs[:W_SPLIT], refs[W_SPLIT:2 * W_SPLIT]
    ba_ref, bg_ref, prev_ref, wdw_ref, bdw_ref, c_ref, tail_ref, xb_ref, wab_ref, wgb_ref, ext_ref = (
        refs[2 * W_SPLIT:2 * W_SPLIT + 11])
    i = pl.program_id(0)
    j = pl.program_id(1)
    tc = GLU_TN
    n_sub = c_ref.shape[1] // tc

    @pl.when(j == 0)
    def _():
        xb_ref[...] = x_ref[...].astype(BF16)

    if tiles_per_seq > 1:
        carry_ref = refs[2 * W_SPLIT + 11]

        @pl.when((i == 0) & (j == 0))
        def _():
            carry_ref[...] = jnp.zeros_like(carry_ref)

    _stage_bf16(wa_refs, wab_ref)
    _stage_bf16(wg_refs, wgb_ref)
    xb = xb_ref[...]
    for sub in range(n_sub):
        col = sub * tc
        a = jnp.dot(xb, wab_ref[:, col:col + tc], preferred_element_type=F32) + ba_ref[:, col:col + tc]
        g = jnp.dot(xb, wgb_ref[:, col:col + tc], preferred_element_type=F32) + bg_ref[:, col:col + tc]
        u = a * jax.nn.sigmoid(g)
        ext = ext_ref.at[sub]
        if tiles_per_seq > 1:
            ext[0:halo, :] = jnp.where(i % tiles_per_seq == 0, prev_ref[:, col:col + tc],
                                       carry_ref[j, :, col:col + tc])
            carry_ref[j, :, col:col + tc] = u[tm - halo:, :]
        else:
            ext[0:halo, :] = prev_ref[:, col:col + tc]
        ext[halo:halo + tm, :] = u
        tail_ref[:, col:col + tc] = u[tm - tail_ref.shape[0]:, :]

    for sub in range(n_sub):
        col = sub * tc
        ext = ext_ref.at[sub]
        bias = jnp.broadcast_to(bdw_ref[:, col:col + tc], (SUBLANES, tc))
        if stride == 1:
            _dwconv_unit_stride(ext, wdw_ref, bias, c_ref, tm, col, tc)
        else:
            _dwconv_aligned_stride(ext, wdw_ref, bias, c_ref, tm, halo, stride, col, tc)


def _glu_conv(x, w_in, b_in, prev, w_dw, b_dw, l, *, tm, tn, stride, tiles_per_seq, tail_rows, name):
    m, k = x.shape
    n = w_in.shape[2] // 2
    nj = n // tn
    ni = m // tm
    halo = prev.shape[0]
    b3 = _rows3(b_in)
    w_rep = jnp.repeat(w_dw, SUBLANES, axis=1)
    kern = functools.partial(_glu_conv_kernel, tm=tm, halo=halo, stride=stride, tiles_per_seq=tiles_per_seq)
    scratch = [pltpu.VMEM((tm, k), BF16), pltpu.VMEM((k, tn), BF16), pltpu.VMEM((k, tn), BF16),
               pltpu.VMEM((tn // GLU_TN, halo + tm, GLU_TN), F32)]
    if tiles_per_seq > 1:
        scratch.append(pltpu.VMEM((nj, halo, tn), F32))
    vmem = (2 * tm * k * 4 + tm * k * 2 + 4 * k * tn * 4 + 2 * k * tn * 2 + 6 * tm * tn * 4
            + 3 * halo * tn * 4 + (halo + tm) * tn * 4 + nj * halo * tn * 4)
    return pl.pallas_call(
        kern,
        grid=(ni, nj),
        in_specs=([pl.BlockSpec((tm, k), lambda i, j: (i, 0))]
                  + _k_split_specs(l, k, tn, lambda j: j)
                  + _k_split_specs(l, k, tn, lambda j: j + nj)
                  + [_layer_block(l, (1, tn), lambda i, j: (0, j)),
                     _layer_block(l, (1, tn), lambda i, j: (0, j + nj)),
                     pl.BlockSpec((halo, tn), lambda i, j: (0, j)),
                     _layer_block(l, (CONV_A_WIDTH * SUBLANES, tn), lambda i, j: (0, j)),
                     _layer_block(l, (1, tn), lambda i, j: (0, j))]),
        out_specs=[pl.BlockSpec((tm, tn), lambda i, j: (i, j)),
                   pl.BlockSpec((None, tail_rows, tn), lambda i, j: (i, 0, j))],
        out_shape=[jax.ShapeDtypeStruct((m, n), F32),
                   jax.ShapeDtypeStruct((ni, tail_rows, n), F32)],
        scratch_shapes=scratch,
        compiler_params=_params(("arbitrary", "arbitrary"), vmem),
        name=name,
    )(x, *([w_in] * (2 * W_SPLIT)), b3, b3, prev, w_rep, _rows3(b_dw))


def _layer_norm(z, g, b):
    mu = jnp.mean(z, axis=-1, keepdims=True)
    zc = z - mu
    var = jnp.mean(zc * zc, axis=-1, keepdims=True)
    return zc * lax.rsqrt(var + LN_EPS) * g + b


def _residual_ln_chunks(x_ref, acc_ref, g_ref, b_ref, o_ref, tn):
    n_c = acc_ref.shape[0]
    d = n_c * tn
    tot = None
    for jj in range(n_c):
        z = DN_ALPHA * x_ref[:, jj * tn:(jj + 1) * tn] + acc_ref[jj]
        acc_ref[jj] = z
        part = jnp.sum(z, axis=-1, keepdims=True)
        tot = part if tot is None else tot + part
    mu = tot * (1.0 / d)
    sq = None
    for jj in range(n_c):
        zc = acc_ref[jj] - mu
        part = jnp.sum(zc * zc, axis=-1, keepdims=True)
        sq = part if sq is None else sq + part
    rs = lax.rsqrt(sq * (1.0 / d) + LN_EPS)
    for jj in range(n_c):
        o_ref[:, jj * tn:(jj + 1) * tn] = ((acc_ref[jj] - mu) * rs * g_ref[:, jj * tn:(jj + 1) * tn]
                                           + b_ref[:, jj * tn:(jj + 1) * tn])


def _mm_res_ln_kernel(a_ref, *refs, swish_norm, has_bias, tn, n_w, save_w):
    w_refs, refs = refs[:n_w], list(refs[n_w:])
    gn_ref, bn_ref = (refs.pop(0), refs.pop(0)) if swish_norm else (None, None)
    bias_ref = refs.pop(0) if has_bias else None
    x_ref, g_ref, b_ref, o_ref = refs[:4]
    refs = refs[4:]
    copy_ref = refs.pop(0) if save_w else None
    acc_ref = refs.pop(0)
    wb_ref = refs.pop(0) if n_w > 1 else None
    j = pl.program_id(1)

    if swish_norm:
        ab_ref = refs.pop(0)

        @pl.when(j == 0)
        def _():
            y = _layer_norm(a_ref[...], gn_ref[...], bn_ref[...])
            ab_ref[...] = (y * jax.nn.sigmoid(y)).astype(BF16)

        lhs = ab_ref[...]
    else:
        lhs = a_ref[...].astype(BF16)

    part = jnp.dot(lhs, _bf16_weight(w_refs, wb_ref, copy_ref), preferred_element_type=F32)
    if has_bias:
        part = part + bias_ref[...]
    acc_ref[j] = part

    @pl.when(j == pl.num_programs(1) - 1)
    def _():
        _residual_ln_chunks(x_ref, acc_ref, g_ref, b_ref, o_ref, tn)


def _matmul_res_ln(a, norm, w, bias, x, g, b, l, jb, *, tm, tn, save_w=False, name):
    m, kdim = a.shape
    n = w.shape[-1]
    nj = n // tn
    assert not save_w or m == tm
    has_bias = bias is not None
    swish_norm = norm is not None
    w_specs, w_args = _weight_operand(w, jb, kdim, tn, lambda j: j)
    in_specs = [pl.BlockSpec((tm, kdim), lambda i, j: (i, 0))] + w_specs
    args = [a] + w_args
    scratch = [pltpu.VMEM((nj, tm, tn), F32)]
    if len(w_specs) > 1:
        scratch.append(pltpu.VMEM((kdim, tn), BF16))
    if swish_norm:
        in_specs += [_layer_block(jb, (1, kdim), lambda i, j: (0, 0)),
                     _layer_block(jb, (1, kdim), lambda i, j: (0, 0))]
        args += [_rows3(norm[0]), _rows3(norm[1])]
        scratch.append(pltpu.VMEM((tm, kdim), BF16))
    if has_bias:
        in_specs.append(_layer_block(jb, (1, tn), lambda i, j: (0, j)))
        args.append(_rows3(bias))
    in_specs += [pl.BlockSpec((tm, n), lambda i, j: (i, 0)),
                 _layer_block(l, (1, n), lambda i, j: (0, 0)),
                 _layer_block(l, (1, n), lambda i, j: (0, 0))]
    args += [x, _rows3(g), _rows3(b)]
    out_specs = [pl.BlockSpec((tm, n), lambda i, j: (i, 0))]
    out_shape = [jax.ShapeDtypeStruct((m, n), F32)]
    if save_w:
        out_specs.append(pl.BlockSpec((kdim, tn), lambda i, j: (0, j)))
        out_shape.append(jax.ShapeDtypeStruct((kdim, n), BF16))
    vmem = (2 * tm * kdim * a.dtype.itemsize + 2 * kdim * tn * 4 + 3 * kdim * tn * 2 + 2 * tm * n * 4
            + 2 * tm * n * 4 + tm * n * 4 + 2 * tm * tn * 4 + (tm * kdim * 2 if swish_norm else 0))
    outs = pl.pallas_call(
        functools.partial(_mm_res_ln_kernel, swish_norm=swish_norm, has_bias=has_bias, tn=tn,
                          n_w=len(w_specs), save_w=save_w),
        grid=(m // tm, nj),
        in_specs=in_specs,
        out_specs=out_specs,
        out_shape=out_shape,
        scratch_shapes=scratch,
        compiler_params=_params(("arbitrary", "arbitrary"), vmem),
        name=name,
    )(*args)
    return outs if save_w else outs[0]


def _ffn_kernel(x_ref, gprev_ref, *refs, tm, halo, stride, tiles_per_seq, n_wu, n_wd, save_w):
    refs = list(refs)
    wg_refs = [refs.pop(0) for _ in range(n_wu)]
    wv_refs = [refs.pop(0) for _ in range(n_wu)]
    wd_refs = [refs.pop(0) for _ in range(n_wd)]
    wdw_ref, bdw_ref, lg_ref, lb_ref, y_ref, tail_ref = [refs.pop(0) for _ in range(6)]
    wg_copy, wv_copy, wd_copy = [refs.pop(0) for _ in range(3)] if save_w else (None, None, None)
    xb_ref = refs.pop(0)
    wgb_ref, wvb_ref = (refs.pop(0), refs.pop(0)) if n_wu > 1 else (None, None)
    wdb_ref = refs.pop(0) if n_wd > 1 else None
    ext_ref = refs.pop(0)
    i = pl.program_id(0)
    j = pl.program_id(1)

    @pl.when(j == 0)
    def _():
        xb_ref[...] = x_ref[...].astype(BF16)
        y_ref[...] = jnp.zeros_like(y_ref)

    if tiles_per_seq > 1:
        carry_ref = refs.pop(0)

        @pl.when((i == 0) & (j == 0))
        def _():
            carry_ref[...] = jnp.zeros_like(carry_ref)

    xb = xb_ref[...]
    g = jnp.dot(xb, _bf16_weight(wg_refs, wgb_ref, wg_copy), preferred_element_type=F32)
    v = jnp.dot(xb, _bf16_weight(wv_refs, wvb_ref, wv_copy), preferred_element_type=F32)

    if tiles_per_seq > 1:
        prev = jnp.where(i % tiles_per_seq == 0, gprev_ref[...], carry_ref[j])
    else:
        prev = gprev_ref[...]
    ext_ref[0:halo, :] = prev
    ext_ref[halo:halo + tm, :] = g
    g_last = g[tm - halo:, :]
    tail_ref[...] = g_last
    if tiles_per_seq > 1:
        carry_ref[j] = g_last

    c = (wdw_ref[2:3, :] * g
         + wdw_ref[1:2, :] * ext_ref[halo - stride:halo - stride + tm, :]
         + wdw_ref[0:1, :] * ext_ref[halo - 2 * stride:halo - 2 * stride + tm, :]
         + bdw_ref[...])
    h = (_gelu_exact(c) * v).astype(BF16)
    y_ref[...] += jnp.dot(h, _bf16_weight(wd_refs, wdb_ref, wd_copy), preferred_element_type=F32)

    @pl.when(j == pl.num_programs(1) - 1)
    def _():
        y_ref[...] = _layer_norm(DN_ALPHA * x_ref[...] + y_ref[...], lg_ref[...], lb_ref[...])


def _conv_ffn(x, g_prev, w_up, w_dw, b_dw, w_down, ln_g, ln_b, l, *, tm, tf, stride, tiles_per_seq,
              save_w=False, name):
    m, d = x.shape
    halo = g_prev.shape[0]
    ni = m // tm
    assert not save_w or ni == 1
    if isinstance(w_up, tuple):
        f = w_down.shape[0]
        nj = f // tf
        wg_specs, wg_args = _weight_operand(w_up[0], l, d, tf, lambda j: j)
        wv_specs, wv_args = _weight_operand(w_up[1], l, d, tf, lambda j: j)
        wd_specs, wd_args = [pl.BlockSpec((tf, d), lambda i, j: (j, 0))], [w_down]
    else:
        f = w_down.shape[1]
        nj = f // tf
        tfq = tf // WD_SPLIT
        wg_specs, wg_args = _weight_operand(w_up, l, d, tf, lambda j: j)
        wv_specs, wv_args = _weight_operand(w_up, l, d, tf, lambda j: j + nj)

        def wd_spec(q):
            return _layer_block(l, (tfq, d), lambda i, j: (j * WD_SPLIT + q, 0))

        wd_specs, wd_args = [wd_spec(q) for q in range(WD_SPLIT)], [w_down] * WD_SPLIT
    n_wu, n_wd = len(wg_specs), len(wd_specs)
    kern = functools.partial(_ffn_kernel, tm=tm, halo=halo, stride=stride, tiles_per_seq=tiles_per_seq,
                             n_wu=n_wu, n_wd=n_wd, save_w=save_w)
    out_specs = [pl.BlockSpec((tm, d), lambda i, j: (i, 0)),
                 pl.BlockSpec((None, halo, tf), lambda i, j: (i, 0, j))]
    out_shape = [jax.ShapeDtypeStruct((m, d), F32), jax.ShapeDtypeStruct((ni, halo, f), F32)]
    if save_w:
        out_specs += [pl.BlockSpec((d, tf), lambda i, j: (0, j)), pl.BlockSpec((d, tf), lambda i, j: (0, j)),
                      pl.BlockSpec((tf, d), lambda i, j: (j, 0))]
        out_shape += [jax.ShapeDtypeStruct((d, f), BF16), jax.ShapeDtypeStruct((d, f), BF16),
                      jax.ShapeDtypeStruct((f, d), BF16)]
    scratch = [pltpu.VMEM((tm, d), BF16)]
    if n_wu > 1:
        scratch += [pltpu.VMEM((d, tf), BF16), pltpu.VMEM((d, tf), BF16)]
    if n_wd > 1:
        scratch.append(pltpu.VMEM((tf, d), BF16))
    scratch.append(pltpu.VMEM((halo + tm, tf), F32))
    if tiles_per_seq > 1:
        scratch.append(pltpu.VMEM((nj, halo, tf), F32))
    w_bytes = 4 if n_wu > 1 else 2
    vmem = (tm * d * 4 + 2 * tm * d * 4 + tm * d * 2 + 6 * d * tf * w_bytes + (9 * d * tf * 2 if save_w else 0)
            + (3 * d * tf * 2 if n_wu > 1 else 0) + (halo + tm) * tf * 4 + 6 * tm * tf * 4 + nj * halo * tf * 4)
    outs = pl.pallas_call(
        kern,
        grid=(ni, nj),
        in_specs=([_single((tm, d), lambda i, j: (i, 0)),
                   pl.BlockSpec((halo, tf), lambda i, j: (0, j))]
                  + wg_specs + wv_specs + wd_specs
                  + [_layer_block(l, (FFN_CONV_WIDTH, tf), lambda i, j: (0, j)),
                     _layer_block(l, (1, tf), lambda i, j: (0, j)),
                     _layer_block(l, (1, d), lambda i, j: (0, 0)),
                     _layer_block(l, (1, d), lambda i, j: (0, 0))]),
        out_specs=out_specs,
        out_shape=out_shape,
        scratch_shapes=scratch,
        compiler_params=_params(("arbitrary", "arbitrary"), vmem),
        name=name,
    )(x, g_prev, *wg_args, *wv_args, *wd_args, w_dw, _rows3(b_dw), _rows3(ln_g), _rows3(ln_b))
    if save_w:
        return outs[0], outs[1], ((outs[2], outs[3]), outs[4])
    return outs[0], outs[1]


def _softmax_pv(s, sink, v):
    m = jnp.maximum(jnp.max(s, axis=-1, keepdims=True), sink)
    p = jnp.exp(s - m)
    denom = jnp.sum(p, axis=-1, keepdims=True) + jnp.exp(sink - m)
    o = jnp.dot(p.astype(BF16), v, preferred_element_type=F32)
    return o / denom


def _alibi_bias_table():
    qi = jnp.arange(WINDOW)[:, None]
    kj = jnp.arange(2 * WINDOW)[None, :]
    dist = WINDOW + qi - kj
    band = (dist >= 0) & (dist <= WINDOW)
    slopes = 2.0 ** (-8.0 * jnp.arange(1, N_HEADS + 1, dtype=F32) / N_HEADS)
    pen = slopes[:, None, None] * dist.astype(F32)[None]
    first = band & (kj >= WINDOW)
    table = jnp.stack([jnp.where(first[None], pen, MASK_PENALTY), jnp.where(band[None], pen, MASK_PENALTY)])
    return table.reshape(2, N_HEADS * WINDOW, 2 * WINDOW)


def _attn_prompt_kernel(sinks_ref, q_ref, kvp_ref, kvc_ref, bias_ref, o_ref, *, jb):
    kv_all = jnp.concatenate([kvp_ref[...], kvc_ref[...]], axis=0).astype(BF16)
    for pair in range(N_HEADS // 2):
        outs = []
        for h in (2 * pair, 2 * pair + 1):
            kvh = h // GROUP
            kh = kv_all[:, kvh * HEAD_DIM:(kvh + 1) * HEAD_DIM]
            vh = kv_all[:, KV_WIDTH + kvh * HEAD_DIM:KV_WIDTH + (kvh + 1) * HEAD_DIM]
            s = lax.dot_general(q_ref[h], kh, (((1,), (1,)), ((), ())), preferred_element_type=F32)
            s = s - bias_ref[h * WINDOW:(h + 1) * WINDOW, :]
            outs.append(_softmax_pv(s, sinks_ref[jb, h], vh))
        o_ref[:, pair * 2 * HEAD_DIM:(pair + 1) * 2 * HEAD_DIM] = (
            jnp.concatenate(outs, axis=1).astype(o_ref.dtype))


def _attn_prompt(q, kv, sinks, bias, jb, *, name):
    m = q.shape[1]
    nb = SEQ // WINDOW
    return pl.pallas_call(
        functools.partial(_attn_prompt_kernel, jb=jb),
        grid=(BATCH, nb),
        in_specs=[pl.BlockSpec(memory_space=pltpu.SMEM),
                  pl.BlockSpec((N_HEADS, WINDOW, HEAD_DIM), lambda b, n: (0, b * nb + n, 0)),
                  pl.BlockSpec((WINDOW, 2 * KV_WIDTH), lambda b, n: (b * nb + jnp.maximum(n - 1, 0), 0)),
                  pl.BlockSpec((WINDOW, 2 * KV_WIDTH), lambda b, n: (b * nb + n, 0)),
                  pl.BlockSpec((None, N_HEADS * WINDOW, 2 * WINDOW), lambda b, n: (jnp.minimum(n, 1), 0, 0))],
        out_specs=pl.BlockSpec((WINDOW, D_MODEL), lambda b, n: (b * nb + n, 0)),
        out_shape=jax.ShapeDtypeStruct((m, D_MODEL), BF16),
        compiler_params=_params(("arbitrary", "arbitrary"), 32 << 20),
        name=name,
    )(sinks, q, kv, kv, bias)


def _attn_sample_kernel(sinks_ref, q_ref, kvn_ref, ck_ref, cv_ref, o_ref, *, jb):
    rows = GROUP * DEC_SEQ
    n_keys = WINDOW + DEC_SEQ
    row = lax.broadcasted_iota(jnp.int32, (rows, n_keys), 0)
    kj = lax.broadcasted_iota(jnp.int32, (rows, n_keys), 1)
    head_local = lax.shift_right_logical(row, DEC_SEQ.bit_length() - 1)
    dist = WINDOW + (row & (DEC_SEQ - 1)) - kj
    distf = dist.astype(F32)
    mask = (dist >= 0) & (dist <= WINDOW)
    head_col = lax.shift_right_logical(lax.broadcasted_iota(jnp.int32, (rows, 1), 0), DEC_SEQ.bit_length() - 1)
    kvn = kvn_ref[...]
    pieces = [None] * N_HEADS
    for kvh in range(N_KV_HEADS):
        lo, hi = kvh * HEAD_DIM, (kvh + 1) * HEAD_DIM
        q_st = jnp.concatenate(
            [q_ref[:, (kvh * GROUP + g) * HEAD_DIM:(kvh * GROUP + g + 1) * HEAD_DIM] for g in range(GROUP)],
            axis=0).astype(BF16)
        k_ext = jnp.concatenate([ck_ref[:, lo:hi], kvn[:, lo:hi]], axis=0).astype(BF16)
        v_ext = jnp.concatenate([cv_ref[:, lo:hi], kvn[:, KV_WIDTH + lo:KV_WIDTH + hi]], axis=0).astype(BF16)
        slope = jnp.zeros((rows, n_keys), F32)
        sink = jnp.zeros((rows, 1), F32)
        for g in range(GROUP):
            h = kvh * GROUP + g
            slope = jnp.where(head_local == g, ALIBI_SLOPES[h], slope)
            sink = jnp.where(head_col == g, sinks_ref[jb, h], sink)
        s = lax.dot_general(q_st, k_ext, (((1,), (1,)), ((), ())), preferred_element_type=F32)
        s = jnp.where(mask, s - slope * distf, NEG_BIG)
        o = _softmax_pv(s, sink, v_ext)
        for g in range(GROUP):
            pieces[kvh * GROUP + g] = o[g * DEC_SEQ:(g + 1) * DEC_SEQ, :]
    o_ref[...] = jnp.concatenate(pieces, axis=1)


def _attn_sample(q, kv_new, cache_k, cache_v, sinks, jb, *, name):
    ck = cache_k.reshape(DEC_BATCH, WINDOW, KV_WIDTH)
    cv = cache_v.reshape(DEC_BATCH, WINDOW, KV_WIDTH)
    return pl.pallas_call(
        functools.partial(_attn_sample_kernel, jb=jb),
        grid=(DEC_BATCH,),
        in_specs=[pl.BlockSpec(memory_space=pltpu.SMEM),
                  pl.BlockSpec((None, DEC_SEQ, D_MODEL), lambda b: (b, 0, 0)),
                  pl.BlockSpec((None, DEC_SEQ, 2 * KV_WIDTH), lambda b: (b, 0, 0)),
                  pl.BlockSpec((None, WINDOW, KV_WIDTH), lambda b: (b, 0, 0)),
                  pl.BlockSpec((None, WINDOW, KV_WIDTH), lambda b: (b, 0, 0))],
        out_specs=pl.BlockSpec((None, DEC_SEQ, D_MODEL), lambda b: (b, 0, 0)),
        out_shape=jax.ShapeDtypeStruct((DEC_BATCH, DEC_SEQ, D_MODEL), F32),
        compiler_params=_params(("arbitrary",), 8 << 20),
        name=name,
    )(sinks, q, kv_new, ck, cv)


def _to_time_major(a):
    return jnp.swapaxes(a, 0, 1).reshape(a.shape[1] * a.shape[0], a.shape[2])


def _to_batch_major(a2d, t):
    return jnp.swapaxes(a2d.reshape(t, DEC_BATCH, a2d.shape[1]), 0, 1)


def _trunk(x, conv_prev, ffn_prev, params, *, tag, tm, stride, tiles_per_seq, conv_tail_rows, attention,
           bf16_weights, save_w):
    (a_w_in, a_b_in, a_w_dw, a_b_dw, a_norm_g, a_norm_b, a_w_out, a_b_out, w_kv, b_w_q, b_sinks, b_w_o,
     f_w_up, f_w_dw, f_b_dw, f_w_down, ln_mix_g, ln_mix_b, ln_ffn_g, ln_ffn_b) = params
    u_tails, tails = [], []
    kv = None
    for l in range(DEPTH):
        if l < N_A_LAYERS:
            c, u_tail = _glu_conv(x, a_w_in, a_b_in, conv_prev[l], a_w_dw, a_b_dw, l, tm=tm, tn=GLU_STEP_TN,
                                  stride=stride, tiles_per_seq=tiles_per_seq, tail_rows=conv_tail_rows,
                                  name=f"{tag}_gluconv{l}")
            u_tails.append(u_tail)
            key, w_out, bias, norm, jb = ("aout", l), a_w_out, a_b_out, (a_norm_g, a_norm_b), l
            lhs = c
        else:
            jb = l - N_A_LAYERS
            lhs = attention(x, kv, jb)
            key, w_out, bias, norm = ("bout", jb), b_w_o, None, None
        res = _matmul_res_ln(lhs, norm, w_out if save_w else bf16_weights[key], bias, x, ln_mix_g, ln_mix_b,
                             l, jb, tm=min(OUT_TM, tm), tn=OUT_TN, save_w=save_w, name=f"{tag}_{key[0]}{l}")
        if save_w:
            x, bf16_weights[key] = res
        else:
            x = res
        if save_w:
            x, tail, bf16_weights["ffn", l] = _conv_ffn(
                x, ffn_prev[l], f_w_up, f_w_dw, f_b_dw, f_w_down, ln_ffn_g, ln_ffn_b, l, tm=tm, tf=FFN_TF_F32,
                stride=stride, tiles_per_seq=tiles_per_seq, save_w=True, name=f"{tag}_ffn{l}")
        else:
            w_up_b, w_down_b = bf16_weights["ffn", l]
            x, tail = _conv_ffn(x, ffn_prev[l], w_up_b, f_w_dw, f_b_dw, w_down_b, ln_ffn_g, ln_ffn_b, l, tm=tm,
                                tf=FFN_TF_BF16, stride=stride, tiles_per_seq=tiles_per_seq, name=f"{tag}_ffn{l}")
        tails.append(tail)
        if l == N_A_LAYERS - 1:
            kv = _matmul(x, w_kv[None], 0, tm=tm, tn=2 * KV_WIDTH, out_dtype=F32, name=f"{tag}_kv")
    return x, u_tails, tails, kv


def kernel(x_prompt, x_sample, state_conv_a, state_ffn_conv, cache_k_win, cache_v_win, a_w_in, a_b_in, a_w_dw, a_b_dw, a_norm_g, a_norm_b, a_w_out, a_b_out, w_kv, b_w_q, b_sinks, b_w_o, f_w_up, f_w_dw, f_b_dw, f_w_down, ln_mix_g, ln_mix_b, ln_ffn_g, ln_ffn_b):
    params = (a_w_in, a_b_in, a_w_dw, a_b_dw, a_norm_g, a_norm_b, a_w_out, a_b_out, w_kv, b_w_q, b_sinks, b_w_o,
              f_w_up, f_w_dw, f_b_dw, f_w_down, ln_mix_g, ln_mix_b, ln_ffn_g, ln_ffn_b)
    q_scale = HEAD_DIM ** -0.5
    alibi_bias = _alibi_bias_table()
    bf16_weights = {}

    def attn_s(x, kv, jb):
        q, bf16_weights["q", jb] = _matmul(x, b_w_q, jb, tm=SAMPLE_ROWS, tn=GROUP * HEAD_DIM, out_dtype=F32,
                                           scale=q_scale, save_w=True, name=f"s_q{jb}")
        o = _attn_sample(_to_batch_major(q, DEC_SEQ), _to_batch_major(kv, DEC_SEQ),
                         cache_k_win, cache_v_win, b_sinks, jb, name=f"s_attn{jb}")
        return _to_time_major(o)

    xs = _to_time_major(x_sample)
    conv_prev_s = [_to_time_major(state_conv_a[l]) for l in range(N_A_LAYERS)]
    ffn_prev_s = [_to_time_major(state_ffn_conv[l]) for l in range(DEPTH)]
    ys, us_s, tails_s, kv_s = _trunk(xs, conv_prev_s, ffn_prev_s, params,
                                     tag="s", tm=SAMPLE_ROWS, stride=DEC_BATCH, tiles_per_seq=1,
                                     conv_tail_rows=SAMPLE_ROWS, attention=attn_s,
                                     bf16_weights=bf16_weights, save_w=True)
    y_sample = _to_batch_major(ys, DEC_SEQ)
    conv_a_sample = jnp.stack([
        _to_batch_major(jnp.concatenate([conv_prev_s[l][SAMPLE_ROWS:], us_s[l][0]], axis=0), CONV_A_WIDTH - 1)
        for l in range(N_A_LAYERS)])
    ffn_conv_sample = jnp.stack([_to_batch_major(t[0], FFN_CONV_WIDTH - 1) for t in tails_s])
    kv_s3 = _to_batch_major(kv_s, DEC_SEQ)
    k_new = kv_s3[..., :KV_WIDTH].reshape(DEC_BATCH, DEC_SEQ, N_KV_HEADS, HEAD_DIM)
    v_new = kv_s3[..., KV_WIDTH:].reshape(DEC_BATCH, DEC_SEQ, N_KV_HEADS, HEAD_DIM)
    k_win_sample = jnp.concatenate([cache_k_win[:, DEC_SEQ:], k_new], axis=1)
    v_win_sample = jnp.concatenate([cache_v_win[:, DEC_SEQ:], v_new], axis=1)

    def attn_p(x, kv, jb):
        q = _matmul(x, bf16_weights["q", jb], jb, tm=PROMPT_TM, tn=2 * GROUP * HEAD_DIM, out_dtype=BF16,
                    scale=q_scale, head_major=True, name=f"p_q{jb}")
        return _attn_prompt(q, kv, b_sinks, alibi_bias, jb, name=f"p_attn{jb}")

    xp = x_prompt.reshape(PROMPT_ROWS, D_MODEL)
    zero_conv = jnp.zeros((CONV_HALO, D_MODEL), F32)
    zero_ffn = jnp.zeros((SUBLANES, D_FF), F32)
    yp, us_p, tails_p, kv_p = _trunk(xp, [zero_conv] * N_A_LAYERS, [zero_ffn] * DEPTH, params,
                                     tag="p", tm=PROMPT_TM, stride=1, tiles_per_seq=SEQ // PROMPT_TM,
                                     conv_tail_rows=CONV_HALO, attention=attn_p,
                                     bf16_weights=bf16_weights, save_w=False)
    y_prompt = yp.reshape(BATCH, SEQ, D_MODEL)
    tps = SEQ // PROMPT_TM
    conv_a_prompt = jnp.stack([u[tps - 1::tps, CONV_HALO - (CONV_A_WIDTH - 1):] for u in us_p])
    ffn_conv_prompt = jnp.stack([t[tps - 1::tps, SUBLANES - (FFN_CONV_WIDTH - 1):] for t in tails_p])
    kv_p3 = kv_p.reshape(BATCH, SEQ, 2 * KV_WIDTH)[:, SEQ - WINDOW:]
    k_win_prompt = kv_p3[..., :KV_WIDTH].reshape(BATCH, WINDOW, N_KV_HEADS, HEAD_DIM)
    v_win_prompt = kv_p3[..., KV_WIDTH:].reshape(BATCH, WINDOW, N_KV_HEADS, HEAD_DIM)

    return (y_prompt, y_sample, conv_a_prompt, ffn_conv_prompt, k_win_prompt, v_win_prompt,
            conv_a_sample, ffn_conv_sample, k_win_sample, v_win_sample)
```

```python
import functools

import jax
import jax.numpy as jnp
from jax import lax
from jax.experimental import pallas as pl
from jax.experimental.pallas import tpu as pltpu

D_MODEL = 2048
BATCH = 4
SEQ = 2048
DEPTH = 4
DEC_BATCH = 32
DEC_SEQ = 8
N_A_LAYERS = DEPTH // 2
CONV_A_WIDTH = 31
FFN_CONV_WIDTH = 3
D_FF = ((8 * D_MODEL // 3 + 255) // 256) * 256
HEAD_DIM = 64
N_HEADS = D_MODEL // HEAD_DIM
N_KV_HEADS = max(1, N_HEADS // 8)
GROUP = N_HEADS // N_KV_HEADS
KV_WIDTH = N_KV_HEADS * HEAD_DIM
WINDOW = 128
DN_ALPHA = (2.0 * DEPTH) ** 0.25
LN_EPS = 1e-5
NEG_BIG = -1e30
ALIBI_SLOPES = tuple(2.0 ** (-8.0 * (h + 1) / N_HEADS) for h in range(N_HEADS))

BF16 = jnp.bfloat16
F32 = jnp.float32

V7X_VMEM_BYTES = 64 * 1024 * 1024
SUBLANES = 8

PROMPT_ROWS = BATCH * SEQ
SAMPLE_ROWS = DEC_BATCH * DEC_SEQ
PROMPT_TM = 1024
FFN_TF_F32 = 256
FFN_TF_BF16 = 512
OUT_TM = 512
OUT_TN = 1024
OUT_ROWS = 256
MASK_PENALTY = 1e30
GLU_TN = 256
GLU_STEP_TN = 2 * GLU_TN
GLU_DOT_ROWS = 512
W_SPLIT = 4
WD_SPLIT = 2
CONV_HALO = 32


def _params(semantics, vmem_bytes):
    limit = min(int(vmem_bytes * 1.25) + (4 << 20), V7X_VMEM_BYTES - (6 << 20))
    return pltpu.CompilerParams(dimension_semantics=semantics, vmem_limit_bytes=limit)


def _gelu_exact(z):
    return 0.5 * z * (1.0 + lax.erf(z * (0.5 ** 0.5)))


def _single(shape, index_map):
    return pl.BlockSpec(shape, index_map, pipeline_mode=pl.Buffered(1))


def _layer_block(l, shape, index_map):
    return pl.BlockSpec((None,) + shape, lambda *g: (l,) + index_map(*g))


def _rows3(p):
    return p.reshape(p.shape[0], 1, p.shape[1])


def _k_split_specs(l, k, tn, col_map):
    kq = k // W_SPLIT

    def spec(q):
        return _layer_block(l, (kq, tn), lambda i, j: (q, col_map(j)))

    return [spec(q) for q in range(W_SPLIT)]


def _weight_operand(w, l, k, tn, col_map):
    if w.dtype == BF16:
        return [pl.BlockSpec((k, tn), lambda i, j: (0, col_map(j)))], [w]
    return _k_split_specs(l, k, tn, col_map), [w] * W_SPLIT


def _stage_bf16(w_refs, wb_ref):
    kq = w_refs[0].shape[0]
    for q, w_ref in enumerate(w_refs):
        wb_ref[q * kq:(q + 1) * kq, :] = w_ref[...].astype(BF16)


def _bf16_weight(w_refs, wb_ref, copy_ref=None):
    if len(w_refs) == 1:
        return w_refs[0][...]
    _stage_bf16(w_refs, wb_ref)
    if copy_ref is not None:
        copy_ref[...] = wb_ref[...]
    return wb_ref[...]


def _mm_kernel(x_ref, *refs, scale, head_major, n_w, save_w):
    w_refs, rest = refs[:n_w], list(refs[n_w:])
    o_ref = rest.pop(0)
    copy_ref = rest.pop(0) if save_w else None
    xb_ref = rest.pop(0)
    wb_ref = rest.pop(0) if n_w > 1 else None

    @pl.when(pl.program_id(1) == 0)
    def _():
        xb_ref[...] = x_ref[...].astype(BF16)

    acc = jnp.dot(xb_ref[...], _bf16_weight(w_refs, wb_ref, copy_ref), preferred_element_type=F32)
    if scale != 1.0:
        acc = acc * scale
    if head_major:
        for h in range(o_ref.shape[0]):
            o_ref[h] = acc[:, h * HEAD_DIM:(h + 1) * HEAD_DIM].astype(o_ref.dtype)
    else:
        o_ref[...] = acc.astype(o_ref.dtype)


def _matmul(x, w, l, *, tm, tn, out_dtype, scale=1.0, head_major=False, save_w=False, name):
    m, k = x.shape
    n = w.shape[-1]
    assert not save_w or m == tm
    w_specs, w_args = _weight_operand(w, l, k, tn, lambda j: j)
    vmem = 2 * tm * k * 4 + tm * k * 2 + 2 * k * tn * 4 + 3 * k * tn * 2 + 5 * tm * tn * 4
    if head_major:
        heads = tn // HEAD_DIM
        out_specs = [pl.BlockSpec((heads, tm, HEAD_DIM), lambda i, j: (j, i, 0))]
        out_shape = [jax.ShapeDtypeStruct((n // HEAD_DIM, m, HEAD_DIM), out_dtype)]
    else:
        out_specs = [pl.BlockSpec((tm, tn), lambda i, j: (i, j))]
        out_shape = [jax.ShapeDtypeStruct((m, n), out_dtype)]
    if save_w:
        out_specs.append(pl.BlockSpec((k, tn), lambda i, j: (0, j)))
        out_shape.append(jax.ShapeDtypeStruct((k, n), BF16))
    scratch = [pltpu.VMEM((tm, k), BF16)]
    if len(w_specs) > 1:
        scratch.append(pltpu.VMEM((k, tn), BF16))
    outs = pl.pallas_call(
        functools.partial(_mm_kernel, scale=scale, head_major=head_major, n_w=len(w_specs), save_w=save_w),
        grid=(m // tm, n // tn),
        in_specs=[pl.BlockSpec((tm, k), lambda i, j: (i, 0))] + w_specs,
        out_specs=out_specs,
        out_shape=out_shape,
        scratch_shapes=scratch,
        compiler_params=_params(("arbitrary", "arbitrary"), vmem),
        name=name,
    )(x, *w_args)
    return outs if save_w else outs[0]


def _dwconv_unit_stride(ext_ref, w_ref, bias, c_ref, tm, col, tc):
    sub = lax.broadcasted_iota(jnp.int32, (SUBLANES, tc), 0)
    first = CONV_HALO - (CONV_A_WIDTH - 1)
    n_blocks = tm // SUBLANES
    n_ext = CONV_HALO // SUBLANES + 1
    t_prev = None
    for blk in range(n_blocks + 1):
        base = blk * SUBLANES
        e = [ext_ref[base + a * SUBLANES:base + (a + 1) * SUBLANES, :] if base + (a + 1) * SUBLANES <= CONV_HALO + tm
             else None for a in range(n_ext)]
        t = [None] * SUBLANES
        for p in range(SUBLANES):
            if blk == n_blocks and p == 0:
                continue
            for a in range(n_ext):
                k = a * SUBLANES + p - first
                if 0 <= k < CONV_A_WIDTH:
                    term = w_ref[k * SUBLANES:(k + 1) * SUBLANES, col:col + tc] * e[a]
                    t[p] = term if t[p] is None else t[p] + term
        if blk > 0:
            acc = bias + t_prev[0]
            for p in range(1, SUBLANES):
                mixed = jnp.where(sub >= p, t_prev[p], t[p])
                acc = acc + pltpu.roll(mixed, SUBLANES - p, axis=0)
            c_ref[base - SUBLANES:base, col:col + tc] = acc
        t_prev = t


def _dwconv_aligned_stride(ext_ref, w_ref, bias, c_ref, tm, halo, stride, col, tc):
    first = halo - (CONV_A_WIDTH - 1) * stride
    for blk in range(tm // SUBLANES):
        acc = bias
        for k in range(CONV_A_WIDTH):
            off = first + k * stride + blk * SUBLANES
            acc = acc + w_ref[k * SUBLANES:(k + 1) * SUBLANES, col:col + tc] * ext_ref[off:off + SUBLANES, :]
        c_ref[blk * SUBLANES:(blk + 1) * SUBLANES, col:col + tc] = acc


def _glu_conv_kernel(x_ref, *refs, tm, halo, stride, tiles_per_seq):
    wa_refs, wg_refs = refs[:W_SPLIT], refs[W_SPLIT:2 * W_SPLIT]
    ba_ref, bg_ref, prev_ref, wdw_ref, bdw_ref, c_ref, tail_ref, xb_ref, wab_ref, wgb_ref, ext_ref = (
        refs[2 * W_SPLIT:2 * W_SPLIT + 11])
    i = pl.program_id(0)
    j = pl.program_id(1)
    tc = GLU_TN
    n_sub = c_ref.shape[1] // tc

    @pl.when(j == 0)
    def _():
        xb_ref[...] = x_ref[...].astype(BF16)

    if tiles_per_seq > 1:
        carry_ref = refs[2 * W_SPLIT + 11]

        @pl.when((i == 0) & (j == 0))
        def _():
            carry_ref[...] = jnp.zeros_like(carry_ref)

    _stage_bf16(wa_refs, wab_ref)
    _stage_bf16(wg_refs, wgb_ref)
    rows = min(tm, GLU_DOT_ROWS)
    for sub in range(n_sub):
        col = sub * tc
        ext = ext_ref.at[sub]
        if tiles_per_seq > 1:
            ext[0:halo, :] = jnp.where(i % tiles_per_seq == 0, prev_ref[:, col:col + tc],
                                       carry_ref[j, :, col:col + tc])
        else:
            ext[0:halo, :] = prev_ref[:, col:col + tc]
        for r0 in range(0, tm, rows):
            xb = xb_ref[r0:r0 + rows, :]
            a = jnp.dot(xb, wab_ref[:, col:col + tc], preferred_element_type=F32) + ba_ref[:, col:col + tc]
            g = jnp.dot(xb, wgb_ref[:, col:col + tc], preferred_element_type=F32) + bg_ref[:, col:col + tc]
            ext[halo + r0:halo + r0 + rows, :] = a * jax.nn.sigmoid(g)
        if tiles_per_seq > 1:
            carry_ref[j, :, col:col + tc] = ext[tm:halo + tm, :]
        tail_ref[:, col:col + tc] = ext[halo + tm - tail_ref.shape[0]:halo + tm, :]

    for sub in range(n_sub):
        col = sub * tc
        ext = ext_ref.at[sub]
        bias = jnp.broadcast_to(bdw_ref[:, col:col + tc], (SUBLANES, tc))
        if stride == 1:
            _dwconv_unit_stride(ext, wdw_ref, bias, c_ref, tm, col, tc)
        else:
            _dwconv_aligned_stride(ext, wdw_ref, bias, c_ref, tm, halo, stride, col, tc)


def _glu_conv(x, w_in, b_in, prev, w_dw, b_dw, l, *, tm, tn, stride, tiles_per_seq, tail_rows, name):
    m, k = x.shape
    n = w_in.shape[2] // 2
    nj = n // tn
    ni = m // tm
    halo = prev.shape[0]
    b3 = _rows3(b_in)
    w_rep = jnp.repeat(w_dw, SUBLANES, axis=1)
    kern = functools.partial(_glu_conv_kernel, tm=tm, halo=halo, stride=stride, tiles_per_seq=tiles_per_seq)
    scratch = [pltpu.VMEM((tm, k), BF16), pltpu.VMEM((k, tn), BF16), pltpu.VMEM((k, tn), BF16),
               pltpu.VMEM((tn // GLU_TN, halo + tm, GLU_TN), F32)]
    if tiles_per_seq > 1:
        scratch.append(pltpu.VMEM((nj, halo, tn), F32))
    vmem = (2 * tm * k * 4 + tm * k * 2 + 4 * k * tn * 4 + 2 * k * tn * 2 + 6 * tm * tn * 4
            + 3 * halo * tn * 4 + (halo + tm) * tn * 4 + nj * halo * tn * 4)
    return pl.pallas_call(
        kern,
        grid=(ni, nj),
        in_specs=([pl.BlockSpec((tm, k), lambda i, j: (i, 0))]
                  + _k_split_specs(l, k, tn, lambda j: j)
                  + _k_split_specs(l, k, tn, lambda j: j + nj)
                  + [_layer_block(l, (1, tn), lambda i, j: (0, j)),
                     _layer_block(l, (1, tn), lambda i, j: (0, j + nj)),
                     pl.BlockSpec((halo, tn), lambda i, j: (0, j)),
                     _layer_block(l, (CONV_A_WIDTH * SUBLANES, tn), lambda i, j: (0, j)),
                     _layer_block(l, (1, tn), lambda i, j: (0, j))]),
        out_specs=[pl.BlockSpec((tm, tn), lambda i, j: (i, j)),
                   pl.BlockSpec((None, tail_rows, tn), lambda i, j: (i, 0, j))],
        out_shape=[jax.ShapeDtypeStruct((m, n), F32),
                   jax.ShapeDtypeStruct((ni, tail_rows, n), F32)],
        scratch_shapes=scratch,
        compiler_params=_params(("arbitrary", "arbitrary"), vmem),
        name=name,
    )(x, *([w_in] * (2 * W_SPLIT)), b3, b3, prev, w_rep, _rows3(b_dw))


def _layer_norm(z, g, b):
    mu = jnp.mean(z, axis=-1, keepdims=True)
    zc = z - mu
    var = jnp.mean(zc * zc, axis=-1, keepdims=True)
    return zc * lax.rsqrt(var + LN_EPS) * g + b


def _residual_ln_chunks(x_ref, acc_ref, g_ref, b_ref, o_ref, tn):
    n_c = acc_ref.shape[0]
    d = n_c * tn
    tot = None
    for jj in range(n_c):
        z = DN_ALPHA * x_ref[:, jj * tn:(jj + 1) * tn] + acc_ref[jj]
        acc_ref[jj] = z
        part = jnp.sum(z, axis=-1, keepdims=True)
        tot = part if tot is None else tot + part
    mu = tot * (1.0 / d)
    sq = None
    for jj in range(n_c):
        zc = acc_ref[jj] - mu
        part = jnp.sum(zc * zc, axis=-1, keepdims=True)
        sq = part if sq is None else sq + part
    rs = lax.rsqrt(sq * (1.0 / d) + LN_EPS)
    for jj in range(n_c):
        o_ref[:, jj * tn:(jj + 1) * tn] = ((acc_ref[jj] - mu) * rs * g_ref[:, jj * tn:(jj + 1) * tn]
                                           + b_ref[:, jj * tn:(jj + 1) * tn])


def _mm_res_ln_kernel(a_ref, *refs, swish_norm, has_bias, tn, n_w, save_w):
    w_refs, refs = refs[:n_w], list(refs[n_w:])
    gn_ref, bn_ref = (refs.pop(0), refs.pop(0)) if swish_norm else (None, None)
    bias_ref = refs.pop(0) if has_bias else None
    x_ref, g_ref, b_ref, o_ref = refs[:4]
    refs = refs[4:]
    copy_ref = refs.pop(0) if save_w else None
    acc_ref = refs.pop(0)
    wb_ref = refs.pop(0) if n_w > 1 else None
    j = pl.program_id(1)

    if swish_norm:
        ab_ref = refs.pop(0)

        @pl.when(j == 0)
        def _():
            y = _layer_norm(a_ref[...], gn_ref[...], bn_ref[...])
            ab_ref[...] = (y * jax.nn.sigmoid(y)).astype(BF16)

        lhs = ab_ref[...]
    else:
        lhs = a_ref[...].astype(BF16)

    part = jnp.dot(lhs, _bf16_weight(w_refs, wb_ref, copy_ref), preferred_element_type=F32)
    if has_bias:
        part = part + bias_ref[...]
    acc_ref[j] = part

    @pl.when(j == pl.num_programs(1) - 1)
    def _():
        _residual_ln_chunks(x_ref, acc_ref, g_ref, b_ref, o_ref, tn)


def _matmul_res_ln(a, norm, w, bias, x, g, b, l, jb, *, tm, tn, save_w=False, name):
    m, kdim = a.shape
    n = w.shape[-1]
    nj = n // tn
    assert not save_w or m == tm
    has_bias = bias is not None
    swish_norm = norm is not None
    w_specs, w_args = _weight_operand(w, jb, kdim, tn, lambda j: j)
    in_specs = [pl.BlockSpec((tm, kdim), lambda i, j: (i, 0))] + w_specs
    args = [a] + w_args
    scratch = [pltpu.VMEM((nj, tm, tn), F32)]
    if len(w_specs) > 1:
        scratch.append(pltpu.VMEM((kdim, tn), BF16))
    if swish_norm:
        in_specs += [_layer_block(jb, (1, kdim), lambda i, j: (0, 0)),
                     _layer_block(jb, (1, kdim), lambda i, j: (0, 0))]
        args += [_rows3(norm[0]), _rows3(norm[1])]
        scratch.append(pltpu.VMEM((tm, kdim), BF16))
    if has_bias:
        in_specs.append(_layer_block(jb, (1, tn), lambda i, j: (0, j)))
        args.append(_rows3(bias))
    in_specs += [pl.BlockSpec((tm, n), lambda i, j: (i, 0)),
                 _layer_block(l, (1, n), lambda i, j: (0, 0)),
                 _layer_block(l, (1, n), lambda i, j: (0, 0))]
    args += [x, _rows3(g), _rows3(b)]
    out_specs = [pl.BlockSpec((tm, n), lambda i, j: (i, 0))]
    out_shape = [jax.ShapeDtypeStruct((m, n), F32)]
    if save_w:
        out_specs.append(pl.BlockSpec((kdim, tn), lambda i, j: (0, j)))
        out_shape.append(jax.ShapeDtypeStruct((kdim, n), BF16))
    vmem = (2 * tm * kdim * a.dtype.itemsize + 2 * kdim * tn * 4 + 3 * kdim * tn * 2 + 2 * tm * n * 4
            + 2 * tm * n * 4 + tm * n * 4 + 2 * tm * tn * 4 + (tm * kdim * 2 if swish_norm else 0))
    outs = pl.pallas_call(
        functools.partial(_mm_res_ln_kernel, swish_norm=swish_norm, has_bias=has_bias, tn=tn,
                          n_w=len(w_specs), save_w=save_w),
        grid=(m // tm, nj),
        in_specs=in_specs,
        out_specs=out_specs,
        out_shape=out_shape,
        scratch_shapes=scratch,
        compiler_params=_params(("arbitrary", "arbitrary"), vmem),
        name=name,
    )(*args)
    return outs if save_w else outs[0]


def _mm_res_ln_rows_kernel(a_ref, w_ref, *refs, swish_norm, has_bias, rows):
    refs = list(refs)
    gn_ref, bn_ref = (refs.pop(0), refs.pop(0)) if swish_norm else (None, None)
    bias_ref = refs.pop(0) if has_bias else None
    x_ref, g_ref, b_ref, o_ref = refs
    for r0 in range(0, a_ref.shape[0], rows):
        a = a_ref[r0:r0 + rows, :]
        if swish_norm:
            y = _layer_norm(a, gn_ref[...], bn_ref[...])
            a = y * jax.nn.sigmoid(y)
        f = jnp.dot(a.astype(BF16), w_ref[...], preferred_element_type=F32)
        if has_bias:
            f = f + bias_ref[...]
        o_ref[r0:r0 + rows, :] = _layer_norm(DN_ALPHA * x_ref[r0:r0 + rows, :] + f, g_ref[...], b_ref[...])


def _matmul_res_ln_rows(a, norm, w_bf16, bias, x, g, b, l, jb, *, tm, rows, name):
    m, kdim = a.shape
    n = w_bf16.shape[1]
    has_bias = bias is not None
    swish_norm = norm is not None
    in_specs = [pl.BlockSpec((tm, kdim), lambda i: (i, 0)), pl.BlockSpec((kdim, n), lambda i: (0, 0))]
    args = [a, w_bf16]
    if swish_norm:
        in_specs += [_layer_block(jb, (1, kdim), lambda i: (0, 0)), _layer_block(jb, (1, kdim), lambda i: (0, 0))]
        args += [_rows3(norm[0]), _rows3(norm[1])]
    if has_bias:
        in_specs.append(_layer_block(jb, (1, n), lambda i: (0, 0)))
        args.append(_rows3(bias))
    in_specs += [pl.BlockSpec((tm, n), lambda i: (i, 0)),
                 _layer_block(l, (1, n), lambda i: (0, 0)), _layer_block(l, (1, n), lambda i: (0, 0))]
    args += [x, _rows3(g), _rows3(b)]
    vmem = 2 * tm * kdim * a.dtype.itemsize + 2 * kdim * n * 2 + 4 * tm * n * 4 + 6 * rows * n * 4
    return pl.pallas_call(
        functools.partial(_mm_res_ln_rows_kernel, swish_norm=swish_norm, has_bias=has_bias, rows=rows),
        grid=(m // tm,),
        in_specs=in_specs,
        out_specs=pl.BlockSpec((tm, n), lambda i: (i, 0)),
        out_shape=jax.ShapeDtypeStruct((m, n), F32),
        compiler_params=_params(("arbitrary",), vmem),
        name=name,
    )(*args)


def _ffn_kernel(x_ref, gprev_ref, *refs, tm, halo, stride, tiles_per_seq, n_wu, n_wd, save_w):
    refs = list(refs)
    wg_refs = [refs.pop(0) for _ in range(n_wu)]
    wv_refs = [refs.pop(0) for _ in range(n_wu)]
    wd_refs = [refs.pop(0) for _ in range(n_wd)]
    wdw_ref, bdw_ref, lg_ref, lb_ref, y_ref, tail_ref = [refs.pop(0) for _ in range(6)]
    wg_copy, wv_copy, wd_copy = [refs.pop(0) for _ in range(3)] if save_w else (None, None, None)
    xb_ref = refs.pop(0)
    wgb_ref, wvb_ref = (refs.pop(0), refs.pop(0)) if n_wu > 1 else (None, None)
    wdb_ref = refs.pop(0) if n_wd > 1 else None
    ext_ref = refs.pop(0)
    i = pl.program_id(0)
    j = pl.program_id(1)

    @pl.when(j == 0)
    def _():
        xb_ref[...] = x_ref[...].astype(BF16)
        y_ref[...] = jnp.zeros_like(y_ref)

    if tiles_per_seq > 1:
        carry_ref = refs.pop(0)

        @pl.when((i == 0) & (j == 0))
        def _():
            carry_ref[...] = jnp.zeros_like(carry_ref)

    xb = xb_ref[...]
    g = jnp.dot(xb, _bf16_weight(wg_refs, wgb_ref, wg_copy), preferred_element_type=F32)
    v = jnp.dot(xb, _bf16_weight(wv_refs, wvb_ref, wv_copy), preferred_element_type=F32)

    if tiles_per_seq > 1:
        prev = jnp.where(i % tiles_per_seq == 0, gprev_ref[...], carry_ref[j])
    else:
        prev = gprev_ref[...]
    ext_ref[0:halo, :] = prev
    ext_ref[halo:halo + tm, :] = g
    g_last = g[tm - halo:, :]
    tail_ref[...] = g_last
    if tiles_per_seq > 1:
        carry_ref[j] = g_last

    c = (wdw_ref[2:3, :] * g
         + wdw_ref[1:2, :] * ext_ref[halo - stride:halo - stride + tm, :]
         + wdw_ref[0:1, :] * ext_ref[halo - 2 * stride:halo - 2 * stride + tm, :]
         + bdw_ref[...])
    h = (_gelu_exact(c) * v).astype(BF16)
    y_ref[...] += jnp.dot(h, _bf16_weight(wd_refs, wdb_ref, wd_copy), preferred_element_type=F32)

    @pl.when(j == pl.num_programs(1) - 1)
    def _():
        y_ref[...] = _layer_norm(DN_ALPHA * x_ref[...] + y_ref[...], lg_ref[...], lb_ref[...])


def _conv_ffn(x, g_prev, w_up, w_dw, b_dw, w_down, ln_g, ln_b, l, *, tm, tf, stride, tiles_per_seq,
              save_w=False, name):
    m, d = x.shape
    halo = g_prev.shape[0]
    ni = m // tm
    assert not save_w or ni == 1
    if isinstance(w_up, tuple):
        f = w_down.shape[0]
        nj = f // tf
        wg_specs, wg_args = _weight_operand(w_up[0], l, d, tf, lambda j: j)
        wv_specs, wv_args = _weight_operand(w_up[1], l, d, tf, lambda j: j)
        wd_specs, wd_args = [pl.BlockSpec((tf, d), lambda i, j: (j, 0))], [w_down]
    else:
        f = w_down.shape[1]
        nj = f // tf
        tfq = tf // WD_SPLIT
        wg_specs, wg_args = _weight_operand(w_up, l, d, tf, lambda j: j)
        wv_specs, wv_args = _weight_operand(w_up, l, d, tf, lambda j: j + nj)

        def wd_spec(q):
            return _layer_block(l, (tfq, d), lambda i, j: (j * WD_SPLIT + q, 0))

        wd_specs, wd_args = [wd_spec(q) for q in range(WD_SPLIT)], [w_down] * WD_SPLIT
    n_wu, n_wd = len(wg_specs), len(wd_specs)
    kern = functools.partial(_ffn_kernel, tm=tm, halo=halo, stride=stride, tiles_per_seq=tiles_per_seq,
                             n_wu=n_wu, n_wd=n_wd, save_w=save_w)
    out_specs = [pl.BlockSpec((tm, d), lambda i, j: (i, 0)),
                 pl.BlockSpec((None, halo, tf), lambda i, j: (i, 0, j))]
    out_shape = [jax.ShapeDtypeStruct((m, d), F32), jax.ShapeDtypeStruct((ni, halo, f), F32)]
    if save_w:
        out_specs += [pl.BlockSpec((d, tf), lambda i, j: (0, j)), pl.BlockSpec((d, tf), lambda i, j: (0, j)),
                      pl.BlockSpec((tf, d), lambda i, j: (j, 0))]
        out_shape += [jax.ShapeDtypeStruct((d, f), BF16), jax.ShapeDtypeStruct((d, f), BF16),
                      jax.ShapeDtypeStruct((f, d), BF16)]
    scratch = [pltpu.VMEM((tm, d), BF16)]
    if n_wu > 1:
        scratch += [pltpu.VMEM((d, tf), BF16), pltpu.VMEM((d, tf), BF16)]
    if n_wd > 1:
        scratch.append(pltpu.VMEM((tf, d), BF16))
    scratch.append(pltpu.VMEM((halo + tm, tf), F32))
    if tiles_per_seq > 1:
        scratch.append(pltpu.VMEM((nj, halo, tf), F32))
    w_bytes = 4 if n_wu > 1 else 2
    vmem = (tm * d * 4 + 2 * tm * d * 4 + tm * d * 2 + 6 * d * tf * w_bytes + (9 * d * tf * 2 if save_w else 0)
            + (3 * d * tf * 2 if n_wu > 1 else 0) + (halo + tm) * tf * 4 + 6 * tm * tf * 4 + nj * halo * tf * 4)
    outs = pl.pallas_call(
        kern,
        grid=(ni, nj),
        in_specs=([_single((tm, d), lambda i, j: (i, 0)),
                   pl.BlockSpec((halo, tf), lambda i, j: (0, j))]
                  + wg_specs + wv_specs + wd_specs
                  + [_layer_block(l, (FFN_CONV_WIDTH, tf), lambda i, j: (0, j)),
                     _layer_block(l, (1, tf), lambda i, j: (0, j)),
                     _layer_block(l, (1, d), lambda i, j: (0, 0)),
                     _layer_block(l, (1, d), lambda i, j: (0, 0))]),
        out_specs=out_specs,
        out_shape=out_shape,
        scratch_shapes=scratch,
        compiler_params=_params(("arbitrary", "arbitrary"), vmem),
        name=name,
    )(x, g_prev, *wg_args, *wv_args, *wd_args, w_dw, _rows3(b_dw), _rows3(ln_g), _rows3(ln_b))
    if save_w:
        return outs[0], outs[1], ((outs[2], outs[3]), outs[4])
    return outs[0], outs[1]


def _softmax_pv(s, sink, v):
    m = jnp.maximum(jnp.max(s, axis=-1, keepdims=True), sink)
    p = jnp.exp(s - m)
    denom = jnp.sum(p, axis=-1, keepdims=True) + jnp.exp(sink - m)
    o = jnp.dot(p.astype(BF16), v, preferred_element_type=F32)
    return o / denom


def _alibi_bias_table():
    qi = jnp.arange(WINDOW)[:, None]
    kj = jnp.arange(2 * WINDOW)[None, :]
    dist = WINDOW + qi - kj
    band = (dist >= 0) & (dist <= WINDOW)
    slopes = 2.0 ** (-8.0 * jnp.arange(1, N_HEADS + 1, dtype=F32) / N_HEADS)
    pen = slopes[:, None, None] * dist.astype(F32)[None]
    first = band & (kj >= WINDOW)
    table = jnp.stack([jnp.where(first[None], pen, MASK_PENALTY), jnp.where(band[None], pen, MASK_PENALTY)])
    return table.reshape(2, N_HEADS * WINDOW, 2 * WINDOW)


def _attn_prompt_kernel(sinks_ref, q_ref, kvp_ref, kvc_ref, bias_ref, o_ref, *, jb):
    kv_all = jnp.concatenate([kvp_ref[...], kvc_ref[...]], axis=0).astype(BF16)
    for pair in range(N_HEADS // 2):
        outs = []
        for h in (2 * pair, 2 * pair + 1):
            kvh = h // GROUP
            kh = kv_all[:, kvh * HEAD_DIM:(kvh + 1) * HEAD_DIM]
            vh = kv_all[:, KV_WIDTH + kvh * HEAD_DIM:KV_WIDTH + (kvh + 1) * HEAD_DIM]
            s = lax.dot_general(q_ref[h], kh, (((1,), (1,)), ((), ())), preferred_element_type=F32)
            s = s - bias_ref[h * WINDOW:(h + 1) * WINDOW, :]
            outs.append(_softmax_pv(s, sinks_ref[jb, h], vh))
        o_ref[:, pair * 2 * HEAD_DIM:(pair + 1) * 2 * HEAD_DIM] = (
            jnp.concatenate(outs, axis=1).astype(o_ref.dtype))


def _attn_prompt(q, kv, sinks, bias, jb, *, name):
    m = q.shape[1]
    nb = SEQ // WINDOW
    return pl.pallas_call(
        functools.partial(_attn_prompt_kernel, jb=jb),
        grid=(BATCH, nb),
        in_specs=[pl.BlockSpec(memory_space=pltpu.SMEM),
                  pl.BlockSpec((N_HEADS, WINDOW, HEAD_DIM), lambda b, n: (0, b * nb + n, 0)),
                  pl.BlockSpec((WINDOW, 2 * KV_WIDTH), lambda b, n: (b * nb + jnp.maximum(n - 1, 0), 0)),
                  pl.BlockSpec((WINDOW, 2 * KV_WIDTH), lambda b, n: (b * nb + n, 0)),
                  pl.BlockSpec((None, N_HEADS * WINDOW, 2 * WINDOW), lambda b, n: (jnp.minimum(n, 1), 0, 0))],
        out_specs=pl.BlockSpec((WINDOW, D_MODEL), lambda b, n: (b * nb + n, 0)),
        out_shape=jax.ShapeDtypeStruct((m, D_MODEL), BF16),
        compiler_params=_params(("arbitrary", "arbitrary"), 32 << 20),
        name=name,
    )(sinks, q, kv, kv, bias)


def _attn_sample_kernel(sinks_ref, q_ref, kvn_ref, ck_ref, cv_ref, o_ref, *, jb):
    rows = GROUP * DEC_SEQ
    n_keys = WINDOW + DEC_SEQ
    row = lax.broadcasted_iota(jnp.int32, (rows, n_keys), 0)
    kj = lax.broadcasted_iota(jnp.int32, (rows, n_keys), 1)
    head_local = lax.shift_right_logical(row, DEC_SEQ.bit_length() - 1)
    dist = WINDOW + (row & (DEC_SEQ - 1)) - kj
    distf = dist.astype(F32)
    mask = (dist >= 0) & (dist <= WINDOW)
    head_col = lax.shift_right_logical(lax.broadcasted_iota(jnp.int32, (rows, 1), 0), DEC_SEQ.bit_length() - 1)
    kvn = kvn_ref[...]
    pieces = [None] * N_HEADS
    for kvh in range(N_KV_HEADS):
        lo, hi = kvh * HEAD_DIM, (kvh + 1) * HEAD_DIM
        q_st = jnp.concatenate(
            [q_ref[:, (kvh * GROUP + g) * HEAD_DIM:(kvh * GROUP + g + 1) * HEAD_DIM] for g in range(GROUP)],
            axis=0).astype(BF16)
        k_ext = jnp.concatenate([ck_ref[:, lo:hi], kvn[:, lo:hi]], axis=0).astype(BF16)
        v_ext = jnp.concatenate([cv_ref[:, lo:hi], kvn[:, KV_WIDTH + lo:KV_WIDTH + hi]], axis=0).astype(BF16)
        slope = jnp.zeros((rows, n_keys), F32)
        sink = jnp.zeros((rows, 1), F32)
        for g in range(GROUP):
            h = kvh * GROUP + g
            slope = jnp.where(head_local == g, ALIBI_SLOPES[h], slope)
            sink = jnp.where(head_col == g, sinks_ref[jb, h], sink)
        s = lax.dot_general(q_st, k_ext, (((1,), (1,)), ((), ())), preferred_element_type=F32)
        s = jnp.where(mask, s - slope * distf, NEG_BIG)
        o = _softmax_pv(s, sink, v_ext)
        for g in range(GROUP):
            pieces[kvh * GROUP + g] = o[g * DEC_SEQ:(g + 1) * DEC_SEQ, :]
    o_ref[...] = jnp.concatenate(pieces, axis=1)


def _attn_sample(q, kv_new, cache_k, cache_v, sinks, jb, *, name):
    ck = cache_k.reshape(DEC_BATCH, WINDOW, KV_WIDTH)
    cv = cache_v.reshape(DEC_BATCH, WINDOW, KV_WIDTH)
    return pl.pallas_call(
        functools.partial(_attn_sample_kernel, jb=jb),
        grid=(DEC_BATCH,),
        in_specs=[pl.BlockSpec(memory_space=pltpu.SMEM),
                  pl.BlockSpec((None, DEC_SEQ, D_MODEL), lambda b: (b, 0, 0)),
                  pl.BlockSpec((None, DEC_SEQ, 2 * KV_WIDTH), lambda b: (b, 0, 0)),
                  pl.BlockSpec((None, WINDOW, KV_WIDTH), lambda b: (b, 0, 0)),
                  pl.BlockSpec((None, WINDOW, KV_WIDTH), lambda b: (b, 0, 0))],
        out_specs=pl.BlockSpec((None, DEC_SEQ, D_MODEL), lambda b: (b, 0, 0)),
        out_shape=jax.ShapeDtypeStruct((DEC_BATCH, DEC_SEQ, D_MODEL), F32),
        compiler_params=_params(("arbitrary",), 8 << 20),
        name=name,
    )(sinks, q, kv_new, ck, cv)


def _to_time_major(a):
    return jnp.swapaxes(a, 0, 1).reshape(a.shape[1] * a.shape[0], a.shape[2])


def _to_batch_major(a2d, t):
    return jnp.swapaxes(a2d.reshape(t, DEC_BATCH, a2d.shape[1]), 0, 1)


def _trunk(x, conv_prev, ffn_prev, params, *, tag, tm, stride, tiles_per_seq, conv_tail_rows, attention,
           bf16_weights, save_w):
    (a_w_in, a_b_in, a_w_dw, a_b_dw, a_norm_g, a_norm_b, a_w_out, a_b_out, w_kv, b_w_q, b_sinks, b_w_o,
     f_w_up, f_w_dw, f_b_dw, f_w_down, ln_mix_g, ln_mix_b, ln_ffn_g, ln_ffn_b) = params
    u_tails, tails = [], []
    kv = None
    for l in range(DEPTH):
        if l < N_A_LAYERS:
            c, u_tail = _glu_conv(x, a_w_in, a_b_in, conv_prev[l], a_w_dw, a_b_dw, l, tm=tm, tn=GLU_STEP_TN,
                                  stride=stride, tiles_per_seq=tiles_per_seq, tail_rows=conv_tail_rows,
                                  name=f"{tag}_gluconv{l}")
            u_tails.append(u_tail)
            key, w_out, bias, norm, jb = ("aout", l), a_w_out, a_b_out, (a_norm_g, a_norm_b), l
            lhs = c
        else:
            jb = l - N_A_LAYERS
            lhs = attention(x, kv, jb)
            key, w_out, bias, norm = ("bout", jb), b_w_o, None, None
        if save_w:
            x, bf16_weights[key] = _matmul_res_ln(lhs, norm, w_out, bias, x, ln_mix_g, ln_mix_b, l, jb,
                                                  tm=min(OUT_TM, tm), tn=OUT_TN, save_w=True,
                                                  name=f"{tag}_{key[0]}{l}")
        else:
            x = _matmul_res_ln_rows(lhs, norm, bf16_weights[key], bias, x, ln_mix_g, ln_mix_b, l, jb,
                                    tm=OUT_TM, rows=OUT_ROWS, name=f"{tag}_{key[0]}{l}")
        if save_w:
            x, tail, bf16_weights["ffn", l] = _conv_ffn(
                x, ffn_prev[l], f_w_up, f_w_dw, f_b_dw, f_w_down, ln_ffn_g, ln_ffn_b, l, tm=tm, tf=FFN_TF_F32,
                stride=stride, tiles_per_seq=tiles_per_seq, save_w=True, name=f"{tag}_ffn{l}")
        else:
            w_up_b, w_down_b = bf16_weights["ffn", l]
            x, tail = _conv_ffn(x, ffn_prev[l], w_up_b, f_w_dw, f_b_dw, w_down_b, ln_ffn_g, ln_ffn_b, l, tm=tm,
                                tf=FFN_TF_BF16, stride=stride, tiles_per_seq=tiles_per_seq, name=f"{tag}_ffn{l}")
        tails.append(tail)
        if l == N_A_LAYERS - 1:
            kv = _matmul(x, w_kv[None], 0, tm=tm, tn=2 * KV_WIDTH, out_dtype=F32, name=f"{tag}_kv")
    return x, u_tails, tails, kv


def kernel(x_prompt, x_sample, state_conv_a, state_ffn_conv, cache_k_win, cache_v_win, a_w_in, a_b_in, a_w_dw, a_b_dw, a_norm_g, a_norm_b, a_w_out, a_b_out, w_kv, b_w_q, b_sinks, b_w_o, f_w_up, f_w_dw, f_b_dw, f_w_down, ln_mix_g, ln_mix_b, ln_ffn_g, ln_ffn_b):
    params = (a_w_in, a_b_in, a_w_dw, a_b_dw, a_norm_g, a_norm_b, a_w_out, a_b_out, w_kv, b_w_q, b_sinks, b_w_o,
              f_w_up, f_w_dw, f_b_dw, f_w_down, ln_mix_g, ln_mix_b, ln_ffn_g, ln_ffn_b)
    q_scale = HEAD_DIM ** -0.5
    alibi_bias = _alibi_bias_table()
    bf16_weights = {}

    def attn_s(x, kv, jb):
        q, bf16_weights["q", jb] = _matmul(x, b_w_q, jb, tm=SAMPLE_ROWS, tn=GROUP * HEAD_DIM, out_dtype=F32,
                                           scale=q_scale, save_w=True, name=f"s_q{jb}")
        o = _attn_sample(_to_batch_major(q, DEC_SEQ), _to_batch_major(kv, DEC_SEQ),
                         cache_k_win, cache_v_win, b_sinks, jb, name=f"s_attn{jb}")
        return _to_time_major(o)

    xs = _to_time_major(x_sample)
    conv_prev_s = [_to_time_major(state_conv_a[l]) for l in range(N_A_LAYERS)]
    ffn_prev_s = [_to_time_major(state_ffn_conv[l]) for l in range(DEPTH)]
    ys, us_s, tails_s, kv_s = _trunk(xs, conv_prev_s, ffn_prev_s, params,
                                     tag="s", tm=SAMPLE_ROWS, stride=DEC_BATCH, tiles_per_seq=1,
                                     conv_tail_rows=SAMPLE_ROWS, attention=attn_s,
                                     bf16_weights=bf16_weights, save_w=True)
    y_sample = _to_batch_major(ys, DEC_SEQ)
    conv_a_sample = jnp.stack([
        _to_batch_major(jnp.concatenate([conv_prev_s[l][SAMPLE_ROWS:], us_s[l][0]], axis=0), CONV_A_WIDTH - 1)
        for l in range(N_A_LAYERS)])
    ffn_conv_sample = jnp.stack([_to_batch_major(t[0], FFN_CONV_WIDTH - 1) for t in tails_s])
    kv_s3 = _to_batch_major(kv_s, DEC_SEQ)
    k_new = kv_s3[..., :KV_WIDTH].reshape(DEC_BATCH, DEC_SEQ, N_KV_HEADS, HEAD_DIM)
    v_new = kv_s3[..., KV_WIDTH:].reshape(DEC_BATCH, DEC_SEQ, N_KV_HEADS, HEAD_DIM)
    k_win_sample = jnp.concatenate([cache_k_win[:, DEC_SEQ:], k_new], axis=1)
    v_win_sample = jnp.concatenate([cache_v_win[:, DEC_SEQ:], v_new], axis=1)

    def attn_p(x, kv, jb):
        q = _matmul(x, bf16_weights["q", jb], jb, tm=PROMPT_TM, tn=2 * GROUP * HEAD_DIM, out_dtype=BF16,
                    scale=q_scale, head_major=True, name=f"p_q{jb}")
        return _attn_prompt(q, kv, b_sinks, alibi_bias, jb, name=f"p_attn{jb}")

    xp = x_prompt.reshape(PROMPT_ROWS, D_MODEL)
    zero_conv = jnp.zeros((CONV_HALO, D_MODEL), F32)
    zero_ffn = jnp.zeros((SUBLANES, D_FF), F32)
    yp, us_p, tails_p, kv_p = _trunk(xp, [zero_conv] * N_A_LAYERS, [zero_ffn] * DEPTH, params,
                                     tag="p", tm=PROMPT_TM, stride=1, tiles_per_seq=SEQ // PROMPT_TM,
                                     conv_tail_rows=CONV_HALO, attention=attn_p,
                                     bf16_weights=bf16_weights, save_w=False)
    y_prompt = yp.reshape(BATCH, SEQ, D_MODEL)
    tps = SEQ // PROMPT_TM
    conv_a_prompt = jnp.stack([u[tps - 1::tps, CONV_HALO - (CONV_A_WIDTH - 1):] for u in us_p])
    ffn_conv_prompt = jnp.stack([t[tps - 1::tps, SUBLANES - (FFN_CONV_WIDTH - 1):] for t in tails_p])
    kv_p3 = kv_p.reshape(BATCH, SEQ, 2 * KV_WIDTH)[:, SEQ - WINDOW:]
    k_win_prompt = kv_p3[..., :KV_WIDTH].reshape(BATCH, WINDOW, N_KV_HEADS, HEAD_DIM)
    v_win_prompt = kv_p3[..., KV_WIDTH:].reshape(BATCH, WINDOW, N_KV_HEADS, HEAD_DIM)

    return (y_prompt, y_sample, conv_a_prompt, ffn_conv_prompt, k_win_prompt, v_win_prompt,
            conv_a_sample, ffn_conv_sample, k_win_sample, v_win_sample)
```

```python
import functools

import jax
import jax.numpy as jnp
from jax import lax
from jax.experimental import pallas as pl
from jax.experimental.pallas import tpu as pltpu

D_MODEL = 2048
BATCH = 4
SEQ = 2048
DEPTH = 4
DEC_BATCH = 32
DEC_SEQ = 8
N_A_LAYERS = DEPTH // 2
CONV_A_WIDTH = 31
FFN_CONV_WIDTH = 3
D_FF = ((8 * D_MODEL // 3 + 255) // 256) * 256
HEAD_DIM = 64
N_HEADS = D_MODEL // HEAD_DIM
N_KV_HEADS = max(1, N_HEADS // 8)
GROUP = N_HEADS // N_KV_HEADS
KV_WIDTH = N_KV_HEADS * HEAD_DIM
WINDOW = 128
DN_ALPHA = (2.0 * DEPTH) ** 0.25
LN_EPS = 1e-5
NEG_BIG = -1e30
ALIBI_SLOPES = tuple(2.0 ** (-8.0 * (h + 1) / N_HEADS) for h in range(N_HEADS))

BF16 = jnp.bfloat16
F32 = jnp.float32

V7X_VMEM_BYTES = 64 * 1024 * 1024
SUBLANES = 8

PROMPT_ROWS = BATCH * SEQ
SAMPLE_ROWS = DEC_BATCH * DEC_SEQ
PROMPT_TM = 1024
FFN_TF_F32 = 256
FFN_TF_BF16 = 512
FFN_LAST_ROWS = 256
OUT_TM = 512
OUT_TN = 1024
OUT_ROWS = 256
MASK_PENALTY = 1e30
GLU_TN = 256
GLU_STEP_TN = 2 * GLU_TN
GLU_DOT_ROWS = 512
W_SPLIT = 4
WD_SPLIT = 2
CONV_HALO = 32


def _params(semantics, vmem_bytes):
    limit = min(int(vmem_bytes * 1.25) + (4 << 20), V7X_VMEM_BYTES - (6 << 20))
    return pltpu.CompilerParams(dimension_semantics=semantics, vmem_limit_bytes=limit)


def _gelu_exact(z):
    return 0.5 * z * (1.0 + lax.erf(z * (0.5 ** 0.5)))


def _single(shape, index_map):
    return pl.BlockSpec(shape, index_map, pipeline_mode=pl.Buffered(1))


def _layer_block(l, shape, index_map):
    return pl.BlockSpec((None,) + shape, lambda *g: (l,) + index_map(*g))


def _rows3(p):
    return p.reshape(p.shape[0], 1, p.shape[1])


def _k_split_specs(l, k, tn, col_map):
    kq = k // W_SPLIT

    def spec(q):
        return _layer_block(l, (kq, tn), lambda i, j: (q, col_map(j)))

    return [spec(q) for q in range(W_SPLIT)]


def _weight_operand(w, l, k, tn, col_map):
    if w.dtype == BF16:
        return [pl.BlockSpec((k, tn), lambda i, j: (0, col_map(j)))], [w]
    return _k_split_specs(l, k, tn, col_map), [w] * W_SPLIT


def _stage_bf16(w_refs, wb_ref):
    kq = w_refs[0].shape[0]
    for q, w_ref in enumerate(w_refs):
        wb_ref[q * kq:(q + 1) * kq, :] = w_ref[...].astype(BF16)


def _bf16_weight(w_refs, wb_ref, copy_ref=None):
    if len(w_refs) == 1:
        return w_refs[0][...]
    _stage_bf16(w_refs, wb_ref)
    if copy_ref is not None:
        copy_ref[...] = wb_ref[...]
    return wb_ref[...]


def _mm_kernel(x_ref, *refs, scale, head_major, n_w, save_w):
    w_refs, rest = refs[:n_w], list(refs[n_w:])
    o_ref = rest.pop(0)
    copy_ref = rest.pop(0) if save_w else None
    xb_ref = rest.pop(0)
    wb_ref = rest.pop(0) if n_w > 1 else None

    @pl.when(pl.program_id(1) == 0)
    def _():
        xb_ref[...] = x_ref[...].astype(BF16)

    acc = jnp.dot(xb_ref[...], _bf16_weight(w_refs, wb_ref, copy_ref), preferred_element_type=F32)
    if scale != 1.0:
        acc = acc * scale
    if head_major:
        for h in range(o_ref.shape[0]):
            o_ref[h] = acc[:, h * HEAD_DIM:(h + 1) * HEAD_DIM].astype(o_ref.dtype)
    else:
        o_ref[...] = acc.astype(o_ref.dtype)


def _matmul(x, w, l, *, tm, tn, out_dtype, scale=1.0, head_major=False, save_w=False, name):
    m, k = x.shape
    n = w.shape[-1]
    assert not save_w or m == tm
    w_specs, w_args = _weight_operand(w, l, k, tn, lambda j: j)
    vmem = 2 * tm * k * 4 + tm * k * 2 + 2 * k * tn * 4 + 3 * k * tn * 2 + 5 * tm * tn * 4
    if head_major:
        heads = tn // HEAD_DIM
        out_specs = [pl.BlockSpec((heads, tm, HEAD_DIM), lambda i, j: (j, i, 0))]
        out_shape = [jax.ShapeDtypeStruct((n // HEAD_DIM, m, HEAD_DIM), out_dtype)]
    else:
        out_specs = [pl.BlockSpec((tm, tn), lambda i, j: (i, j))]
        out_shape = [jax.ShapeDtypeStruct((m, n), out_dtype)]
    if save_w:
        out_specs.append(pl.BlockSpec((k, tn), lambda i, j: (0, j)))
        out_shape.append(jax.ShapeDtypeStruct((k, n), BF16))
    scratch = [pltpu.VMEM((tm, k), BF16)]
    if len(w_specs) > 1:
        scratch.append(pltpu.VMEM((k, tn), BF16))
    outs = pl.pallas_call(
        functools.partial(_mm_kernel, scale=scale, head_major=head_major, n_w=len(w_specs), save_w=save_w),
        grid=(m // tm, n // tn),
        in_specs=[pl.BlockSpec((tm, k), lambda i, j: (i, 0))] + w_specs,
        out_specs=out_specs,
        out_shape=out_shape,
        scratch_shapes=scratch,
        compiler_params=_params(("arbitrary", "arbitrary"), vmem),
        name=name,
    )(x, *w_args)
    return outs if save_w else outs[0]


def _dwconv_unit_stride(ext_ref, w_ref, bias, c_ref, tm, col, tc):
    sub = lax.broadcasted_iota(jnp.int32, (SUBLANES, tc), 0)
    first = CONV_HALO - (CONV_A_WIDTH - 1)
    n_blocks = tm // SUBLANES
    n_ext = CONV_HALO // SUBLANES + 1
    t_prev = None
    for blk in range(n_blocks + 1):
        base = blk * SUBLANES
        e = [ext_ref[base + a * SUBLANES:base + (a + 1) * SUBLANES, :] if base + (a + 1) * SUBLANES <= CONV_HALO + tm
             else None for a in range(n_ext)]
        t = [None] * SUBLANES
        for p in range(SUBLANES):
            if blk == n_blocks and p == 0:
                continue
            for a in range(n_ext):
                k = a * SUBLANES + p - first
                if 0 <= k < CONV_A_WIDTH:
                    term = w_ref[k * SUBLANES:(k + 1) * SUBLANES, col:col + tc] * e[a]
                    t[p] = term if t[p] is None else t[p] + term
        if blk > 0:
            acc = bias + t_prev[0]
            for p in range(1, SUBLANES):
                mixed = jnp.where(sub >= p, t_prev[p], t[p])
                acc = acc + pltpu.roll(mixed, SUBLANES - p, axis=0)
            c_ref[base - SUBLANES:base, col:col + tc] = acc
        t_prev = t


def _dwconv_aligned_stride(ext_ref, w_ref, bias, c_ref, tm, halo, stride, col, tc):
    first = halo - (CONV_A_WIDTH - 1) * stride
    for blk in range(tm // SUBLANES):
        acc = bias
        for k in range(CONV_A_WIDTH):
            off = first + k * stride + blk * SUBLANES
            acc = acc + w_ref[k * SUBLANES:(k + 1) * SUBLANES, col:col + tc] * ext_ref[off:off + SUBLANES, :]
        c_ref[blk * SUBLANES:(blk + 1) * SUBLANES, col:col + tc] = acc


def _glu_conv_kernel(x_ref, *refs, tm, halo, stride, tiles_per_seq):
    wa_refs, wg_refs = refs[:W_SPLIT], refs[W_SPLIT:2 * W_SPLIT]
    ba_ref, bg_ref, prev_ref, wdw_ref, bdw_ref, c_ref, tail_ref, xb_ref, wab_ref, wgb_ref, ext_ref = (
        refs[2 * W_SPLIT:2 * W_SPLIT + 11])
    i = pl.program_id(0)
    j = pl.program_id(1)
    tc = GLU_TN
    n_sub = c_ref.shape[1] // tc

    @pl.when(j == 0)
    def _():
        xb_ref[...] = x_ref[...].astype(BF16)

    if tiles_per_seq > 1:
        carry_ref = refs[2 * W_SPLIT + 11]

        @pl.when((i == 0) & (j == 0))
        def _():
            carry_ref[...] = jnp.zeros_like(carry_ref)

    _stage_bf16(wa_refs, wab_ref)
    _stage_bf16(wg_refs, wgb_ref)
    rows = min(tm, GLU_DOT_ROWS)
    for sub in range(n_sub):
        col = sub * tc
        ext = ext_ref.at[sub]
        if tiles_per_seq > 1:
            ext[0:halo, :] = jnp.where(i % tiles_per_seq == 0, prev_ref[:, col:col + tc],
                                       carry_ref[j, :, col:col + tc])
        else:
            ext[0:halo, :] = prev_ref[:, col:col + tc]
        for r0 in range(0, tm, rows):
            xb = xb_ref[r0:r0 + rows, :]
            a = jnp.dot(xb, wab_ref[:, col:col + tc], preferred_element_type=F32) + ba_ref[:, col:col + tc]
            g = jnp.dot(xb, wgb_ref[:, col:col + tc], preferred_element_type=F32) + bg_ref[:, col:col + tc]
            ext[halo + r0:halo + r0 + rows, :] = a * jax.nn.sigmoid(g)
        if tiles_per_seq > 1:
            carry_ref[j, :, col:col + tc] = ext[tm:halo + tm, :]
        tail_ref[:, col:col + tc] = ext[halo + tm - tail_ref.shape[0]:halo + tm, :]

    for sub in range(n_sub):
        col = sub * tc
        ext = ext_ref.at[sub]
        bias = jnp.broadcast_to(bdw_ref[:, col:col + tc], (SUBLANES, tc))
        if stride == 1:
            _dwconv_unit_stride(ext, wdw_ref, bias, c_ref, tm, col, tc)
        else:
            _dwconv_aligned_stride(ext, wdw_ref, bias, c_ref, tm, halo, stride, col, tc)


def _glu_conv(x, w_in, b_in, prev, w_dw, b_dw, l, *, tm, tn, stride, tiles_per_seq, tail_rows, name):
    m, k = x.shape
    n = w_in.shape[2] // 2
    nj = n // tn
    ni = m // tm
    halo = prev.shape[0]
    b3 = _rows3(b_in)
    w_rep = jnp.repeat(w_dw, SUBLANES, axis=1)
    kern = functools.partial(_glu_conv_kernel, tm=tm, halo=halo, stride=stride, tiles_per_seq=tiles_per_seq)
    scratch = [pltpu.VMEM((tm, k), BF16), pltpu.VMEM((k, tn), BF16), pltpu.VMEM((k, tn), BF16),
               pltpu.VMEM((tn // GLU_TN, halo + tm, GLU_TN), F32)]
    if tiles_per_seq > 1:
        scratch.append(pltpu.VMEM((nj, halo, tn), F32))
    vmem = (2 * tm * k * 4 + tm * k * 2 + 4 * k * tn * 4 + 2 * k * tn * 2 + 6 * tm * tn * 4
            + 3 * halo * tn * 4 + (halo + tm) * tn * 4 + nj * halo * tn * 4)
    return pl.pallas_call(
        kern,
        grid=(ni, nj),
        in_specs=([pl.BlockSpec((tm, k), lambda i, j: (i, 0))]
                  + _k_split_specs(l, k, tn, lambda j: j)
                  + _k_split_specs(l, k, tn, lambda j: j + nj)
                  + [_layer_block(l, (1, tn), lambda i, j: (0, j)),
                     _layer_block(l, (1, tn), lambda i, j: (0, j + nj)),
                     pl.BlockSpec((halo, tn), lambda i, j: (0, j)),
                     _layer_block(l, (CONV_A_WIDTH * SUBLANES, tn), lambda i, j: (0, j)),
                     _layer_block(l, (1, tn), lambda i, j: (0, j))]),
        out_specs=[pl.BlockSpec((tm, tn), lambda i, j: (i, j)),
                   pl.BlockSpec((None, tail_rows, tn), lambda i, j: (i, 0, j))],
        out_shape=[jax.ShapeDtypeStruct((m, n), F32),
                   jax.ShapeDtypeStruct((ni, tail_rows, n), F32)],
        scratch_shapes=scratch,
        compiler_params=_params(("arbitrary", "arbitrary"), vmem),
        name=name,
    )(x, *([w_in] * (2 * W_SPLIT)), b3, b3, prev, w_rep, _rows3(b_dw))


def _layer_norm(z, g, b):
    mu = jnp.mean(z, axis=-1, keepdims=True)
    zc = z - mu
    var = jnp.mean(zc * zc, axis=-1, keepdims=True)
    return zc * lax.rsqrt(var + LN_EPS) * g + b


def _residual_ln_chunks(x_ref, acc_ref, g_ref, b_ref, o_ref, tn):
    n_c = acc_ref.shape[0]
    d = n_c * tn
    tot = None
    for jj in range(n_c):
        z = DN_ALPHA * x_ref[:, jj * tn:(jj + 1) * tn] + acc_ref[jj]
        acc_ref[jj] = z
        part = jnp.sum(z, axis=-1, keepdims=True)
        tot = part if tot is None else tot + part
    mu = tot * (1.0 / d)
    sq = None
    for jj in range(n_c):
        zc = acc_ref[jj] - mu
        part = jnp.sum(zc * zc, axis=-1, keepdims=True)
        sq = part if sq is None else sq + part
    rs = lax.rsqrt(sq * (1.0 / d) + LN_EPS)
    for jj in range(n_c):
        o_ref[:, jj * tn:(jj + 1) * tn] = ((acc_ref[jj] - mu) * rs * g_ref[:, jj * tn:(jj + 1) * tn]
                                           + b_ref[:, jj * tn:(jj + 1) * tn])


def _mm_res_ln_kernel(a_ref, *refs, swish_norm, has_bias, tn, n_w, save_w):
    w_refs, refs = refs[:n_w], list(refs[n_w:])
    gn_ref, bn_ref = (refs.pop(0), refs.pop(0)) if swish_norm else (None, None)
    bias_ref = refs.pop(0) if has_bias else None
    x_ref, g_ref, b_ref, o_ref = refs[:4]
    refs = refs[4:]
    copy_ref = refs.pop(0) if save_w else None
    acc_ref = refs.pop(0)
    wb_ref = refs.pop(0) if n_w > 1 else None
    j = pl.program_id(1)

    if swish_norm:
        ab_ref = refs.pop(0)

        @pl.when(j == 0)
        def _():
            y = _layer_norm(a_ref[...], gn_ref[...], bn_ref[...])
            ab_ref[...] = (y * jax.nn.sigmoid(y)).astype(BF16)

        lhs = ab_ref[...]
    else:
        lhs = a_ref[...].astype(BF16)

    part = jnp.dot(lhs, _bf16_weight(w_refs, wb_ref, copy_ref), preferred_element_type=F32)
    if has_bias:
        part = part + bias_ref[...]
    acc_ref[j] = part

    @pl.when(j == pl.num_programs(1) - 1)
    def _():
        _residual_ln_chunks(x_ref, acc_ref, g_ref, b_ref, o_ref, tn)


def _matmul_res_ln(a, norm, w, bias, x, g, b, l, jb, *, tm, tn, save_w=False, name):
    m, kdim = a.shape
    n = w.shape[-1]
    nj = n // tn
    assert not save_w or m == tm
    has_bias = bias is not None
    swish_norm = norm is not None
    w_specs, w_args = _weight_operand(w, jb, kdim, tn, lambda j: j)
    in_specs = [pl.BlockSpec((tm, kdim), lambda i, j: (i, 0))] + w_specs
    args = [a] + w_args
    scratch = [pltpu.VMEM((nj, tm, tn), F32)]
    if len(w_specs) > 1:
        scratch.append(pltpu.VMEM((kdim, tn), BF16))
    if swish_norm:
        in_specs += [_layer_block(jb, (1, kdim), lambda i, j: (0, 0)),
                     _layer_block(jb, (1, kdim), lambda i, j: (0, 0))]
        args += [_rows3(norm[0]), _rows3(norm[1])]
        scratch.append(pltpu.VMEM((tm, kdim), BF16))
    if has_bias:
        in_specs.append(_layer_block(jb, (1, tn), lambda i, j: (0, j)))
        args.append(_rows3(bias))
    in_specs += [pl.BlockSpec((tm, n), lambda i, j: (i, 0)),
                 _layer_block(l, (1, n), lambda i, j: (0, 0)),
                 _layer_block(l, (1, n), lambda i, j: (0, 0))]
    args += [x, _rows3(g), _rows3(b)]
    out_specs = [pl.BlockSpec((tm, n), lambda i, j: (i, 0))]
    out_shape = [jax.ShapeDtypeStruct((m, n), F32)]
    if save_w:
        out_specs.append(pl.BlockSpec((kdim, tn), lambda i, j: (0, j)))
        out_shape.append(jax.ShapeDtypeStruct((kdim, n), BF16))
    vmem = (2 * tm * kdim * a.dtype.itemsize + 2 * kdim * tn * 4 + 3 * kdim * tn * 2 + 2 * tm * n * 4
            + 2 * tm * n * 4 + tm * n * 4 + 2 * tm * tn * 4 + (tm * kdim * 2 if swish_norm else 0))
    outs = pl.pallas_call(
        functools.partial(_mm_res_ln_kernel, swish_norm=swish_norm, has_bias=has_bias, tn=tn,
                          n_w=len(w_specs), save_w=save_w),
        grid=(m // tm, nj),
        in_specs=in_specs,
        out_specs=out_specs,
        out_shape=out_shape,
        scratch_shapes=scratch,
        compiler_params=_params(("arbitrary", "arbitrary"), vmem),
        name=name,
    )(*args)
    return outs if save_w else outs[0]


def _mm_res_ln_rows_kernel(a_ref, w_ref, *refs, swish_norm, has_bias, rows):
    refs = list(refs)
    gn_ref, bn_ref = (refs.pop(0), refs.pop(0)) if swish_norm else (None, None)
    bias_ref = refs.pop(0) if has_bias else None
    x_ref, g_ref, b_ref, o_ref = refs
    for r0 in range(0, a_ref.shape[0], rows):
        a = a_ref[r0:r0 + rows, :]
        if swish_norm:
            y = _layer_norm(a, gn_ref[...], bn_ref[...])
            a = y * jax.nn.sigmoid(y)
        f = jnp.dot(a.astype(BF16), w_ref[...], preferred_element_type=F32)
        if has_bias:
            f = f + bias_ref[...]
        o_ref[r0:r0 + rows, :] = _layer_norm(DN_ALPHA * x_ref[r0:r0 + rows, :] + f, g_ref[...], b_ref[...])


def _matmul_res_ln_rows(a, norm, w_bf16, bias, x, g, b, l, jb, *, tm, rows, name):
    m, kdim = a.shape
    n = w_bf16.shape[1]
    has_bias = bias is not None
    swish_norm = norm is not None
    in_specs = [pl.BlockSpec((tm, kdim), lambda i: (i, 0)), pl.BlockSpec((kdim, n), lambda i: (0, 0))]
    args = [a, w_bf16]
    if swish_norm:
        in_specs += [_layer_block(jb, (1, kdim), lambda i: (0, 0)), _layer_block(jb, (1, kdim), lambda i: (0, 0))]
        args += [_rows3(norm[0]), _rows3(norm[1])]
    if has_bias:
        in_specs.append(_layer_block(jb, (1, n), lambda i: (0, 0)))
        args.append(_rows3(bias))
    in_specs += [pl.BlockSpec((tm, n), lambda i: (i, 0)),
                 _layer_block(l, (1, n), lambda i: (0, 0)), _layer_block(l, (1, n), lambda i: (0, 0))]
    args += [x, _rows3(g), _rows3(b)]
    vmem = 2 * tm * kdim * a.dtype.itemsize + 2 * kdim * n * 2 + 4 * tm * n * 4 + 6 * rows * n * 4
    return pl.pallas_call(
        functools.partial(_mm_res_ln_rows_kernel, swish_norm=swish_norm, has_bias=has_bias, rows=rows),
        grid=(m // tm,),
        in_specs=in_specs,
        out_specs=pl.BlockSpec((tm, n), lambda i: (i, 0)),
        out_shape=jax.ShapeDtypeStruct((m, n), F32),
        compiler_params=_params(("arbitrary",), vmem),
        name=name,
    )(*args)


def _ffn_kernel(x_ref, gprev_ref, *refs, tm, halo, stride, tiles_per_seq, n_wu, n_wd, save_w):
    refs = list(refs)
    wg_refs = [refs.pop(0) for _ in range(n_wu)]
    wv_refs = [refs.pop(0) for _ in range(n_wu)]
    wd_refs = [refs.pop(0) for _ in range(n_wd)]
    wdw_ref, bdw_ref, lg_ref, lb_ref, y_ref, tail_ref = [refs.pop(0) for _ in range(6)]
    wg_copy, wv_copy, wd_copy = [refs.pop(0) for _ in range(3)] if save_w else (None, None, None)
    xb_ref = refs.pop(0)
    wgb_ref, wvb_ref = (refs.pop(0), refs.pop(0)) if n_wu > 1 else (None, None)
    wdb_ref = refs.pop(0) if n_wd > 1 else None
    ext_ref = refs.pop(0)
    i = pl.program_id(0)
    j = pl.program_id(1)

    @pl.when(j == 0)
    def _():
        xb_ref[...] = x_ref[...].astype(BF16)
        y_ref[...] = jnp.zeros_like(y_ref)

    if tiles_per_seq > 1:
        carry_ref = refs.pop(0)

        @pl.when((i == 0) & (j == 0))
        def _():
            carry_ref[...] = jnp.zeros_like(carry_ref)

    def body(rows, finish):
        wg = _bf16_weight(wg_refs, wgb_ref, wg_copy)
        wv = _bf16_weight(wv_refs, wvb_ref, wv_copy)
        wd = _bf16_weight(wd_refs, wdb_ref, wd_copy)
        if tiles_per_seq > 1:
            prev = jnp.where(i % tiles_per_seq == 0, gprev_ref[...], carry_ref[j])
        else:
            prev = gprev_ref[...]
        ext_ref[0:halo, :] = prev
        for r0 in range(0, tm, rows):
            xb = xb_ref[r0:r0 + rows, :]
            g = jnp.dot(xb, wg, preferred_element_type=F32)
            v = jnp.dot(xb, wv, preferred_element_type=F32)
            ext_ref[halo + r0:halo + r0 + rows, :] = g
            c = (wdw_ref[2:3, :] * g
                 + wdw_ref[1:2, :] * ext_ref[halo + r0 - stride:halo + r0 - stride + rows, :]
                 + wdw_ref[0:1, :] * ext_ref[halo + r0 - 2 * stride:halo + r0 - 2 * stride + rows, :]
                 + bdw_ref[...])
            h = (_gelu_exact(c) * v).astype(BF16)
            y = y_ref[r0:r0 + rows, :] + jnp.dot(h, wd, preferred_element_type=F32)
            if finish:
                y = _layer_norm(DN_ALPHA * x_ref[r0:r0 + rows, :] + y, lg_ref[...], lb_ref[...])
            y_ref[r0:r0 + rows, :] = y
        g_last = ext_ref[tm:halo + tm, :]
        tail_ref[...] = g_last
        if tiles_per_seq > 1:
            carry_ref[j] = g_last

    last = j == pl.num_programs(1) - 1
    if n_wu == 1 and tm > FFN_LAST_ROWS:
        pl.when(jnp.logical_not(last))(lambda: body(tm, False))
        pl.when(last)(lambda: body(FFN_LAST_ROWS, True))
    else:
        body(tm, False)

        @pl.when(last)
        def _():
            y_ref[...] = _layer_norm(DN_ALPHA * x_ref[...] + y_ref[...], lg_ref[...], lb_ref[...])


def _conv_ffn(x, g_prev, w_up, w_dw, b_dw, w_down, ln_g, ln_b, l, *, tm, tf, stride, tiles_per_seq,
              save_w=False, name):
    m, d = x.shape
    halo = g_prev.shape[0]
    ni = m // tm
    assert not save_w or ni == 1
    if isinstance(w_up, tuple):
        f = w_down.shape[0]
        nj = f // tf
        wg_specs, wg_args = _weight_operand(w_up[0], l, d, tf, lambda j: j)
        wv_specs, wv_args = _weight_operand(w_up[1], l, d, tf, lambda j: j)
        wd_specs, wd_args = [pl.BlockSpec((tf, d), lambda i, j: (j, 0))], [w_down]
    else:
        f = w_down.shape[1]
        nj = f // tf
        tfq = tf // WD_SPLIT
        wg_specs, wg_args = _weight_operand(w_up, l, d, tf, lambda j: j)
        wv_specs, wv_args = _weight_operand(w_up, l, d, tf, lambda j: j + nj)

        def wd_spec(q):
            return _layer_block(l, (tfq, d), lambda i, j: (j * WD_SPLIT + q, 0))

        wd_specs, wd_args = [wd_spec(q) for q in range(WD_SPLIT)], [w_down] * WD_SPLIT
    n_wu, n_wd = len(wg_specs), len(wd_specs)
    kern = functools.partial(_ffn_kernel, tm=tm, halo=halo, stride=stride, tiles_per_seq=tiles_per_seq,
                             n_wu=n_wu, n_wd=n_wd, save_w=save_w)
    out_specs = [pl.BlockSpec((tm, d), lambda i, j: (i, 0)),
                 pl.BlockSpec((None, halo, tf), lambda i, j: (i, 0, j))]
    out_shape = [jax.ShapeDtypeStruct((m, d), F32), jax.ShapeDtypeStruct((ni, halo, f), F32)]
    if save_w:
        out_specs += [pl.BlockSpec((d, tf), lambda i, j: (0, j)), pl.BlockSpec((d, tf), lambda i, j: (0, j)),
                      pl.BlockSpec((tf, d), lambda i, j: (j, 0))]
        out_shape += [jax.ShapeDtypeStruct((d, f), BF16), jax.ShapeDtypeStruct((d, f), BF16),
                      jax.ShapeDtypeStruct((f, d), BF16)]
    scratch = [pltpu.VMEM((tm, d), BF16)]
    if n_wu > 1:
        scratch += [pltpu.VMEM((d, tf), BF16), pltpu.VMEM((d, tf), BF16)]
    if n_wd > 1:
        scratch.append(pltpu.VMEM((tf, d), BF16))
    scratch.append(pltpu.VMEM((halo + tm, tf), F32))
    if tiles_per_seq > 1:
        scratch.append(pltpu.VMEM((nj, halo, tf), F32))
    w_bytes = 4 if n_wu > 1 else 2
    vmem = (tm * d * 4 + 2 * tm * d * 4 + tm * d * 2 + 6 * d * tf * w_bytes + (9 * d * tf * 2 if save_w else 0)
            + (3 * d * tf * 2 if n_wu > 1 else 0) + (halo + tm) * tf * 4 + 6 * tm * tf * 4 + nj * halo * tf * 4)
    outs = pl.pallas_call(
        kern,
        grid=(ni, nj),
        in_specs=([_single((tm, d), lambda i, j: (i, 0)),
                   pl.BlockSpec((halo, tf), lambda i, j: (0, j))]
                  + wg_specs + wv_specs + wd_specs
                  + [_layer_block(l, (FFN_CONV_WIDTH, tf), lambda i, j: (0, j)),
                     _layer_block(l, (1, tf), lambda i, j: (0, j)),
                     _layer_block(l, (1, d), lambda i, j: (0, 0)),
                     _layer_block(l, (1, d), lambda i, j: (0, 0))]),
        out_specs=out_specs,
        out_shape=out_shape,
        scratch_shapes=scratch,
        compiler_params=_params(("arbitrary", "arbitrary"), vmem),
        name=name,
    )(x, g_prev, *wg_args, *wv_args, *wd_args, w_dw, _rows3(b_dw), _rows3(ln_g), _rows3(ln_b))
    if save_w:
        return outs[0], outs[1], ((outs[2], outs[3]), outs[4])
    return outs[0], outs[1]


def _softmax_pv(s, sink, v):
    m = jnp.maximum(jnp.max(s, axis=-1, keepdims=True), sink)
    p = jnp.exp(s - m)
    denom = jnp.sum(p, axis=-1, keepdims=True) + jnp.exp(sink - m)
    o = jnp.dot(p.astype(BF16), v, preferred_element_type=F32)
    return o / denom


def _alibi_bias_table():
    qi = jnp.arange(WINDOW)[:, None]
    kj = jnp.arange(2 * WINDOW)[None, :]
    dist = WINDOW + qi - kj
    band = (dist >= 0) & (dist <= WINDOW)
    slopes = 2.0 ** (-8.0 * jnp.arange(1, N_HEADS + 1, dtype=F32) / N_HEADS)
    pen = slopes[:, None, None] * dist.astype(F32)[None]
    first = band & (kj >= WINDOW)
    table = jnp.stack([jnp.where(first[None], pen, MASK_PENALTY), jnp.where(band[None], pen, MASK_PENALTY)])
    return table.reshape(2, N_HEADS * WINDOW, 2 * WINDOW)


def _attn_prompt_kernel(sinks_ref, q_ref, kvp_ref, kvc_ref, bias_ref, o_ref, *, jb):
    kv_all = jnp.concatenate([kvp_ref[...], kvc_ref[...]], axis=0).astype(BF16)
    for pair in range(N_HEADS // 2):
        outs = []
        for h in (2 * pair, 2 * pair + 1):
            kvh = h // GROUP
            kh = kv_all[:, kvh * HEAD_DIM:(kvh + 1) * HEAD_DIM]
            vh = kv_all[:, KV_WIDTH + kvh * HEAD_DIM:KV_WIDTH + (kvh + 1) * HEAD_DIM]
            s = lax.dot_general(q_ref[h], kh, (((1,), (1,)), ((), ())), preferred_element_type=F32)
            s = s - bias_ref[h * WINDOW:(h + 1) * WINDOW, :]
            outs.append(_softmax_pv(s, sinks_ref[jb, h], vh))
        o_ref[:, pair * 2 * HEAD_DIM:(pair + 1) * 2 * HEAD_DIM] = (
            jnp.concatenate(outs, axis=1).astype(o_ref.dtype))


def _attn_prompt(q, kv, sinks, bias, jb, *, name):
    m = q.shape[1]
    nb = SEQ // WINDOW
    return pl.pallas_call(
        functools.partial(_attn_prompt_kernel, jb=jb),
        grid=(BATCH, nb),
        in_specs=[pl.BlockSpec(memory_space=pltpu.SMEM),
                  pl.BlockSpec((N_HEADS, WINDOW, HEAD_DIM), lambda b, n: (0, b * nb + n, 0)),
                  pl.BlockSpec((WINDOW, 2 * KV_WIDTH), lambda b, n: (b * nb + jnp.maximum(n - 1, 0), 0)),
                  pl.BlockSpec((WINDOW, 2 * KV_WIDTH), lambda b, n: (b * nb + n, 0)),
                  pl.BlockSpec((None, N_HEADS * WINDOW, 2 * WINDOW), lambda b, n: (jnp.minimum(n, 1), 0, 0))],
        out_specs=pl.BlockSpec((WINDOW, D_MODEL), lambda b, n: (b * nb + n, 0)),
        out_shape=jax.ShapeDtypeStruct((m, D_MODEL), BF16),
        compiler_params=_params(("arbitrary", "arbitrary"), 32 << 20),
        name=name,
    )(sinks, q, kv, kv, bias)


def _attn_sample_kernel(sinks_ref, q_ref, kvn_ref, ck_ref, cv_ref, o_ref, *, jb):
    rows = GROUP * DEC_SEQ
    n_keys = WINDOW + DEC_SEQ
    row = lax.broadcasted_iota(jnp.int32, (rows, n_keys), 0)
    kj = lax.broadcasted_iota(jnp.int32, (rows, n_keys), 1)
    head_local = lax.shift_right_logical(row, DEC_SEQ.bit_length() - 1)
    dist = WINDOW + (row & (DEC_SEQ - 1)) - kj
    distf = dist.astype(F32)
    mask = (dist >= 0) & (dist <= WINDOW)
    head_col = lax.shift_right_logical(lax.broadcasted_iota(jnp.int32, (rows, 1), 0), DEC_SEQ.bit_length() - 1)
    kvn = kvn_ref[...]
    pieces = [None] * N_HEADS
    for kvh in range(N_KV_HEADS):
        lo, hi = kvh * HEAD_DIM, (kvh + 1) * HEAD_DIM
        q_st = jnp.concatenate(
            [q_ref[:, (kvh * GROUP + g) * HEAD_DIM:(kvh * GROUP + g + 1) * HEAD_DIM] for g in range(GROUP)],
            axis=0).astype(BF16)
        k_ext = jnp.concatenate([ck_ref[:, lo:hi], kvn[:, lo:hi]], axis=0).astype(BF16)
        v_ext = jnp.concatenate([cv_ref[:, lo:hi], kvn[:, KV_WIDTH + lo:KV_WIDTH + hi]], axis=0).astype(BF16)
        slope = jnp.zeros((rows, n_keys), F32)
        sink = jnp.zeros((rows, 1), F32)
        for g in range(GROUP):
            h = kvh * GROUP + g
            slope = jnp.where(head_local == g, ALIBI_SLOPES[h], slope)
            sink = jnp.where(head_col == g, sinks_ref[jb, h], sink)
        s = lax.dot_general(q_st, k_ext, (((1,), (1,)), ((), ())), preferred_element_type=F32)
        s = jnp.where(mask, s - slope * distf, NEG_BIG)
        o = _softmax_pv(s, sink, v_ext)
        for g in range(GROUP):
            pieces[kvh * GROUP + g] = o[g * DEC_SEQ:(g + 1) * DEC_SEQ, :]
    o_ref[...] = jnp.concatenate(pieces, axis=1)


def _attn_sample(q, kv_new, cache_k, cache_v, sinks, jb, *, name):
    ck = cache_k.reshape(DEC_BATCH, WINDOW, KV_WIDTH)
    cv = cache_v.reshape(DEC_BATCH, WINDOW, KV_WIDTH)
    return pl.pallas_call(
        functools.partial(_attn_sample_kernel, jb=jb),
        grid=(DEC_BATCH,),
        in_specs=[pl.BlockSpec(memory_space=pltpu.SMEM),
                  pl.BlockSpec((None, DEC_SEQ, D_MODEL), lambda b: (b, 0, 0)),
                  pl.BlockSpec((None, DEC_SEQ, 2 * KV_WIDTH), lambda b: (b, 0, 0)),
                  pl.BlockSpec((None, WINDOW, KV_WIDTH), lambda b: (b, 0, 0)),
                  pl.BlockSpec((None, WINDOW, KV_WIDTH), lambda b: (b, 0, 0))],
        out_specs=pl.BlockSpec((None, DEC_SEQ, D_MODEL), lambda b: (b, 0, 0)),
        out_shape=jax.ShapeDtypeStruct((DEC_BATCH, DEC_SEQ, D_MODEL), F32),
        compiler_params=_params(("arbitrary",), 8 << 20),
        name=name,
    )(sinks, q, kv_new, ck, cv)


def _to_time_major(a):
    return jnp.swapaxes(a, 0, 1).reshape(a.shape[1] * a.shape[0], a.shape[2])


def _to_batch_major(a2d, t):
    return jnp.swapaxes(a2d.reshape(t, DEC_BATCH, a2d.shape[1]), 0, 1)


def _trunk(x, conv_prev, ffn_prev, params, *, tag, tm, stride, tiles_per_seq, conv_tail_rows, attention,
           bf16_weights, save_w):
    (a_w_in, a_b_in, a_w_dw, a_b_dw, a_norm_g, a_norm_b, a_w_out, a_b_out, w_kv, b_w_q, b_sinks, b_w_o,
     f_w_up, f_w_dw, f_b_dw, f_w_down, ln_mix_g, ln_mix_b, ln_ffn_g, ln_ffn_b) = params
    u_tails, tails = [], []
    kv = None
    for l in range(DEPTH):
        if l < N_A_LAYERS:
            c, u_tail = _glu_conv(x, a_w_in, a_b_in, conv_prev[l], a_w_dw, a_b_dw, l, tm=tm, tn=GLU_STEP_TN,
                                  stride=stride, tiles_per_seq=tiles_per_seq, tail_rows=conv_tail_rows,
                                  name=f"{tag}_gluconv{l}")
            u_tails.append(u_tail)
            key, w_out, bias, norm, jb = ("aout", l), a_w_out, a_b_out, (a_norm_g, a_norm_b), l
            lhs = c
        else:
            jb = l - N_A_LAYERS
            lhs = attention(x, kv, jb)
            key, w_out, bias, norm = ("bout", jb), b_w_o, None, None
        if save_w:
            x, bf16_weights[key] = _matmul_res_ln(lhs, norm, w_out, bias, x, ln_mix_g, ln_mix_b, l, jb,
                                                  tm=min(OUT_TM, tm), tn=OUT_TN, save_w=True,
                                                  name=f"{tag}_{key[0]}{l}")
        else:
            x = _matmul_res_ln_rows(lhs, norm, bf16_weights[key], bias, x, ln_mix_g, ln_mix_b, l, jb,
                                    tm=OUT_TM, rows=OUT_ROWS, name=f"{tag}_{key[0]}{l}")
        if save_w:
            x, tail, bf16_weights["ffn", l] = _conv_ffn(
                x, ffn_prev[l], f_w_up, f_w_dw, f_b_dw, f_w_down, ln_ffn_g, ln_ffn_b, l, tm=tm, tf=FFN_TF_F32,
                stride=stride, tiles_per_seq=tiles_per_seq, save_w=True, name=f"{tag}_ffn{l}")
        else:
            w_up_b, w_down_b = bf16_weights["ffn", l]
            x, tail = _conv_ffn(x, ffn_prev[l], w_up_b, f_w_dw, f_b_dw, w_down_b, ln_ffn_g, ln_ffn_b, l, tm=tm,
                                tf=FFN_TF_BF16, stride=stride, tiles_per_seq=tiles_per_seq, name=f"{tag}_ffn{l}")
        tails.append(tail)
        if l == N_A_LAYERS - 1:
            kv = _matmul(x, w_kv[None], 0, tm=tm, tn=2 * KV_WIDTH, out_dtype=F32, name=f"{tag}_kv")
    return x, u_tails, tails, kv


def kernel(x_prompt, x_sample, state_conv_a, state_ffn_conv, cache_k_win, cache_v_win, a_w_in, a_b_in, a_w_dw, a_b_dw, a_norm_g, a_norm_b, a_w_out, a_b_out, w_kv, b_w_q, b_sinks, b_w_o, f_w_up, f_w_dw, f_b_dw, f_w_down, ln_mix_g, ln_mix_b, ln_ffn_g, ln_ffn_b):
    params = (a_w_in, a_b_in, a_w_dw, a_b_dw, a_norm_g, a_norm_b, a_w_out, a_b_out, w_kv, b_w_q, b_sinks, b_w_o,
              f_w_up, f_w_dw, f_b_dw, f_w_down, ln_mix_g, ln_mix_b, ln_ffn_g, ln_ffn_b)
    q_scale = HEAD_DIM ** -0.5
    alibi_bias = _alibi_bias_table()
    bf16_weights = {}

    def attn_s(x, kv, jb):
        q, bf16_weights["q", jb] = _matmul(x, b_w_q, jb, tm=SAMPLE_ROWS, tn=GROUP * HEAD_DIM, out_dtype=F32,
                                           scale=q_scale, save_w=True, name=f"s_q{jb}")
        o = _attn_sample(_to_batch_major(q, DEC_SEQ), _to_batch_major(kv, DEC_SEQ),
                         cache_k_win, cache_v_win, b_sinks, jb, name=f"s_attn{jb}")
        return _to_time_major(o)

    xs = _to_time_major(x_sample)
    conv_prev_s = [_to_time_major(state_conv_a[l]) for l in range(N_A_LAYERS)]
    ffn_prev_s = [_to_time_major(state_ffn_conv[l]) for l in range(DEPTH)]
    ys, us_s, tails_s, kv_s = _trunk(xs, conv_prev_s, ffn_prev_s, params,
                                     tag="s", tm=SAMPLE_ROWS, stride=DEC_BATCH, tiles_per_seq=1,
                                     conv_tail_rows=SAMPLE_ROWS, attention=attn_s,
                                     bf16_weights=bf16_weights, save_w=True)
    y_sample = _to_batch_major(ys, DEC_SEQ)
    conv_a_sample = jnp.stack([
        _to_batch_major(jnp.concatenate([conv_prev_s[l][SAMPLE_ROWS:], us_s[l][0]], axis=0), CONV_A_WIDTH - 1)
        for l in range(N_A_LAYERS)])
    ffn_conv_sample = jnp.stack([_to_batch_major(t[0], FFN_CONV_WIDTH - 1) for t in tails_s])
    kv_s3 = _to_batch_major(kv_s, DEC_SEQ)
    k_new = kv_s3[..., :KV_WIDTH].reshape(DEC_BATCH, DEC_SEQ, N_KV_HEADS, HEAD_DIM)
    v_new = kv_s3[..., KV_WIDTH:].reshape(DEC_BATCH, DEC_SEQ, N_KV_HEADS, HEAD_DIM)
    k_win_sample = jnp.concatenate([cache_k_win[:, DEC_SEQ:], k_new], axis=1)
    v_win_sample = jnp.concatenate([cache_v_win[:, DEC_SEQ:], v_new], axis=1)

    def attn_p(x, kv, jb):
        q = _matmul(x, bf16_weights["q", jb], jb, tm=PROMPT_TM, tn=2 * GROUP * HEAD_DIM, out_dtype=BF16,
                    scale=q_scale, head_major=True, name=f"p_q{jb}")
        return _attn_prompt(q, kv, b_sinks, alibi_bias, jb, name=f"p_attn{jb}")

    xp = x_prompt.reshape(PROMPT_ROWS, D_MODEL)
    zero_conv = jnp.zeros((CONV_HALO, D_MODEL), F32)
    zero_ffn = jnp.zeros((SUBLANES, D_FF), F32)
    yp, us_p, tails_p, kv_p = _trunk(xp, [zero_conv] * N_A_LAYERS, [zero_ffn] * DEPTH, params,
                                     tag="p", tm=PROMPT_TM, stride=1, tiles_per_seq=SEQ // PROMPT_TM,
                                     conv_tail_rows=CONV_HALO, attention=attn_p,
                                     bf16_weights=bf16_weights, save_w=False)
    y_prompt = yp.reshape(BATCH, SEQ, D_MODEL)
    tps = SEQ // PROMPT_TM
    conv_a_prompt = jnp.stack([u[tps - 1::tps, CONV_HALO - (CONV_A_WIDTH - 1):] for u in us_p])
    ffn_conv_prompt = jnp.stack([t[tps - 1::tps, SUBLANES - (FFN_CONV_WIDTH - 1):] for t in tails_p])
    kv_p3 = kv_p.reshape(BATCH, SEQ, 2 * KV_WIDTH)[:, SEQ - WINDOW:]
    k_win_prompt = kv_p3[..., :KV_WIDTH].reshape(BATCH, WINDOW, N_KV_HEADS, HEAD_DIM)
    v_win_prompt = kv_p3[..., KV_WIDTH:].reshape(BATCH, WINDOW, N_KV_HEADS, HEAD_DIM)

    return (y_prompt, y_sample, conv_a_prompt, ffn_conv_prompt, k_win_prompt, v_win_prompt,
            conv_a_sample, ffn_conv_sample, k_win_sample, v_win_sample)
```

```python
import functools

import jax
import jax.numpy as jnp
from jax import lax
from jax.experimental import pallas as pl
from jax.experimental.pallas import tpu as pltpu

D_MODEL = 2048
BATCH = 4
SEQ = 2048
DEPTH = 4
DEC_BATCH = 32
DEC_SEQ = 8
N_A_LAYERS = DEPTH // 2
CONV_A_WIDTH = 31
FFN_CONV_WIDTH = 3
D_FF = ((8 * D_MODEL // 3 + 255) // 256) * 256
HEAD_DIM = 64
N_HEADS = D_MODEL // HEAD_DIM
N_KV_HEADS = max(1, N_HEADS // 8)
GROUP = N_HEADS // N_KV_HEADS
KV_WIDTH = N_KV_HEADS * HEAD_DIM
WINDOW = 128
DN_ALPHA = (2.0 * DEPTH) ** 0.25
LN_EPS = 1e-5
NEG_BIG = -1e30
ALIBI_SLOPES = tuple(2.0 ** (-8.0 * (h + 1) / N_HEADS) for h in range(N_HEADS))

BF16 = jnp.bfloat16
F32 = jnp.float32

V7X_VMEM_BYTES = 64 * 1024 * 1024
SUBLANES = 8

PROMPT_ROWS = BATCH * SEQ
SAMPLE_ROWS = DEC_BATCH * DEC_SEQ
PROMPT_TM = 1024
FFN_TF_F32 = 256
FFN_TF_BF16 = 512
OUT_TM = 512
OUT_TN = 1024
OUT_ROWS = 256
MASK_PENALTY = 1e30
GLU_TN = 256
GLU_STEP_TN = 2 * GLU_TN
GLU_DOT_ROWS = 512
W_SPLIT = 4
WD_SPLIT = 2
CONV_HALO = 32


def _params(semantics, vmem_bytes):
    limit = min(int(vmem_bytes * 1.25) + (4 << 20), V7X_VMEM_BYTES - (6 << 20))
    return pltpu.CompilerParams(dimension_semantics=semantics, vmem_limit_bytes=limit)


def _gelu_exact(z):
    return 0.5 * z * (1.0 + lax.erf(z * (0.5 ** 0.5)))


def _single(shape, index_map):
    return pl.BlockSpec(shape, index_map, pipeline_mode=pl.Buffered(1))


def _layer_block(l, shape, index_map):
    return pl.BlockSpec((None,) + shape, lambda *g: (l,) + index_map(*g))


def _rows3(p):
    return p.reshape(p.shape[0], 1, p.shape[1])


def _k_split_specs(l, k, tn, col_map):
    kq = k // W_SPLIT

    def spec(q):
        return _layer_block(l, (kq, tn), lambda i, j: (q, col_map(j)))

    return [spec(q) for q in range(W_SPLIT)]


def _weight_operand(w, l, k, tn, col_map):
    if w.dtype == BF16:
        return [pl.BlockSpec((k, tn), lambda i, j: (0, col_map(j)))], [w]
    return _k_split_specs(l, k, tn, col_map), [w] * W_SPLIT


def _stage_bf16(w_refs, wb_ref):
    kq = w_refs[0].shape[0]
    for q, w_ref in enumerate(w_refs):
        wb_ref[q * kq:(q + 1) * kq, :] = w_ref[...].astype(BF16)


def _bf16_weight(w_refs, wb_ref, copy_ref=None):
    if len(w_refs) == 1:
        return w_refs[0][...]
    _stage_bf16(w_refs, wb_ref)
    if copy_ref is not None:
        copy_ref[...] = wb_ref[...]
    return wb_ref[...]


def _mm_kernel(x_ref, *refs, scale, head_major, n_w, save_w):
    w_refs, rest = refs[:n_w], list(refs[n_w:])
    o_ref = rest.pop(0)
    copy_ref = rest.pop(0) if save_w else None
    xb_ref = rest.pop(0)
    wb_ref = rest.pop(0) if n_w > 1 else None

    @pl.when(pl.program_id(1) == 0)
    def _():
        xb_ref[...] = x_ref[...].astype(BF16)

    acc = jnp.dot(xb_ref[...], _bf16_weight(w_refs, wb_ref, copy_ref), preferred_element_type=F32)
    if scale != 1.0:
        acc = acc * scale
    if head_major:
        for h in range(o_ref.shape[0]):
            o_ref[h] = acc[:, h * HEAD_DIM:(h + 1) * HEAD_DIM].astype(o_ref.dtype)
    else:
        o_ref[...] = acc.astype(o_ref.dtype)


def _matmul(x, w, l, *, tm, tn, out_dtype, scale=1.0, head_major=False, save_w=False, name):
    m, k = x.shape
    n = w.shape[-1]
    assert not save_w or m == tm
    w_specs, w_args = _weight_operand(w, l, k, tn, lambda j: j)
    vmem = 2 * tm * k * 4 + tm * k * 2 + 2 * k * tn * 4 + 3 * k * tn * 2 + 5 * tm * tn * 4
    if head_major:
        heads = tn // HEAD_DIM
        out_specs = [pl.BlockSpec((heads, tm, HEAD_DIM), lambda i, j: (j, i, 0))]
        out_shape = [jax.ShapeDtypeStruct((n // HEAD_DIM, m, HEAD_DIM), out_dtype)]
    else:
        out_specs = [pl.BlockSpec((tm, tn), lambda i, j: (i, j))]
        out_shape = [jax.ShapeDtypeStruct((m, n), out_dtype)]
    if save_w:
        out_specs.append(pl.BlockSpec((k, tn), lambda i, j: (0, j)))
        out_shape.append(jax.ShapeDtypeStruct((k, n), BF16))
    scratch = [pltpu.VMEM((tm, k), BF16)]
    if len(w_specs) > 1:
        scratch.append(pltpu.VMEM((k, tn), BF16))
    outs = pl.pallas_call(
        functools.partial(_mm_kernel, scale=scale, head_major=head_major, n_w=len(w_specs), save_w=save_w),
        grid=(m // tm, n // tn),
        in_specs=[pl.BlockSpec((tm, k), lambda i, j: (i, 0))] + w_specs,
        out_specs=out_specs,
        out_shape=out_shape,
        scratch_shapes=scratch,
        compiler_params=_params(("arbitrary", "arbitrary"), vmem),
        name=name,
    )(x, *w_args)
    return outs if save_w else outs[0]


def _dwconv_unit_stride(ext_ref, w_ref, bias, c_ref, tm, col, tc):
    sub = lax.broadcasted_iota(jnp.int32, (SUBLANES, tc), 0)
    first = CONV_HALO - (CONV_A_WIDTH - 1)
    n_blocks = tm // SUBLANES
    n_ext = CONV_HALO // SUBLANES + 1
    t_prev = None
    for blk in range(n_blocks + 1):
        base = blk * SUBLANES
        e = [ext_ref[base + a * SUBLANES:base + (a + 1) * SUBLANES, :] if base + (a + 1) * SUBLANES <= CONV_HALO + tm
             else None for a in range(n_ext)]
        t = [None] * SUBLANES
        for p in range(SUBLANES):
            if blk == n_blocks and p == 0:
                continue
            for a in range(n_ext):
                k = a * SUBLANES + p - first
                if 0 <= k < CONV_A_WIDTH:
                    term = w_ref[k * SUBLANES:(k + 1) * SUBLANES, col:col + tc] * e[a]
                    t[p] = term if t[p] is None else t[p] + term
        if blk > 0:
            acc = bias + t_prev[0]
            for p in range(1, SUBLANES):
                mixed = jnp.where(sub >= p, t_prev[p], t[p])
                acc = acc + pltpu.roll(mixed, SUBLANES - p, axis=0)
            c_ref[base - SUBLANES:base, col:col + tc] = acc
        t_prev = t


def _dwconv_aligned_stride(ext_ref, w_ref, bias, c_ref, tm, halo, stride, col, tc):
    first = halo - (CONV_A_WIDTH - 1) * stride
    for blk in range(tm // SUBLANES):
        acc = bias
        for k in range(CONV_A_WIDTH):
            off = first + k * stride + blk * SUBLANES
            acc = acc + w_ref[k * SUBLANES:(k + 1) * SUBLANES, col:col + tc] * ext_ref[off:off + SUBLANES, :]
        c_ref[blk * SUBLANES:(blk + 1) * SUBLANES, col:col + tc] = acc


def _glu_conv_kernel(x_ref, *refs, tm, halo, stride, tiles_per_seq):
    wa_refs, wg_refs = refs[:W_SPLIT], refs[W_SPLIT:2 * W_SPLIT]
    ba_ref, bg_ref, prev_ref, wdw_ref, bdw_ref, c_ref, tail_ref, xb_ref, wab_ref, wgb_ref, ext_ref = (
        refs[2 * W_SPLIT:2 * W_SPLIT + 11])
    i = pl.program_id(0)
    j = pl.program_id(1)
    tc = GLU_TN
    n_sub = c_ref.shape[1] // tc

    @pl.when(j == 0)
    def _():
        xb_ref[...] = x_ref[...].astype(BF16)

    if tiles_per_seq > 1:
        carry_ref = refs[2 * W_SPLIT + 11]

        @pl.when((i == 0) & (j == 0))
        def _():
            carry_ref[...] = jnp.zeros_like(carry_ref)

    _stage_bf16(wa_refs, wab_ref)
    _stage_bf16(wg_refs, wgb_ref)
    rows = min(tm, GLU_DOT_ROWS)
    for sub in range(n_sub):
        col = sub * tc
        ext = ext_ref.at[sub]
        if tiles_per_seq > 1:
            ext[0:halo, :] = jnp.where(i % tiles_per_seq == 0, prev_ref[:, col:col + tc],
                                       carry_ref[j, :, col:col + tc])
        else:
            ext[0:halo, :] = prev_ref[:, col:col + tc]
        for r0 in range(0, tm, rows):
            xb = xb_ref[r0:r0 + rows, :]
            a = jnp.dot(xb, wab_ref[:, col:col + tc], preferred_element_type=F32) + ba_ref[:, col:col + tc]
            g = jnp.dot(xb, wgb_ref[:, col:col + tc], preferred_element_type=F32) + bg_ref[:, col:col + tc]
            ext[halo + r0:halo + r0 + rows, :] = a * jax.nn.sigmoid(g)
        if tiles_per_seq > 1:
            carry_ref[j, :, col:col + tc] = ext[tm:halo + tm, :]
        tail_ref[:, col:col + tc] = ext[halo + tm - tail_ref.shape[0]:halo + tm, :]

    for sub in range(n_sub):
        col = sub * tc
        ext = ext_ref.at[sub]
        bias = jnp.broadcast_to(bdw_ref[:, col:col + tc], (SUBLANES, tc))
        if stride == 1:
            _dwconv_unit_stride(ext, wdw_ref, bias, c_ref, tm, col, tc)
        else:
            _dwconv_aligned_stride(ext, wdw_ref, bias, c_ref, tm, halo, stride, col, tc)


def _glu_conv(x, w_in, b_in, prev, w_dw, b_dw, l, *, tm, tn, stride, tiles_per_seq, tail_rows, name):
    m, k = x.shape
    n = w_in.shape[2] // 2
    nj = n // tn
    ni = m // tm
    halo = prev.shape[0]
    b3 = _rows3(b_in)
    w_rep = jnp.repeat(w_dw, SUBLANES, axis=1)
    kern = functools.partial(_glu_conv_kernel, tm=tm, halo=halo, stride=stride, tiles_per_seq=tiles_per_seq)
    scratch = [pltpu.VMEM((tm, k), BF16), pltpu.VMEM((k, tn), BF16), pltpu.VMEM((k, tn), BF16),
               pltpu.VMEM((tn // GLU_TN, halo + tm, GLU_TN), F32)]
    if tiles_per_seq > 1:
        scratch.append(pltpu.VMEM((nj, halo, tn), F32))
    vmem = (2 * tm * k * 4 + tm * k * 2 + 4 * k * tn * 4 + 2 * k * tn * 2 + 6 * tm * tn * 4
            + 3 * halo * tn * 4 + (halo + tm) * tn * 4 + nj * halo * tn * 4)
    return pl.pallas_call(
        kern,
        grid=(ni, nj),
        in_specs=([pl.BlockSpec((tm, k), lambda i, j: (i, 0))]
                  + _k_split_specs(l, k, tn, lambda j: j)
                  + _k_split_specs(l, k, tn, lambda j: j + nj)
                  + [_layer_block(l, (1, tn), lambda i, j: (0, j)),
                     _layer_block(l, (1, tn), lambda i, j: (0, j + nj)),
                     pl.BlockSpec((halo, tn), lambda i, j: (0, j)),
                     _layer_block(l, (CONV_A_WIDTH * SUBLANES, tn), lambda i, j: (0, j)),
                     _layer_block(l, (1, tn), lambda i, j: (0, j))]),
        out_specs=[pl.BlockSpec((tm, tn), lambda i, j: (i, j)),
                   pl.BlockSpec((None, tail_rows, tn), lambda i, j: (i, 0, j))],
        out_shape=[jax.ShapeDtypeStruct((m, n), F32),
                   jax.ShapeDtypeStruct((ni, tail_rows, n), F32)],
        scratch_shapes=scratch,
        compiler_params=_params(("arbitrary", "arbitrary"), vmem),
        name=name,
    )(x, *([w_in] * (2 * W_SPLIT)), b3, b3, prev, w_rep, _rows3(b_dw))


def _layer_norm(z, g, b):
    mu = jnp.mean(z, axis=-1, keepdims=True)
    zc = z - mu
    var = jnp.mean(zc * zc, axis=-1, keepdims=True)
    return zc * lax.rsqrt(var + LN_EPS) * g + b


def _residual_ln_chunks(x_ref, acc_ref, g_ref, b_ref, o_ref, tn):
    n_c = acc_ref.shape[0]
    d = n_c * tn
    tot = None
    for jj in range(n_c):
        z = DN_ALPHA * x_ref[:, jj * tn:(jj + 1) * tn] + acc_ref[jj]
        acc_ref[jj] = z
        part = jnp.sum(z, axis=-1, keepdims=True)
        tot = part if tot is None else tot + part
    mu = tot * (1.0 / d)
    sq = None
    for jj in range(n_c):
        zc = acc_ref[jj] - mu
        part = jnp.sum(zc * zc, axis=-1, keepdims=True)
        sq = part if sq is None else sq + part
    rs = lax.rsqrt(sq * (1.0 / d) + LN_EPS)
    for jj in range(n_c):
        o_ref[:, jj * tn:(jj + 1) * tn] = ((acc_ref[jj] - mu) * rs * g_ref[:, jj * tn:(jj + 1) * tn]
                                           + b_ref[:, jj * tn:(jj + 1) * tn])


def _mm_res_ln_kernel(a_ref, *refs, swish_norm, has_bias, tn, n_w, save_w):
    w_refs, refs = refs[:n_w], list(refs[n_w:])
    gn_ref, bn_ref = (refs.pop(0), refs.pop(0)) if swish_norm else (None, None)
    bias_ref = refs.pop(0) if has_bias else None
    x_ref, g_ref, b_ref, o_ref = refs[:4]
    refs = refs[4:]
    copy_ref = refs.pop(0) if save_w else None
    acc_ref = refs.pop(0)
    wb_ref = refs.pop(0) if n_w > 1 else None
    j = pl.program_id(1)

    if swish_norm:
        ab_ref = refs.pop(0)

        @pl.when(j == 0)
        def _():
            y = _layer_norm(a_ref[...], gn_ref[...], bn_ref[...])
            ab_ref[...] = (y * jax.nn.sigmoid(y)).astype(BF16)

        lhs = ab_ref[...]
    else:
        lhs = a_ref[...].astype(BF16)

    part = jnp.dot(lhs, _bf16_weight(w_refs, wb_ref, copy_ref), preferred_element_type=F32)
    if has_bias:
        part = part + bias_ref[...]
    acc_ref[j] = part

    @pl.when(j == pl.num_programs(1) - 1)
    def _():
        _residual_ln_chunks(x_ref, acc_ref, g_ref, b_ref, o_ref, tn)


def _matmul_res_ln(a, norm, w, bias, x, g, b, l, jb, *, tm, tn, save_w=False, name):
    m, kdim = a.shape
    n = w.shape[-1]
    nj = n // tn
    assert not save_w or m == tm
    has_bias = bias is not None
    swish_norm = norm is not None
    w_specs, w_args = _weight_operand(w, jb, kdim, tn, lambda j: j)
    in_specs = [pl.BlockSpec((tm, kdim), lambda i, j: (i, 0))] + w_specs
    args = [a] + w_args
    scratch = [pltpu.VMEM((nj, tm, tn), F32)]
    if len(w_specs) > 1:
        scratch.append(pltpu.VMEM((kdim, tn), BF16))
    if swish_norm:
        in_specs += [_layer_block(jb, (1, kdim), lambda i, j: (0, 0)),
                     _layer_block(jb, (1, kdim), lambda i, j: (0, 0))]
        args += [_rows3(norm[0]), _rows3(norm[1])]
        scratch.append(pltpu.VMEM((tm, kdim), BF16))
    if has_bias:
        in_specs.append(_layer_block(jb, (1, tn), lambda i, j: (0, j)))
        args.append(_rows3(bias))
    in_specs += [pl.BlockSpec((tm, n), lambda i, j: (i, 0)),
                 _layer_block(l, (1, n), lambda i, j: (0, 0)),
                 _layer_block(l, (1, n), lambda i, j: (0, 0))]
    args += [x, _rows3(g), _rows3(b)]
    out_specs = [pl.BlockSpec((tm, n), lambda i, j: (i, 0))]
    out_shape = [jax.ShapeDtypeStruct((m, n), F32)]
    if save_w:
        out_specs.append(pl.BlockSpec((kdim, tn), lambda i, j: (0, j)))
        out_shape.append(jax.ShapeDtypeStruct((kdim, n), BF16))
    vmem = (2 * tm * kdim * a.dtype.itemsize + 2 * kdim * tn * 4 + 3 * kdim * tn * 2 + 2 * tm * n * 4
            + 2 * tm * n * 4 + tm * n * 4 + 2 * tm * tn * 4 + (tm * kdim * 2 if swish_norm else 0))
    outs = pl.pallas_call(
        functools.partial(_mm_res_ln_kernel, swish_norm=swish_norm, has_bias=has_bias, tn=tn,
                          n_w=len(w_specs), save_w=save_w),
        grid=(m // tm, nj),
        in_specs=in_specs,
        out_specs=out_specs,
        out_shape=out_shape,
        scratch_shapes=scratch,
        compiler_params=_params(("arbitrary", "arbitrary"), vmem),
        name=name,
    )(*args)
    return outs if save_w else outs[0]


def _mm_res_ln_rows_kernel(a_ref, w_ref, *refs, swish_norm, has_bias, rows):
    refs = list(refs)
    gn_ref, bn_ref = (refs.pop(0), refs.pop(0)) if swish_norm else (None, None)
    bias_ref = refs.pop(0) if has_bias else None
    x_ref, g_ref, b_ref, o_ref = refs
    for r0 in range(0, a_ref.shape[0], rows):
        a = a_ref[r0:r0 + rows, :]
        if swish_norm:
            y = _layer_norm(a, gn_ref[...], bn_ref[...])
            a = y * jax.nn.sigmoid(y)
        f = jnp.dot(a.astype(BF16), w_ref[...], preferred_element_type=F32)
        if has_bias:
            f = f + bias_ref[...]
        o_ref[r0:r0 + rows, :] = _layer_norm(DN_ALPHA * x_ref[r0:r0 + rows, :] + f, g_ref[...], b_ref[...])


def _matmul_res_ln_rows(a, norm, w_bf16, bias, x, g, b, l, jb, *, tm, rows, name):
    m, kdim = a.shape
    n = w_bf16.shape[1]
    has_bias = bias is not None
    swish_norm = norm is not None
    in_specs = [pl.BlockSpec((tm, kdim), lambda i: (i, 0)), pl.BlockSpec((kdim, n), lambda i: (0, 0))]
    args = [a, w_bf16]
    if swish_norm:
        in_specs += [_layer_block(jb, (1, kdim), lambda i: (0, 0)), _layer_block(jb, (1, kdim), lambda i: (0, 0))]
        args += [_rows3(norm[0]), _rows3(norm[1])]
    if has_bias:
        in_specs.append(_layer_block(jb, (1, n), lambda i: (0, 0)))
        args.append(_rows3(bias))
    in_specs += [pl.BlockSpec((tm, n), lambda i: (i, 0)),
                 _layer_block(l, (1, n), lambda i: (0, 0)), _layer_block(l, (1, n), lambda i: (0, 0))]
    args += [x, _rows3(g), _rows3(b)]
    vmem = 2 * tm * kdim * a.dtype.itemsize + 2 * kdim * n * 2 + 4 * tm * n * 4 + 6 * rows * n * 4
    return pl.pallas_call(
        functools.partial(_mm_res_ln_rows_kernel, swish_norm=swish_norm, has_bias=has_bias, rows=rows),
        grid=(m // tm,),
        in_specs=in_specs,
        out_specs=pl.BlockSpec((tm, n), lambda i: (i, 0)),
        out_shape=jax.ShapeDtypeStruct((m, n), F32),
        compiler_params=_params(("arbitrary",), vmem),
        name=name,
    )(*args)


def _ffn_kernel(x_ref, gprev_ref, *refs, tm, halo, stride, tiles_per_seq, n_wu, n_wd, save_w):
    refs = list(refs)
    wg_refs = [refs.pop(0) for _ in range(n_wu)]
    wv_refs = [refs.pop(0) for _ in range(n_wu)]
    wd_refs = [refs.pop(0) for _ in range(n_wd)]
    wdw_ref, bdw_ref, lg_ref, lb_ref, y_ref, tail_ref = [refs.pop(0) for _ in range(6)]
    wg_copy, wv_copy, wd_copy = [refs.pop(0) for _ in range(3)] if save_w else (None, None, None)
    xb_ref = refs.pop(0)
    wgb_ref, wvb_ref = (refs.pop(0), refs.pop(0)) if n_wu > 1 else (None, None)
    wdb_ref = refs.pop(0) if n_wd > 1 else None
    ext_ref = refs.pop(0)
    i = pl.program_id(0)
    j = pl.program_id(1)

    @pl.when(j == 0)
    def _():
        xb_ref[...] = x_ref[...].astype(BF16)
        y_ref[...] = jnp.zeros_like(y_ref)

    if tiles_per_seq > 1:
        carry_ref = refs.pop(0)

        @pl.when((i == 0) & (j == 0))
        def _():
            carry_ref[...] = jnp.zeros_like(carry_ref)

    xb = xb_ref[...]
    g = jnp.dot(xb, _bf16_weight(wg_refs, wgb_ref, wg_copy), preferred_element_type=F32)
    v = jnp.dot(xb, _bf16_weight(wv_refs, wvb_ref, wv_copy), preferred_element_type=F32)

    if tiles_per_seq > 1:
        prev = jnp.where(i % tiles_per_seq == 0, gprev_ref[...], carry_ref[j])
    else:
        prev = gprev_ref[...]
    ext_ref[0:halo, :] = prev
    ext_ref[halo:halo + tm, :] = g
    g_last = g[tm - halo:, :]
    tail_ref[...] = g_last
    if tiles_per_seq > 1:
        carry_ref[j] = g_last

    c = (wdw_ref[2:3, :] * g
         + wdw_ref[1:2, :] * ext_ref[halo - stride:halo - stride + tm, :]
         + wdw_ref[0:1, :] * ext_ref[halo - 2 * stride:halo - 2 * stride + tm, :]
         + bdw_ref[...])
    h = (_gelu_exact(c) * v).astype(BF16)
    y_ref[...] += jnp.dot(h, _bf16_weight(wd_refs, wdb_ref, wd_copy), preferred_element_type=F32)

    @pl.when(j == pl.num_programs(1) - 1)
    def _():
        y_ref[...] = _layer_norm(DN_ALPHA * x_ref[...] + y_ref[...], lg_ref[...], lb_ref[...])


def _conv_ffn(x, g_prev, w_up, w_dw, b_dw, w_down, ln_g, ln_b, l, *, tm, tf, stride, tiles_per_seq,
              save_w=False, name):
    m, d = x.shape
    halo = g_prev.shape[0]
    ni = m // tm
    assert not save_w or ni == 1
    if isinstance(w_up, tuple):
        f = w_down.shape[0]
        nj = f // tf
        wg_specs, wg_args = _weight_operand(w_up[0], l, d, tf, lambda j: j)
        wv_specs, wv_args = _weight_operand(w_up[1], l, d, tf, lambda j: j)
        wd_specs, wd_args = [pl.BlockSpec((tf, d), lambda i, j: (j, 0))], [w_down]
    else:
        f = w_down.shape[1]
        nj = f // tf
        tfq = tf // WD_SPLIT
        wg_specs, wg_args = _weight_operand(w_up, l, d, tf, lambda j: j)
        wv_specs, wv_args = _weight_operand(w_up, l, d, tf, lambda j: j + nj)

        def wd_spec(q):
            return _layer_block(l, (tfq, d), lambda i, j: (j * WD_SPLIT + q, 0))

        wd_specs, wd_args = [wd_spec(q) for q in range(WD_SPLIT)], [w_down] * WD_SPLIT
    n_wu, n_wd = len(wg_specs), len(wd_specs)
    kern = functools.partial(_ffn_kernel, tm=tm, halo=halo, stride=stride, tiles_per_seq=tiles_per_seq,
                             n_wu=n_wu, n_wd=n_wd, save_w=save_w)
    out_specs = [_single((tm, d), lambda i, j: (i, 0)),
                 pl.BlockSpec((None, halo, tf), lambda i, j: (i, 0, j))]
    out_shape = [jax.ShapeDtypeStruct((m, d), F32), jax.ShapeDtypeStruct((ni, halo, f), F32)]
    if save_w:
        out_specs += [pl.BlockSpec((d, tf), lambda i, j: (0, j)), pl.BlockSpec((d, tf), lambda i, j: (0, j)),
                      pl.BlockSpec((tf, d), lambda i, j: (j, 0))]
        out_shape += [jax.ShapeDtypeStruct((d, f), BF16), jax.ShapeDtypeStruct((d, f), BF16),
                      jax.ShapeDtypeStruct((f, d), BF16)]
    scratch = [pltpu.VMEM((tm, d), BF16)]
    if n_wu > 1:
        scratch += [pltpu.VMEM((d, tf), BF16), pltpu.VMEM((d, tf), BF16)]
    if n_wd > 1:
        scratch.append(pltpu.VMEM((tf, d), BF16))
    scratch.append(pltpu.VMEM((halo + tm, tf), F32))
    if tiles_per_seq > 1:
        scratch.append(pltpu.VMEM((nj, halo, tf), F32))
    w_bytes = 4 if n_wu > 1 else 2
    vmem = (tm * d * 4 + 2 * tm * d * 4 + tm * d * 2 + 6 * d * tf * w_bytes + (9 * d * tf * 2 if save_w else 0)
            + (3 * d * tf * 2 if n_wu > 1 else 0) + (halo + tm) * tf * 4 + 6 * tm * tf * 4 + nj * halo * tf * 4)
    outs = pl.pallas_call(
        kern,
        grid=(ni, nj),
        in_specs=([pl.BlockSpec((tm, d), lambda i, j: (i, 0)),
                   pl.BlockSpec((halo, tf), lambda i, j: (0, j))]
                  + wg_specs + wv_specs + wd_specs
                  + [_layer_block(l, (FFN_CONV_WIDTH, tf), lambda i, j: (0, j)),
                     _layer_block(l, (1, tf), lambda i, j: (0, j)),
                     _layer_block(l, (1, d), lambda i, j: (0, 0)),
                     _layer_block(l, (1, d), lambda i, j: (0, 0))]),
        out_specs=out_specs,
        out_shape=out_shape,
        scratch_shapes=scratch,
        compiler_params=_params(("arbitrary", "arbitrary"), vmem),
        name=name,
    )(x, g_prev, *wg_args, *wv_args, *wd_args, w_dw, _rows3(b_dw), _rows3(ln_g), _rows3(ln_b))
    if save_w:
        return outs[0], outs[1], ((outs[2], outs[3]), outs[4])
    return outs[0], outs[1]


def _softmax_pv(s, sink, v):
    m = jnp.maximum(jnp.max(s, axis=-1, keepdims=True), sink)
    p = jnp.exp(s - m)
    denom = jnp.sum(p, axis=-1, keepdims=True) + jnp.exp(sink - m)
    o = jnp.dot(p.astype(BF16), v, preferred_element_type=F32)
    return o / denom


def _alibi_bias_table():
    qi = jnp.arange(WINDOW)[:, None]
    kj = jnp.arange(2 * WINDOW)[None, :]
    dist = WINDOW + qi - kj
    band = (dist >= 0) & (dist <= WINDOW)
    slopes = 2.0 ** (-8.0 * jnp.arange(1, N_HEADS + 1, dtype=F32) / N_HEADS)
    pen = slopes[:, None, None] * dist.astype(F32)[None]
    first = band & (kj >= WINDOW)
    table = jnp.stack([jnp.where(first[None], pen, MASK_PENALTY), jnp.where(band[None], pen, MASK_PENALTY)])
    return table.reshape(2, N_HEADS * WINDOW, 2 * WINDOW)


def _attn_prompt_kernel(sinks_ref, q_ref, kvp_ref, kvc_ref, bias_ref, o_ref, *, jb):
    kv_all = jnp.concatenate([kvp_ref[...], kvc_ref[...]], axis=0).astype(BF16)
    for pair in range(N_HEADS // 2):
        outs = []
        for h in (2 * pair, 2 * pair + 1):
            kvh = h // GROUP
            kh = kv_all[:, kvh * HEAD_DIM:(kvh + 1) * HEAD_DIM]
            vh = kv_all[:, KV_WIDTH + kvh * HEAD_DIM:KV_WIDTH + (kvh + 1) * HEAD_DIM]
            s = lax.dot_general(q_ref[h], kh, (((1,), (1,)), ((), ())), preferred_element_type=F32)
            s = s - bias_ref[h * WINDOW:(h + 1) * WINDOW, :]
            outs.append(_softmax_pv(s, sinks_ref[jb, h], vh))
        o_ref[:, pair * 2 * HEAD_DIM:(pair + 1) * 2 * HEAD_DIM] = (
            jnp.concatenate(outs, axis=1).astype(o_ref.dtype))


def _attn_prompt(q, kv, sinks, bias, jb, *, name):
    m = q.shape[1]
    nb = SEQ // WINDOW
    return pl.pallas_call(
        functools.partial(_attn_prompt_kernel, jb=jb),
        grid=(BATCH, nb),
        in_specs=[pl.BlockSpec(memory_space=pltpu.SMEM),
                  pl.BlockSpec((N_HEADS, WINDOW, HEAD_DIM), lambda b, n: (0, b * nb + n, 0)),
                  pl.BlockSpec((WINDOW, 2 * KV_WIDTH), lambda b, n: (b * nb + jnp.maximum(n - 1, 0), 0)),
                  pl.BlockSpec((WINDOW, 2 * KV_WIDTH), lambda b, n: (b * nb + n, 0)),
                  pl.BlockSpec((None, N_HEADS * WINDOW, 2 * WINDOW), lambda b, n: (jnp.minimum(n, 1), 0, 0))],
        out_specs=pl.BlockSpec((WINDOW, D_MODEL), lambda b, n: (b * nb + n, 0)),
        out_shape=jax.ShapeDtypeStruct((m, D_MODEL), BF16),
        compiler_params=_params(("arbitrary", "arbitrary"), 32 << 20),
        name=name,
    )(sinks, q, kv, kv, bias)


def _attn_sample_kernel(sinks_ref, q_ref, kvn_ref, ck_ref, cv_ref, o_ref, *, jb):
    rows = GROUP * DEC_SEQ
    n_keys = WINDOW + DEC_SEQ
    row = lax.broadcasted_iota(jnp.int32, (rows, n_keys), 0)
    kj = lax.broadcasted_iota(jnp.int32, (rows, n_keys), 1)
    head_local = lax.shift_right_logical(row, DEC_SEQ.bit_length() - 1)
    dist = WINDOW + (row & (DEC_SEQ - 1)) - kj
    distf = dist.astype(F32)
    mask = (dist >= 0) & (dist <= WINDOW)
    head_col = lax.shift_right_logical(lax.broadcasted_iota(jnp.int32, (rows, 1), 0), DEC_SEQ.bit_length() - 1)
    kvn = kvn_ref[...]
    pieces = [None] * N_HEADS
    for kvh in range(N_KV_HEADS):
        lo, hi = kvh * HEAD_DIM, (kvh + 1) * HEAD_DIM
        q_st = jnp.concatenate(
            [q_ref[:, (kvh * GROUP + g) * HEAD_DIM:(kvh * GROUP + g + 1) * HEAD_DIM] for g in range(GROUP)],
            axis=0).astype(BF16)
        k_ext = jnp.concatenate([ck_ref[:, lo:hi], kvn[:, lo:hi]], axis=0).astype(BF16)
        v_ext = jnp.concatenate([cv_ref[:, lo:hi], kvn[:, KV_WIDTH + lo:KV_WIDTH + hi]], axis=0).astype(BF16)
        slope = jnp.zeros((rows, n_keys), F32)
        sink = jnp.zeros((rows, 1), F32)
        for g in range(GROUP):
            h = kvh * GROUP + g
            slope = jnp.where(head_local == g, ALIBI_SLOPES[h], slope)
            sink = jnp.where(head_col == g, sinks_ref[jb, h], sink)
        s = lax.dot_general(q_st, k_ext, (((1,), (1,)), ((), ())), preferred_element_type=F32)
        s = jnp.where(mask, s - slope * distf, NEG_BIG)
        o = _softmax_pv(s, sink, v_ext)
        for g in range(GROUP):
            pieces[kvh * GROUP + g] = o[g * DEC_SEQ:(g + 1) * DEC_SEQ, :]
    o_ref[...] = jnp.concatenate(pieces, axis=1)


def _attn_sample(q, kv_new, cache_k, cache_v, sinks, jb, *, name):
    ck = cache_k.reshape(DEC_BATCH, WINDOW, KV_WIDTH)
    cv = cache_v.reshape(DEC_BATCH, WINDOW, KV_WIDTH)
    return pl.pallas_call(
        functools.partial(_attn_sample_kernel, jb=jb),
        grid=(DEC_BATCH,),
        in_specs=[pl.BlockSpec(memory_space=pltpu.SMEM),
                  pl.BlockSpec((None, DEC_SEQ, D_MODEL), lambda b: (b, 0, 0)),
                  pl.BlockSpec((None, DEC_SEQ, 2 * KV_WIDTH), lambda b: (b, 0, 0)),
                  pl.BlockSpec((None, WINDOW, KV_WIDTH), lambda b: (b, 0, 0)),
                  pl.BlockSpec((None, WINDOW, KV_WIDTH), lambda b: (b, 0, 0))],
        out_specs=pl.BlockSpec((None, DEC_SEQ, D_MODEL), lambda b: (b, 0, 0)),
        out_shape=jax.ShapeDtypeStruct((DEC_BATCH, DEC_SEQ, D_MODEL), F32),
        compiler_params=_params(("arbitrary",), 8 << 20),
        name=name,
    )(sinks, q, kv_new, ck, cv)


def _to_time_major(a):
    return jnp.swapaxes(a, 0, 1).reshape(a.shape[1] * a.shape[0], a.shape[2])


def _to_batch_major(a2d, t):
    return jnp.swapaxes(a2d.reshape(t, DEC_BATCH, a2d.shape[1]), 0, 1)


def _trunk(x, conv_prev, ffn_prev, params, *, tag, tm, stride, tiles_per_seq, conv_tail_rows, attention,
           bf16_weights, save_w):
    (a_w_in, a_b_in, a_w_dw, a_b_dw, a_norm_g, a_norm_b, a_w_out, a_b_out, w_kv, b_w_q, b_sinks, b_w_o,
     f_w_up, f_w_dw, f_b_dw, f_w_down, ln_mix_g, ln_mix_b, ln_ffn_g, ln_ffn_b) = params
    u_tails, tails = [], []
    kv = None
    for l in range(DEPTH):
        if l < N_A_LAYERS:
            c, u_tail = _glu_conv(x, a_w_in, a_b_in, conv_prev[l], a_w_dw, a_b_dw, l, tm=tm, tn=GLU_STEP_TN,
                                  stride=stride, tiles_per_seq=tiles_per_seq, tail_rows=conv_tail_rows,
                                  name=f"{tag}_gluconv{l}")
            u_tails.append(u_tail)
            key, w_out, bias, norm, jb = ("aout", l), a_w_out, a_b_out, (a_norm_g, a_norm_b), l
            lhs = c
        else:
            jb = l - N_A_LAYERS
            lhs = attention(x, kv, jb)
            key, w_out, bias, norm = ("bout", jb), b_w_o, None, None
        if save_w:
            x, bf16_weights[key] = _matmul_res_ln(lhs, norm, w_out, bias, x, ln_mix_g, ln_mix_b, l, jb,
                                                  tm=min(OUT_TM, tm), tn=OUT_TN, save_w=True,
                                                  name=f"{tag}_{key[0]}{l}")
        else:
            x = _matmul_res_ln_rows(lhs, norm, bf16_weights[key], bias, x, ln_mix_g, ln_mix_b, l, jb,
                                    tm=OUT_TM, rows=OUT_ROWS, name=f"{tag}_{key[0]}{l}")
        if save_w:
            x, tail, bf16_weights["ffn", l] = _conv_ffn(
                x, ffn_prev[l], f_w_up, f_w_dw, f_b_dw, f_w_down, ln_ffn_g, ln_ffn_b, l, tm=tm, tf=FFN_TF_F32,
                stride=stride, tiles_per_seq=tiles_per_seq, save_w=True, name=f"{tag}_ffn{l}")
        else:
            w_up_b, w_down_b = bf16_weights["ffn", l]
            x, tail = _conv_ffn(x, ffn_prev[l], w_up_b, f_w_dw, f_b_dw, w_down_b, ln_ffn_g, ln_ffn_b, l, tm=tm,
                                tf=FFN_TF_BF16, stride=stride, tiles_per_seq=tiles_per_seq, name=f"{tag}_ffn{l}")
        tails.append(tail)
        if l == N_A_LAYERS - 1:
            kv = _matmul(x, w_kv[None], 0, tm=tm, tn=2 * KV_WIDTH, out_dtype=F32, name=f"{tag}_kv")
    return x, u_tails, tails, kv


def kernel(x_prompt, x_sample, state_conv_a, state_ffn_conv, cache_k_win, cache_v_win, a_w_in, a_b_in, a_w_dw, a_b_dw, a_norm_g, a_norm_b, a_w_out, a_b_out, w_kv, b_w_q, b_sinks, b_w_o, f_w_up, f_w_dw, f_b_dw, f_w_down, ln_mix_g, ln_mix_b, ln_ffn_g, ln_ffn_b):
    params = (a_w_in, a_b_in, a_w_dw, a_b_dw, a_norm_g, a_norm_b, a_w_out, a_b_out, w_kv, b_w_q, b_sinks, b_w_o,
              f_w_up, f_w_dw, f_b_dw, f_w_down, ln_mix_g, ln_mix_b, ln_ffn_g, ln_ffn_b)
    q_scale = HEAD_DIM ** -0.5
    alibi_bias = _alibi_bias_table()
    bf16_weights = {}

    def attn_s(x, kv, jb):
        q, bf16_weights["q", jb] = _matmul(x, b_w_q, jb, tm=SAMPLE_ROWS, tn=GROUP * HEAD_DIM, out_dtype=F32,
                                           scale=q_scale, save_w=True, name=f"s_q{jb}")
        o = _attn_sample(_to_batch_major(q, DEC_SEQ), _to_batch_major(kv, DEC_SEQ),
                         cache_k_win, cache_v_win, b_sinks, jb, name=f"s_attn{jb}")
        return _to_time_major(o)

    xs = _to_time_major(x_sample)
    conv_prev_s = [_to_time_major(state_conv_a[l]) for l in range(N_A_LAYERS)]
    ffn_prev_s = [_to_time_major(state_ffn_conv[l]) for l in range(DEPTH)]
    ys, us_s, tails_s, kv_s = _trunk(xs, conv_prev_s, ffn_prev_s, params,
                                     tag="s", tm=SAMPLE_ROWS, stride=DEC_BATCH, tiles_per_seq=1,
                                     conv_tail_rows=SAMPLE_ROWS, attention=attn_s,
                                     bf16_weights=bf16_weights, save_w=True)
    y_sample = _to_batch_major(ys, DEC_SEQ)
    conv_a_sample = jnp.stack([
        _to_batch_major(jnp.concatenate([conv_prev_s[l][SAMPLE_ROWS:], us_s[l][0]], axis=0), CONV_A_WIDTH - 1)
        for l in range(N_A_LAYERS)])
    ffn_conv_sample = jnp.stack([_to_batch_major(t[0], FFN_CONV_WIDTH - 1) for t in tails_s])
    kv_s3 = _to_batch_major(kv_s, DEC_SEQ)
    k_new = kv_s3[..., :KV_WIDTH].reshape(DEC_BATCH, DEC_SEQ, N_KV_HEADS, HEAD_DIM)
    v_new = kv_s3[..., KV_WIDTH:].reshape(DEC_BATCH, DEC_SEQ, N_KV_HEADS, HEAD_DIM)
    k_win_sample = jnp.concatenate([cache_k_win[:, DEC_SEQ:], k_new], axis=1)
    v_win_sample = jnp.concatenate([cache_v_win[:, DEC_SEQ:], v_new], axis=1)

    def attn_p(x, kv, jb):
        q = _matmul(x, bf16_weights["q", jb], jb, tm=PROMPT_TM, tn=2 * GROUP * HEAD_DIM, out_dtype=BF16,
                    scale=q_scale, head_major=True, name=f"p_q{jb}")
        return _attn_prompt(q, kv, b_sinks, alibi_bias, jb, name=f"p_attn{jb}")

    xp = x_prompt.reshape(PROMPT_ROWS, D_MODEL)
    zero_conv = jnp.zeros((CONV_HALO, D_MODEL), F32)
    zero_ffn = jnp.zeros((SUBLANES, D_FF), F32)
    yp, us_p, tails_p, kv_p = _trunk(xp, [zero_conv] * N_A_LAYERS, [zero_ffn] * DEPTH, params,
                                     tag="p", tm=PROMPT_TM, stride=1, tiles_per_seq=SEQ // PROMPT_TM,
                                     conv_tail_rows=CONV_HALO, attention=attn_p,
                                     bf16_weights=bf16_weights, save_w=False)
    y_prompt = yp.reshape(BATCH, SEQ, D_MODEL)
    tps = SEQ // PROMPT_TM
    conv_a_prompt = jnp.stack([u[tps - 1::tps, CONV_HALO - (CONV_A_WIDTH - 1):] for u in us_p])
    ffn_conv_prompt = jnp.stack([t[tps - 1::tps, SUBLANES - (FFN_CONV_WIDTH - 1):] for t in tails_p])
    kv_p3 = kv_p.reshape(BATCH, SEQ, 2 * KV_WIDTH)[:, SEQ - WINDOW:]
    k_win_prompt = kv_p3[..., :KV_WIDTH].reshape(BATCH, WINDOW, N_KV_HEADS, HEAD_DIM)
    v_win_prompt = kv_p3[..., KV_WIDTH:].reshape(BATCH, WINDOW, N_KV_HEADS, HEAD_DIM)

    return (y_prompt, y_sample, conv_a_prompt, ffn_conv_prompt, k_win_prompt, v_win_prompt,
            conv_a_sample, ffn_conv_sample, k_win_sample, v_win_sample)
```
